```python
import math
import jax
import jax.numpy as jnp
from jax import lax
import numpy as np

D_MODEL = 1024
BATCH = 4
SEQ = 4096
DEPTH = 2

GRID_W = 64
CTX_LEN = 256
EPS = 1e-6
ROPE_BASE = 10000.0

DA_HEADS = 8
DA_DIM = 64
DA_VDIM = 2 * DA_DIM
DA_QK = DA_HEADS * 2 * DA_DIM
DA_W = DA_HEADS * DA_VDIM
Q_BLOCK = 128

DN_HEADS = 8
DN_DK = 128
DN_DV = 128
DN_QK = DN_HEADS * DN_DK
DN_W = DN_HEADS * DN_DV
DN_QKV = 2 * DN_QK + DN_W
DN_CONV = 5
DN_CHUNK = 64

IN_SIZES = (DA_QK, DA_QK, DA_W, DN_QKV, DN_W, 2 * DN_HEADS, 2 * DN_HEADS, D_MODEL, D_MODEL)
N_IN = sum(IN_SIZES)

PK_HEADS = 8
N_KEYS = 128
N_EXPERTS = N_KEYS * N_KEYS
PK_TOPK = 16
PK_DQ = 256
PK_HALF = PK_DQ // 2
TOK_BLOCK = 128

kernel_name = 'hybrid_diffattn_gdn_peer_dit'


def _rmsnorm(x, g):
    xf = x.astype(jnp.float32)
    y = xf * lax.rsqrt(jnp.mean(xf * xf, axis=-1, keepdims=True) + EPS)
    return (y * g.astype(jnp.float32)).astype(x.dtype)


def _modulate(x, g, shift, scale):
    return _rmsnorm(x, g) * (1 + scale) + shift


def _l2norm(x):
    xf = x.astype(jnp.float32)
    return xf * lax.rsqrt(jnp.sum(xf * xf, axis=-1, keepdims=True) + EPS)


def _split_cols(p):
    outs, start = [], 0
    for n in IN_SIZES:
        outs.append(p[..., start:start + n])
        start += n
    return outs


def _axial_rope(n_rows, dim):
    row = jnp.repeat(jnp.arange(n_rows, dtype=jnp.float32), GRID_W)
    col = jnp.tile(jnp.arange(GRID_W, dtype=jnp.float32), n_rows)
    n_freq = dim // 4
    inv = ROPE_BASE ** (-jnp.arange(n_freq, dtype=jnp.float32) / n_freq)
    ang = jnp.concatenate([row[:, None] * inv, col[:, None] * inv], axis=-1)
    return jnp.cos(ang), jnp.sin(ang)


def _apply_rope(x, cos, sin):
    x1 = x[..., 0::2].astype(jnp.float32)
    x2 = x[..., 1::2].astype(jnp.float32)
    cs = cos[None, :, None, None, :]
    sn = sin[None, :, None, None, :]
    y = jnp.stack([x1 * cs - x2 * sn, x1 * sn + x2 * cs], axis=-1).reshape(x.shape)
    return y.astype(x.dtype)


def _diff_attention(ql, kl, vl, qc, kc, vc, lam, subln, cos, sin, lam_init, need_ctx):
    B, S = ql.shape[:2]
    Lc = qc.shape[1]
    rs = lambda a: a.reshape(a.shape[0], a.shape[1], DA_HEADS, 2, DA_DIM)
    ql = _apply_rope(rs(ql), cos, sin)
    kl = _apply_rope(rs(kl), cos, sin)
    qc, kc = rs(qc), rs(kc)
    vl = vl.reshape(B, S, DA_HEADS, DA_VDIM)
    vc = vc.reshape(B, Lc, DA_HEADS, DA_VDIM)
    lf = lam.astype(jnp.float32)
    lam_val = jnp.exp(jnp.sum(lf[0] * lf[1])) - jnp.exp(jnp.sum(lf[2] * lf[3])) + lam_init
    k_all = jnp.concatenate([kc, kl], axis=1)
    v_all = jnp.concatenate([vc, vl], axis=1)

    def attend(q_blk, k, v):
        s = jnp.einsum('bqhcd,bkhcd->bhcqk', q_blk, k).astype(jnp.float32) * (DA_DIM ** -0.5)
        p = jax.nn.softmax(s, axis=-1)
        a = (p[:, :, 0] - lam_val * p[:, :, 1]).astype(v.dtype)
        return jnp.einsum('bhqk,bkhe->bqhe', a, v)

    def post(o):
        o = _rmsnorm(o, subln) * (1 - lam_init)
        return o.reshape(o.shape[0], o.shape[1], DA_W)

    nblk = S // Q_BLOCK
    qb = jnp.moveaxis(ql.reshape(B, nblk, Q_BLOCK, DA_HEADS, 2, DA_DIM), 1, 0)
    ol = lax.map(lambda q_blk: attend(q_blk, k_all, v_all), qb)
    ol = jnp.moveaxis(ol, 0, 1).reshape(B, S, DA_HEADS, DA_VDIM)
    out_l = post(ol)
    out_c = post(attend(qc, kc, vc)) if need_ctx else None
    return out_l, out_c


def _short_conv(x, w):
    C = x.shape[-1]
    pad = DN_CONV // 2
    y = lax.conv_general_dilated(x, w[:, None, :].astype(x.dtype), window_strides=(1,),
                                 padding=((pad, pad),), dimension_numbers=('NWC', 'WIO', 'NWC'),
                                 feature_group_count=C)
    return jax.nn.silu(y)


def _dn_prep(qkv, b_raw, a_raw, conv_w, a_log, dt_bias):
    B, T, _ = qkv.shape
    y = _short_conv(qkv, conv_w)
    q = y[..., :DN_QK].reshape(B, T, DN_HEADS, DN_DK)
    k = y[..., DN_QK:2 * DN_QK].reshape(B, T, DN_HEADS, DN_DK)
    v = y[..., 2 * DN_QK:].reshape(B, T, DN_HEADS, DN_DV).astype(jnp.float32)
    q = _l2norm(q) * (DN_DK ** -0.5)
    k = _l2norm(k)
    tr = lambda a: jnp.transpose(a, (0, 2, 1, 3))
    beta = jax.nn.sigmoid(b_raw.astype(jnp.float32)).reshape(B, T, 2, DN_HEADS).transpose(2, 0, 3, 1)
    a = a_raw.astype(jnp.float32).reshape(B, T, 2, DN_HEADS).transpose(2, 0, 3, 1)
    g = -jnp.exp(a_log.astype(jnp.float32))[:, None, :, None] * jax.nn.softplus(
        a + dt_bias.astype(jnp.float32)[:, None, :, None])
    return tr(q), tr(k), tr(v), beta, g


def _gdn_chunked(q, k, v, g, beta, s0):
    B, H, T, Dk = q.shape
    Dv = v.shape[-1]
    C = DN_CHUNK
    n = T // C
    ch = lambda a: a.reshape(B, H, n, C, *a.shape[3:])
    q, k, v, g, beta = ch(q), ch(k), ch(v), ch(g), ch(beta)
    g = jnp.cumsum(g, axis=-1)
    incl = jnp.tril(jnp.ones((C, C), bool))
    strict = jnp.tril(jnp.ones((C, C), bool), -1)
    decay = jnp.exp(jnp.where(incl, g[..., :, None] - g[..., None, :], -jnp.inf))
    kb = k * beta[..., None]
    lmat = jnp.where(strict, jnp.einsum('bhncd,bhnsd->bhncs', kb, k) * decay, 0.0)
    eye = jnp.eye(C, dtype=jnp.float32)
    tinv = lax.linalg.triangular_solve(eye + lmat, jnp.broadcast_to(eye, lmat.shape),
                                       left_side=True, lower=True)
    u = tinv @ (v * beta[..., None])
    w = tinv @ (kb * jnp.exp(g)[..., None])
    a_intra = jnp.einsum('bhncd,bhnsd->bhncs', q, k) * decay
    q_g = q * jnp.exp(g)[..., None]
    g_last = g[..., -1]
    k_tail = k * jnp.exp(g_last[..., None] - g)[..., None]

    def step(S, xs):
        w_i, u_i, a_i, qg_i, kt_i, gl_i = xs
        v_new = u_i - jnp.einsum('bhck,bhkv->bhcv', w_i, S)
        o = jnp.einsum('bhck,bhkv->bhcv', qg_i, S) + jnp.einsum('bhcs,bhsv->bhcv', a_i, v_new)
        S = S * jnp.exp(gl_i)[..., None, None] + jnp.einsum('bhck,bhcv->bhkv', kt_i, v_new)
        return S, o

    mv = lambda a: jnp.moveaxis(a, 2, 0)
    s_fin, o = lax.scan(step, s0, (mv(w), mv(u), mv(a_intra), mv(q_g), mv(k_tail), mv(g_last)))
    o = jnp.moveaxis(o, 0, 2).reshape(B, H, T, Dv)
    return o, s_fin


def _gated_deltanet(qkv_l, z_l, b_l, a_l, qkv_c, z_c, b_c, a_c, conv_w, a_log, dt_bias, norm_g, need_ctx):
    ql, kl, vl, betal, gl = _dn_prep(qkv_l, b_l, a_l, conv_w, a_log, dt_bias)
    qc, kc, vc, betac, gc = _dn_prep(qkv_c, b_c, a_c, conv_w, a_log, dt_bias)
    B = ql.shape[0]
    s0 = jnp.zeros((B, DN_HEADS, DN_DK, DN_DV), jnp.float32)
    fl = lambda a: jnp.flip(a, axis=2)
    oc_f, sc_f = _gdn_chunked(qc, kc, vc, gc[0], betac[0], s0)
    ol_f, _ = _gdn_chunked(ql, kl, vl, gl[0], betal[0], sc_f)
    oc_b, sc_b = _gdn_chunked(fl(qc), fl(kc), fl(vc), fl(gc[1]), fl(betac[1]), s0)
    ol_b, _ = _gdn_chunked(fl(ql), fl(kl), fl(vl), fl(gl[1]), fl(betal[1]), sc_b)

    def finish(o, z):
        Bo, H, T, _ = o.shape
        o = jnp.transpose(o, (0, 2, 1, 3))
        zf = z.astype(jnp.float32).reshape(Bo, T, H, DN_DV)
        y = _rmsnorm(o, norm_g) * jax.nn.silu(zf)
        return y.reshape(Bo, T, DN_W).astype(z.dtype)

    out_l = finish(ol_f + fl(ol_b), z_l)
    out_c = finish(oc_f + fl(oc_b), z_c) if need_ctx else None
    return out_l, out_c


def _mixer(hl, hc, w_in, lam, subln, conv_w, a_log, dt_bias, dn_g, w_ba, w_bb, w_o, cos, sin, lam_init, need_ctx):
    pl = _split_cols(hl @ w_in)
    pc = _split_cols(hc @ w_in)
    da_l, da_c = _diff_attention(pl[0], pl[1], pl[2], pc[0], pc[1], pc[2], lam, subln, cos, sin,
                                 lam_init, need_ctx)
    dn_l, dn_c = _gated_deltanet(pl[3], pl[4], pl[5], pl[6], pc[3], pc[4], pc[5], pc[6],
                                 conv_w, a_log, dt_bias, dn_g, need_ctx)

    def merge(da, dn, p):
        y = jax.nn.sigmoid(p[7]) * (da @ w_ba) + jax.nn.sigmoid(p[8]) * (dn @ w_bb)
        return y @ w_o

    out_l = merge(da_l, dn_l, pl)
    out_c = merge(da_c, dn_c, pc) if need_ctx else None
    return out_l, out_c


def _peer(h, wq, keys, u_tab, v_tab):
    N, D = h.shape
    q = (h @ wq).reshape(N, PK_HEADS, 2, PK_HALF)
    s = jnp.einsum('nhpd,hpkd->nhpk', q, keys).astype(jnp.float32)
    top_s, top_i = lax.top_k(s, PK_TOPK)
    cand_s = (top_s[:, :, 0, :, None] + top_s[:, :, 1, None, :]).reshape(N, PK_HEADS, PK_TOPK * PK_TOPK)
    cand_i = (top_i[:, :, 0, :, None] * N_KEYS + top_i[:, :, 1, None, :]).reshape(N, PK_HEADS, PK_TOPK * PK_TOPK)
    best_s, best_pos = lax.top_k(cand_s, PK_TOPK)
    idx = jnp.take_along_axis(cand_i, best_pos, axis=-1)
    gate = jax.nn.softmax(best_s, axis=-1)
    nblk = N // TOK_BLOCK

    def block(args):
        hb, ib, gb = args
        ib = ib.reshape(TOK_BLOCK, PK_HEADS * PK_TOPK)
        gb = gb.reshape(TOK_BLOCK, PK_HEADS * PK_TOPK).astype(hb.dtype)
        u = jnp.take(u_tab, ib, axis=0)
        act = jax.nn.gelu(jnp.einsum('tnd,td->tn', u, hb), approximate=False)
        v = jnp.take(v_tab, ib, axis=0)
        return jnp.einsum('tn,tnd->td', gb * act, v)

    out = lax.map(block, (h.reshape(nblk, TOK_BLOCK, D),
                          idx.reshape(nblk, TOK_BLOCK, PK_HEADS, PK_TOPK),
                          gate.reshape(nblk, TOK_BLOCK, PK_HEADS, PK_TOPK)))
    return out.reshape(N, D)


def setup_inputs(seed: int = 0) -> dict:
    key = jax.random.key(seed)
    ks = jax.random.split(key, 24)
    D = D_MODEL
    nrm = lambda k, shape, s: jax.random.normal(k, shape, jnp.float32) * s
    dt = jnp.exp(jax.random.uniform(ks[13], (DEPTH, 2, DN_HEADS), jnp.float32,
                                    math.log(1e-3), math.log(1e-1)))
    return {
        'x': nrm(ks[0], (BATCH, SEQ, D), 1.0),
        'c': nrm(ks[1], (BATCH, D), 1.0),
        'ctx': nrm(ks[2], (BATCH, CTX_LEN, D), 1.0),
        'c_ctx': nrm(ks[3], (D,), 1.0),
        'w_ada': nrm(ks[4], (DEPTH, D, 6 * D), 0.5 * D ** -0.5),
        'b_ada': nrm(ks[5], (DEPTH, 6 * D), 0.02),
        'norm1_g': 1.0 + nrm(ks[6], (DEPTH, D), 0.02),
        'norm2_g': 1.0 + nrm(ks[7], (DEPTH, D), 0.02),
        'w_in': nrm(ks[8], (DEPTH, D, N_IN), D ** -0.5),
        'da_lambda': nrm(ks[9], (DEPTH, 4, DA_DIM), 0.1),
        'da_subln': 1.0 + nrm(ks[10], (DEPTH, DA_VDIM), 0.02),
        'dn_conv': nrm(ks[11], (DEPTH, DN_CONV, DN_QKV), DN_CONV ** -0.5),
        'dn_a_log': jnp.log(jax.random.uniform(ks[12], (DEPTH, 2, DN_HEADS), jnp.float32, 1.0, 16.0)),
        'dn_dt_bias': dt + jnp.log(-jnp.expm1(-dt)),
        'dn_norm': 1.0 + nrm(ks[14], (DEPTH, DN_DV), 0.02),
        'w_branch_a': nrm(ks[15], (DEPTH, DA_W, D), DA_W ** -0.5),
        'w_branch_b': nrm(ks[16], (DEPTH, DN_W, D), DN_W ** -0.5),
        'w_out': nrm(ks[17], (DEPTH, D, D), D ** -0.5),
        'peer_wq': nrm(ks[18], (DEPTH, D, PK_HEADS * PK_DQ), D ** -0.5),
        'peer_keys': nrm(ks[19], (DEPTH, PK_HEADS, 2, N_KEYS, PK_HALF), PK_HALF ** -0.5),
        'peer_u': nrm(ks[20], (DEPTH, N_EXPERTS, D), D ** -0.5),
        'peer_v': nrm(ks[21], (DEPTH, N_EXPERTS, D), PK_HEADS ** -0.5),
        'final_g': 1.0 + nrm(ks[22], (D,), 0.02),
    }


def reference(x, c, ctx, c_ctx, w_ada, b_ada, norm1_g, norm2_g, w_in, da_lambda, da_subln,
              dn_conv, dn_a_log, dn_dt_bias, dn_norm, w_branch_a, w_branch_b, w_out,
              peer_wq, peer_keys, peer_u, peer_v, final_g):
    B, S, D = x.shape
    n_rows = S // GRID_W
    cos, sin = _axial_rope(n_rows, DA_DIM)
    sc = jax.nn.silu(c)
    scc = jax.nn.silu(c_ctx)
    xl, xc = x, ctx
    for layer in range(DEPTH):
        need_ctx = layer < DEPTH - 1
        lam_init = 0.8 - 0.6 * math.exp(-0.3 * layer)
        mod_l = jnp.split((sc @ w_ada[layer] + b_ada[layer])[:, None, :], 6, axis=-1)
        mod_c = jnp.split((scc @ w_ada[layer] + b_ada[layer])[None, None, :], 6, axis=-1)
        hl = _modulate(xl, norm1_g[layer], mod_l[0], mod_l[1])
        hc = _modulate(xc, norm1_g[layer], mod_c[0], mod_c[1])
        yl, yc = _mixer(hl, hc, w_in[layer], da_lambda[layer], da_subln[layer], dn_conv[layer],
                        dn_a_log[layer], dn_dt_bias[layer], dn_norm[layer], w_branch_a[layer],
                        w_branch_b[layer], w_out[layer], cos, sin, lam_init, need_ctx)
        xl = xl + mod_l[2] * yl
        hl = _modulate(xl, norm2_g[layer], mod_l[3], mod_l[4])
        if need_ctx:
            xc = xc + mod_c[2] * yc
            hc = _modulate(xc, norm2_g[layer], mod_c[3], mod_c[4])
            tokens = jnp.concatenate([hl.reshape(-1, D), hc.reshape(-1, D)], axis=0)
            f = _peer(tokens, peer_wq[layer], peer_keys[layer], peer_u[layer], peer_v[layer])
            xl = xl + mod_l[5] * f[:B * S].reshape(B, S, D)
            xc = xc + mod_c[5] * f[B * S:].reshape(xc.shape)
        else:
            f = _peer(hl.reshape(-1, D), peer_wq[layer], peer_keys[layer], peer_u[layer], peer_v[layer])
            xl = xl + mod_l[5] * f.reshape(B, S, D)
    return _rmsnorm(xl, final_g)
```

```python
import functools
import math

import jax
import jax.numpy as jnp
import numpy as np
from jax import lax
from jax.experimental import pallas as pl
from jax.experimental.pallas import tpu as pltpu

F32 = jnp.float32
BF16 = jnp.bfloat16
HIGHEST = lax.Precision.HIGHEST

GRID_W = 64
EPS = 1e-6
ROPE_BASE = 10000.0
DA_HEADS = 8
DA_DIM = 64
DA_VDIM = 2 * DA_DIM
DN_HEADS = 8
DN_DK = 128
DN_DV = 128
DN_CONV = 5
DN_CHUNK = 64
PK_HEADS = 8
N_KEYS = 128
PK_TOPK = 16
PK_HALF = 128

LANES = 128
SUBLANES = 8
VMEM_LIMIT_BYTES = 56 * 1024 * 1024

NT_DIMS = (((1,), (1,)), ((), ()))


def _cparams(*sem):
    return pltpu.CompilerParams(dimension_semantics=sem, vmem_limit_bytes=VMEM_LIMIT_BYTES)


def _pick(n, cands):
    for c in cands:
        if n % c == 0:
            return c
    raise ValueError(f"no tile in {cands} divides {n}")


def _silu(x):
    return x * jax.nn.sigmoid(x)


def _ada_kernel(c_ref, w_ref, b_ref, o_ref):
    sc = _silu(c_ref[...])
    o_ref[0] = jnp.dot(sc, w_ref[0], preferred_element_type=F32, precision=HIGHEST) + b_ref[0]


def _ada(c_all, w_ada, b_ada):
    L, D, N = w_ada.shape
    tn = _pick(N, (1536, 1024, 512, 256, 128))
    return pl.pallas_call(
        _ada_kernel,
        grid=(L, N // tn),
        in_specs=[pl.BlockSpec((SUBLANES, D), lambda l, j: (0, 0)),
                  pl.BlockSpec((1, D, tn), lambda l, j: (l, 0, j)),
                  pl.BlockSpec((1, 1, tn), lambda l, j: (l, 0, j))],
        out_specs=pl.BlockSpec((1, SUBLANES, tn), lambda l, j: (l, 0, j)),
        out_shape=jax.ShapeDtypeStruct((L, SUBLANES, N), F32),
        compiler_params=_cparams("parallel", "parallel"),
        name="ada",
    )(c_all, w_ada, b_ada.reshape(L, 1, N))


def _rms_mod(x, g, m, shift_row, scale_row):
    ms = jnp.mean(x * x, axis=-1, keepdims=True)
    y = x * lax.rsqrt(ms + EPS) * g
    return y * (1.0 + m[scale_row:scale_row + 1]) + m[shift_row:shift_row + 1]


def _modulate_kernel(x_ref, g_ref, m_ref, h_ref):
    h_ref[...] = _rms_mod(x_ref[...], g_ref[...], m_ref[...], 0, 1).astype(BF16)


def _resid_modulate_kernel(x_ref, f_ref, mp_ref, g_ref, m_ref, xo_ref, h_ref):
    x = x_ref[...] + mp_ref[...][5:6] * f_ref[...]
    xo_ref[...] = x
    h_ref[...] = _rms_mod(x, g_ref[...], m_ref[...], 0, 1).astype(BF16)


def _seg_specs(tm, n_ctx_tiles, D):
    tok = pl.BlockSpec((None, tm, D), lambda b, t: (b, t, 0))
    gain = pl.BlockSpec((1, D), lambda b, t: (0, 0))
    mod = pl.BlockSpec((None, None, 6, D), lambda b, t: (b, jnp.where(t >= n_ctx_tiles, 1, 0), 0, 0))
    return tok, gain, mod


def _modulate(xs, g, modsel, n_ctx):
    B, T, D = xs.shape
    tm = _pick(math.gcd(n_ctx, T - n_ctx), (256, 128))
    tok, gain, mod = _seg_specs(tm, n_ctx // tm, D)
    return pl.pallas_call(
        _modulate_kernel, grid=(B, T // tm),
        in_specs=[tok, gain, mod], out_specs=tok,
        out_shape=jax.ShapeDtypeStruct((B, T, D), BF16),
        compiler_params=_cparams("parallel", "parallel"), name="modulate",
    )(xs, g.reshape(1, D), modsel)


def _resid_modulate(xs, f, modsel_prev, g, modsel, n_ctx):
    B, T, D = xs.shape
    tm = _pick(math.gcd(n_ctx, T - n_ctx), (256, 128))
    tok, gain, mod = _seg_specs(tm, n_ctx // tm, D)
    return pl.pallas_call(
        _resid_modulate_kernel, grid=(B, T // tm),
        in_specs=[tok, tok, mod, gain, mod], out_specs=[tok, tok],
        out_shape=[jax.ShapeDtypeStruct((B, T, D), F32), jax.ShapeDtypeStruct((B, T, D), BF16)],
        compiler_params=_cparams("parallel", "parallel"), name="resid_modulate",
    )(xs, f, modsel_prev, g.reshape(1, D), modsel)


def _final_kernel(x_ref, f_ref, mp_ref, g_ref, o_ref):
    x = x_ref[...] + mp_ref[...][5:6] * f_ref[...]
    ms = jnp.mean(x * x, axis=-1, keepdims=True)
    o_ref[...] = x * lax.rsqrt(ms + EPS) * g_ref[...]


def _final(xs, f, modsel_prev, g, n_ctx):
    B, T, D = xs.shape
    S = T - n_ctx
    tm = _pick(math.gcd(n_ctx, S), (256, 128))
    off = n_ctx // tm
    tok_in = pl.BlockSpec((None, tm, D), lambda b, t: (b, t + off, 0))
    return pl.pallas_call(
        _final_kernel, grid=(B, S // tm),
        in_specs=[tok_in, tok_in,
                  pl.BlockSpec((None, None, 6, D), lambda b, t: (b, 1, 0, 0)),
                  pl.BlockSpec((1, D), lambda b, t: (0, 0))],
        out_specs=pl.BlockSpec((None, tm, D), lambda b, t: (b, t, 0)),
        out_shape=jax.ShapeDtypeStruct((B, S, D), F32),
        compiler_params=_cparams("parallel", "parallel"), name="final_norm",
    )(xs, f, modsel_prev, g.reshape(1, D))


def _proj_kernel(h_ref, w_ref, o_ref):
    o_ref[...] = jnp.dot(h_ref[...], w_ref[...], preferred_element_type=F32).astype(o_ref.dtype)


def _proj_rope_kernel(h_ref, w_ref, cos_ref, sin_ref, o_ref):
    acc = jnp.dot(h_ref[...], w_ref[...], preferred_element_type=F32)
    tn = acc.shape[1]
    half = DA_DIM // 2
    lane = lax.broadcasted_iota(jnp.int32, acc.shape, 1)
    partner = jnp.where((lane & (DA_DIM - 1)) < half,
                        pltpu.roll(acc, tn - half, axis=1),
                        pltpu.roll(acc, half, axis=1))
    reps = tn // LANES
    cos = jnp.concatenate([cos_ref[...]] * reps, axis=1)
    sin = jnp.concatenate([sin_ref[...]] * reps, axis=1)
    y = acc * cos + partner * sin
    scale = jnp.where(pl.program_id(0) == 0, DA_DIM ** -0.5, 1.0).astype(F32)
    o_ref[...] = (y * scale).astype(o_ref.dtype)


def _proj_gates_kernel(h_ref, w_ref, p_ref, o_ref):
    acc = jnp.dot(h_ref[...], w_ref[...], preferred_element_type=F32)
    p = p_ref[...]
    lane = lax.broadcasted_iota(jnp.int32, acc.shape, 1)
    beta = jax.nn.sigmoid(acc)
    g = -jnp.exp(p[0:1]) * jax.nn.softplus(acc + p[1:2])
    o_ref[...] = jnp.where(lane < 2 * DN_HEADS, beta, jnp.where(lane < 4 * DN_HEADS, g, 0.0))


def _proj(h, w, *, out_dtype, kernel=_proj_kernel, extra=(), extra_specs=(), name="proj"):
    M, D = h.shape
    N = w.shape[1]
    tm = _pick(M, (512, 256, 128))
    tn = _pick(N, (1024, 512, 256, 128))
    return pl.pallas_call(
        kernel, grid=(N // tn, M // tm),
        in_specs=[pl.BlockSpec((tm, D), lambda j, i: (i, 0)),
                  pl.BlockSpec((D, tn), lambda j, i: (0, j))] + list(extra_specs(tm)),
        out_specs=pl.BlockSpec((tm, tn), lambda j, i: (i, j)),
        out_shape=jax.ShapeDtypeStruct((M, N), out_dtype),
        compiler_params=_cparams("parallel", "parallel"), name=name,
    )(h, w, *extra)


def _attn_kernel(lam_ref, q_ref, k_ref, v_ref, g_ref, o_ref, m_sc, l_sc, acc_sc, *, n_ctx, tk, lam_init):
    tq = q_ref.shape[0]
    T = k_ref.shape[0]
    qi = pl.program_id(2)
    q = q_ref[...]
    lane = lax.broadcasted_iota(jnp.int32, q.shape, 1)
    zero = jnp.zeros_like(q)
    qq = jnp.concatenate([jnp.where(lane < DA_DIM, q, zero), jnp.where(lane >= DA_DIM, q, zero)], axis=0)

    m_sc[...] = jnp.full(m_sc.shape, -jnp.inf, F32)
    l_sc[...] = jnp.zeros(l_sc.shape, F32)
    acc_sc[...] = jnp.zeros(acc_sc.shape, F32)

    def step(kc, vc):
        s = lax.dot_general(qq, kc, NT_DIMS, preferred_element_type=F32)
        m_prev = m_sc[...]
        m_new = jnp.maximum(m_prev, jnp.max(s, axis=-1, keepdims=True))
        alpha = jnp.exp(m_prev - m_new)
        p = jnp.exp(s - jnp.concatenate([m_new] * (s.shape[1] // LANES), axis=1))
        l_sc[...] = alpha * l_sc[...] + jnp.sum(p, axis=-1, keepdims=True)
        acc_sc[...] = alpha * acc_sc[...] + jnp.dot(p.astype(BF16), vc, preferred_element_type=F32)
        m_sc[...] = m_new

    step(k_ref[0:n_ctx, :], v_ref[0:n_ctx, :])

    @pl.when(qi * tq >= n_ctx)
    def _():
        def body(j, carry):
            r0 = pl.multiple_of(n_ctx + j * tk, tk)
            step(k_ref[pl.ds(r0, tk), :], v_ref[pl.ds(r0, tk), :])
            return carry
        lax.fori_loop(0, (T - n_ctx) // tk, body, 0)

    lam = lam_ref[...]
    lam_val = (jnp.exp(jnp.sum(lam[0:1] * lam[1:2], axis=-1, keepdims=True))
               - jnp.exp(jnp.sum(lam[2:3] * lam[3:4], axis=-1, keepdims=True)) + lam_init)
    o_all = acc_sc[...] / l_sc[...]
    o = o_all[0:tq] - lam_val * o_all[tq:2 * tq]
    ms = jnp.mean(o * o, axis=-1, keepdims=True)
    o_ref[...] = (o * lax.rsqrt(ms + EPS) * g_ref[...] * (1.0 - lam_init)).astype(o_ref.dtype)


def _attention(qk, rest, lam, subln, n_ctx, lam_init):
    B, T, _ = qk.shape
    H = DA_HEADS
    S = T - n_ctx
    tq = _pick(math.gcd(n_ctx, S), (256, 128))
    tk = _pick(S, (512, 256, 128))
    kern = functools.partial(_attn_kernel, n_ctx=n_ctx, tk=tk, lam_init=lam_init)
    return pl.pallas_call(
        kern, grid=(B, H, T // tq),
        in_specs=[pl.BlockSpec((4, DA_DIM), lambda b, h, i: (0, 0)),
                  pl.BlockSpec((None, tq, DA_VDIM), lambda b, h, i: (b, i, h)),
                  pl.BlockSpec((None, T, DA_VDIM), lambda b, h, i: (b, 0, H + h)),
                  pl.BlockSpec((None, T, DA_VDIM), lambda b, h, i: (b, 0, h)),
                  pl.BlockSpec((1, DA_VDIM), lambda b, h, i: (0, 0))],
        out_specs=pl.BlockSpec((None, tq, DA_VDIM), lambda b, h, i: (b, i, h)),
        out_shape=jax.ShapeDtypeStruct((B, T, H * DA_VDIM), BF16),
        scratch_shapes=[pltpu.VMEM((2 * tq, LANES), F32), pltpu.VMEM((2 * tq, LANES), F32),
                        pltpu.VMEM((2 * tq, DA_VDIM), F32)],
        compiler_params=_cparams("parallel", "parallel", "arbitrary"), name="diff_attention",
    )(lam, qk, qk, rest, subln.reshape(1, DA_VDIM))


def _dn_prep_kernel(x_ref, w_ref, o_ref, pad_sc, *, n_ctx, rows):
    T = x_ref.shape[0]
    halo = SUBLANES
    cb = pl.program_id(1)
    pad_sc[0:halo, :] = jnp.zeros((halo, LANES), F32)
    pad_sc[halo + T:2 * halo + T, :] = jnp.zeros((halo, LANES), F32)
    pad_sc[halo:halo + T, :] = x_ref[...].astype(F32)
    w = w_ref[...]
    is_qk = cb < 2 * DN_HEADS
    post = jnp.where(cb < DN_HEADS, DN_DK ** -0.5, 1.0).astype(F32)
    pad = DN_CONV // 2

    def body(c, carry):
        r0 = pl.multiple_of(c * rows, rows)
        win = pad_sc[pl.ds(r0, rows + 2 * halo), :]
        t = r0 + lax.broadcasted_iota(jnp.int32, (rows, LANES), 0)
        lo = jnp.where(t < n_ctx, 0, n_ctx)
        hi = jnp.where(t < n_ctx, n_ctx, T)
        y = jnp.zeros((rows, LANES), F32)
        for j in range(DN_CONV):
            d = j - pad
            xs = win[halo + d:halo + d + rows, :]
            ok = (t + d >= lo) & (t + d < hi)
            y = y + jnp.where(ok, xs, 0.0) * w[j:j + 1]
        y = _silu(y)
        yn = y * lax.rsqrt(jnp.sum(y * y, axis=-1, keepdims=True) + EPS) * post
        o_ref[pl.ds(r0, rows), :] = jnp.where(is_qk, yn, y).astype(o_ref.dtype)
        return carry

    lax.fori_loop(0, T // rows, body, 0)


def _dn_prep(rest, conv_w, n_ctx, col0):
    B, T, _ = rest.shape
    nblk = 3 * DN_HEADS
    rows = _pick(T, (256, 128))
    cw = jnp.zeros((SUBLANES, nblk * LANES), F32).at[:DN_CONV].set(conv_w)
    kern = functools.partial(_dn_prep_kernel, n_ctx=n_ctx, rows=rows)
    return pl.pallas_call(
        kern, grid=(B, nblk),
        in_specs=[pl.BlockSpec((None, T, LANES), lambda b, c: (b, 0, col0 // LANES + c)),
                  pl.BlockSpec((SUBLANES, LANES), lambda b, c: (0, c))],
        out_specs=pl.BlockSpec((None, T, LANES), lambda b, c: (b, 0, c)),
        out_shape=jax.ShapeDtypeStruct((B, T, nblk * LANES), BF16),
        scratch_shapes=[pltpu.VMEM((T + 2 * SUBLANES, LANES), F32)],
        compiler_params=_cparams("parallel", "parallel"), name="dn_prep",
    )(rest, cw)


GDN_HEADS_PER_STEP = 2
GDN_INV_PASSES = 3


def _split_dot(a, b, passes):
    a_hi = a.astype(BF16)
    b_hi = b.astype(BF16)
    out = jnp.dot(a_hi, b_hi, preferred_element_type=F32)
    if passes >= 3:
        a_lo = (a - a_hi.astype(F32)).astype(BF16)
        b_lo = (b - b_hi.astype(F32)).astype(BF16)
        out = out + jnp.dot(a_hi, b_lo, preferred_element_type=F32) + jnp.dot(a_lo, b_hi, preferred_element_type=F32)
    return out


def _gdn_kernel(q_ref, k_ref, v_ref, gt_ref, z_ref, ng_ref, o_ref,
                w_sc, qg_sc, kt_sc, a_sc, u_sc, egl_sc, oacc_sc, s_sc, *, n_ctx):
    C = DN_CHUNK
    C2 = 2 * C
    T = q_ref.shape[0]
    nc = T // C
    ncc = n_ctx // C
    hp = pl.program_id(1)

    row = lax.broadcasted_iota(jnp.int32, (C2, C2), 0)
    col = lax.broadcasted_iota(jnp.int32, (C2, C2), 1)
    fwd_row = row < C
    rc_xor = row ^ col
    ahead = (col - row) * jnp.where(fwd_row, 1, -1)
    same_dir = rc_xor < C
    incl = same_dir & (ahead <= 0)
    strict = same_dir & (ahead < 0)
    eye = (row == col).astype(F32)
    mcs = incl.astype(F32)

    def phase1(c, s):
        r0 = pl.multiple_of(c * C, C)
        head = hp * GDN_HEADS_PER_STEP + s
        lo, hi = s * LANES, (s + 1) * LANES
        k = k_ref[pl.ds(r0, C), lo:hi]
        q = q_ref[pl.ds(r0, C), lo:hi]
        v = v_ref[pl.ds(r0, C), lo:hi]
        kk = jnp.concatenate([k, k], axis=0)
        kf = kk.astype(F32)
        qf = jnp.concatenate([q, q], axis=0).astype(F32)
        vf = jnp.concatenate([v, v], axis=0).astype(F32)
        x = gt_ref[pl.ds(r0, C), :]
        x2 = jnp.concatenate([x, x], axis=0)
        bsel = jnp.where(fwd_row, head, DN_HEADS + head)
        gsel = bsel + 2 * DN_HEADS
        beta = jnp.sum(jnp.where(col == bsel, x2, 0.0), axis=-1, keepdims=True)
        glog = jnp.sum(jnp.where(col == gsel, x2, 0.0), axis=-1, keepdims=True)
        gcum = jnp.dot(mcs, jnp.broadcast_to(glog, (C2, C2)), preferred_element_type=F32,
                       precision=HIGHEST)
        decay = jnp.exp(jnp.where(incl, gcum - gcum.T, -jnp.inf))
        kkt = lax.dot_general(kk, kk, NT_DIMS, preferred_element_type=F32)
        qkt = lax.dot_general(qf.astype(BF16), kk, NT_DIMS, preferred_element_type=F32)
        lmat = jnp.where(strict, beta * kkt * decay, 0.0)
        tinv = eye
        for lvl in range(int(math.log2(C))):
            bm = jnp.where((rc_xor >> lvl) == 1, lmat, 0.0)
            tinv = tinv - _split_dot(_split_dot(tinv, bm, GDN_INV_PASSES), tinv, GDN_INV_PASSES)
        eg = jnp.exp(gcum)
        rhs = jnp.concatenate([(vf * beta).astype(BF16), (kf * beta * eg).astype(BF16)], axis=1)
        uw = jnp.dot(tinv.astype(BF16), rhs, preferred_element_type=F32)
        glast = jnp.where(fwd_row, gcum[C - 1:C, :], gcum[C:C + 1, :])
        ktail = kf * jnp.exp(glast - gcum)
        u_sc[s, c] = uw[:, 0:LANES]
        w_sc[s, c] = uw[:, LANES:2 * LANES].astype(BF16)
        qg_sc[s, c] = (qf * eg).astype(BF16)
        a_sc[s, c] = (qkt * decay).astype(BF16)
        kt_sc[s, c] = ktail.T.astype(BF16)
        egl_sc[s, c] = jnp.exp(jnp.concatenate([jnp.broadcast_to(gcum[C - 1:C, :], (4, LANES)),
                                                jnp.broadcast_to(gcum[C:C + 1, :], (4, LANES))], axis=0))

    def p1_body(c, carry):
        for s in range(GDN_HEADS_PER_STEP):
            phase1(c, s)
        return carry

    lax.fori_loop(0, nc, p1_body, 0)

    oacc_sc[...] = jnp.zeros(oacc_sc.shape, F32)
    s_sc[...] = jnp.zeros(s_sc.shape, F32)
    lane_b = lax.broadcasted_iota(jnp.int32, (LANES, LANES), 1)
    zpad = jnp.zeros((C, LANES), BF16)

    def p2_body(i, carry):
        cf = i
        cb = jnp.where(i < ncc, ncc - 1 - i, nc - 1 + ncc - i)
        rf = pl.multiple_of(cf * C, C)
        rb = pl.multiple_of(cb * C, C)
        for s in range(GDN_HEADS_PER_STEP):
            st = s_sc[s]
            lhs1 = jnp.concatenate([
                jnp.concatenate([w_sc[s, cf, 0:C, :], zpad], axis=1),
                jnp.concatenate([zpad, w_sc[s, cb, C:C2, :]], axis=1),
                jnp.concatenate([qg_sc[s, cf, 0:C, :], zpad], axis=1),
                jnp.concatenate([zpad, qg_sc[s, cb, C:C2, :]], axis=1)], axis=0)
            r1 = jnp.dot(lhs1, st.astype(BF16), preferred_element_type=F32)
            vnew = jnp.concatenate([u_sc[s, cf, 0:C, :] - r1[0:C], u_sc[s, cb, C:C2, :] - r1[C:C2]], axis=0)
            ktf = kt_sc[s, cf]
            ktb = kt_sc[s, cb]
            zk = jnp.zeros_like(ktf)
            lhs2 = jnp.concatenate([a_sc[s, cf, 0:C, :], a_sc[s, cb, C:C2, :],
                                    jnp.where(lane_b < C, ktf, zk), jnp.where(lane_b >= C, ktb, zk)], axis=0)
            r2 = jnp.dot(lhs2, vnew.astype(BF16), preferred_element_type=F32)
            oacc_sc[s, pl.ds(rf, C), :] += r1[C2:C2 + C] + r2[0:C]
            oacc_sc[s, pl.ds(rb, C), :] += r1[C2 + C:2 * C2] + r2[C:C2]
            eglf = egl_sc[s, cf]
            eglb = egl_sc[s, cb]
            scale = jnp.concatenate([jnp.broadcast_to(eglf[0:1], (DN_DK, DN_DV)),
                                     jnp.broadcast_to(eglb[4:5], (DN_DK, DN_DV))], axis=0)
            s_sc[s] = st * scale + r2[C2:C2 + 2 * DN_DK]
        return carry

    lax.fori_loop(0, nc, p2_body, 0)

    ng = ng_ref[...]
    rows = _pick(T, (256, 128))

    def fin_body(c, carry):
        r0 = pl.multiple_of(c * rows, rows)
        for s in range(GDN_HEADS_PER_STEP):
            o = oacc_sc[s, pl.ds(r0, rows), :]
            zf = z_ref[pl.ds(r0, rows), s * LANES:(s + 1) * LANES].astype(F32)
            ms = jnp.mean(o * o, axis=-1, keepdims=True)
            o_ref[pl.ds(r0, rows), s * LANES:(s + 1) * LANES] = (o * lax.rsqrt(ms + EPS) * ng * _silu(zf)).astype(o_ref.dtype)
        return carry

    lax.fori_loop(0, T // rows, fin_body, 0)


def _gdn(dnq, gates, rest, z_col0, norm_g, n_ctx):
    B, T, _ = dnq.shape
    H = DN_HEADS
    hps = GDN_HEADS_PER_STEP
    wblk = hps * LANES
    nc = T // DN_CHUNK
    C2 = 2 * DN_CHUNK
    kern = functools.partial(_gdn_kernel, n_ctx=n_ctx)
    nb = H // hps
    once = dict(pipeline_mode=pl.Buffered(1))
    return pl.pallas_call(
        kern, grid=(B, nb),
        in_specs=[pl.BlockSpec((None, T, wblk), lambda b, h: (b, 0, h), **once),
                  pl.BlockSpec((None, T, wblk), lambda b, h: (b, 0, nb + h), **once),
                  pl.BlockSpec((None, T, wblk), lambda b, h: (b, 0, 2 * nb + h), **once),
                  pl.BlockSpec((None, T, LANES), lambda b, h: (b, 0, 0), **once),
                  pl.BlockSpec((None, T, wblk), lambda b, h: (b, 0, z_col0 // wblk + h), **once),
                  pl.BlockSpec((1, DN_DV), lambda b, h: (0, 0))],
        out_specs=pl.BlockSpec((None, T, wblk), lambda b, h: (b, 0, h)),
        out_shape=jax.ShapeDtypeStruct((B, T, H * DN_DV), BF16),
        scratch_shapes=[pltpu.VMEM((hps, nc, C2, LANES), BF16),
                        pltpu.VMEM((hps, nc, C2, LANES), BF16),
                        pltpu.VMEM((hps, nc, LANES, C2), BF16),
                        pltpu.VMEM((hps, nc, C2, C2), BF16),
                        pltpu.VMEM((hps, nc, C2, LANES), F32),
                        pltpu.VMEM((hps, nc, SUBLANES, LANES), F32),
                        pltpu.VMEM((hps, T, LANES), F32),
                        pltpu.VMEM((hps, 2 * DN_DK, DN_DV), F32)],
        compiler_params=_cparams("parallel", "arbitrary"), name="gated_deltanet",
    )(dnq, dnq, dnq, gates, rest, norm_g.reshape(1, DN_DV))


def _merge_kernel(da_ref, dn_ref, ga_ref, gb_ref, x_ref, wa_ref, wb_ref, wo_ref, g_ref, m_ref, xo_ref, h_ref):
    ya = jnp.dot(da_ref[...], wa_ref[...], preferred_element_type=F32)
    yb = jnp.dot(dn_ref[...], wb_ref[...], preferred_element_type=F32)
    y = jax.nn.sigmoid(ga_ref[...].astype(F32)) * ya + jax.nn.sigmoid(gb_ref[...].astype(F32)) * yb
    y2 = jnp.dot(y.astype(BF16), wo_ref[...], preferred_element_type=F32)
    m = m_ref[...]
    x = x_ref[...] + m[2:3] * y2
    xo_ref[...] = x
    h_ref[...] = _rms_mod(x, g_ref[...], m, 3, 4).astype(BF16)


def _merge(da, dn, rest, ga_col0, xs, w_ba, w_bb, w_o, g2, modsel, n_ctx):
    B, T, D = xs.shape
    tm = _pick(math.gcd(n_ctx, T - n_ctx), (256, 128))
    tok, gain, mod = _seg_specs(tm, n_ctx // tm, D)
    wspec = pl.BlockSpec((D, D), lambda b, t: (0, 0))
    ga_blk = ga_col0 // D
    return pl.pallas_call(
        _merge_kernel, grid=(B, T // tm),
        in_specs=[tok, tok,
                  pl.BlockSpec((None, tm, D), lambda b, t: (b, t, ga_blk)),
                  pl.BlockSpec((None, tm, D), lambda b, t: (b, t, ga_blk + 1)),
                  tok, wspec, wspec, wspec, gain, mod],
        out_specs=[tok, tok],
        out_shape=[jax.ShapeDtypeStruct((B, T, D), F32), jax.ShapeDtypeStruct((B, T, D), BF16)],
        compiler_params=_cparams("parallel", "parallel"), name="merge",
    )(da, dn, rest, rest, xs, w_ba, w_bb, w_o, g2.reshape(1, D), modsel)


def _desc_tops(x, n):
    tops = []
    for _ in range(n):
        m = jnp.max(x, axis=0, keepdims=True)
        tops.append(m)
        x = jnp.where(x == m, -jnp.inf, x)
    return tops


def _peer_prep_kernel(h_ref, wq_ref, keys_ref, a_ref, b_ref, kap_ref):
    tm = h_ref.shape[0]
    q = jnp.dot(h_ref[...], wq_ref[...], preferred_element_type=F32).astype(BF16)
    for h in range(PK_HEADS):
        st = [lax.dot_general(keys_ref[2 * h + p], q[:, (2 * h + p) * PK_HALF:(2 * h + p + 1) * PK_HALF],
                              NT_DIMS, preferred_element_type=F32) for p in range(2)]
        ta = _desc_tops(st[0], PK_TOPK + 1)
        tb = _desc_tops(st[1], PK_TOPK + 1)
        tbs = jnp.concatenate(tb[:PK_TOPK], axis=0)
        cand = jnp.concatenate([ta[r] + tbs for r in range(PK_TOPK)], axis=0)
        best = _desc_tops(cand, PK_TOPK + 1)
        mx = best[0]
        zsum = best[0] * 0.0
        for r in range(PK_TOPK):
            zsum = zsum + jnp.exp(best[r] - mx)
        nxt = jnp.maximum(best[PK_TOPK], jnp.maximum(ta[PK_TOPK] + tb[0], ta[0] + tb[PK_TOPK]))
        thr = 0.5 * (best[PK_TOPK - 1] + nxt)
        a_ref[h] = jnp.exp(st[0] - ta[0])
        b_ref[h] = jnp.exp(st[1] - tb[0]) / zsum
        kap_ref[h:h + 1, :] = jnp.exp(thr - mx) / zsum


def _peer_prep(h2, wq, keys):
    M, D = h2.shape
    tm = _pick(M, (256, 128))
    nk = 2 * PK_HEADS
    return pl.pallas_call(
        _peer_prep_kernel, grid=(M // tm,),
        in_specs=[pl.BlockSpec((tm, D), lambda i: (i, 0)),
                  pl.BlockSpec((D, nk * PK_HALF), lambda i: (0, 0)),
                  pl.BlockSpec((nk, N_KEYS, PK_HALF), lambda i: (0, 0, 0))],
        out_specs=[pl.BlockSpec((PK_HEADS, N_KEYS, tm), lambda i: (0, 0, i)),
                   pl.BlockSpec((PK_HEADS, N_KEYS, tm), lambda i: (0, 0, i)),
                   pl.BlockSpec((PK_HEADS, tm), lambda i: (0, i))],
        out_shape=[jax.ShapeDtypeStruct((PK_HEADS, N_KEYS, M), F32),
                   jax.ShapeDtypeStruct((PK_HEADS, N_KEYS, M), F32),
                   jax.ShapeDtypeStruct((PK_HEADS, M), F32)],
        compiler_params=_cparams("parallel"), name="peer_prep",
    )(h2, wq, keys)


def _peer_dense_kernel(h_ref, u_ref, vt_ref, a_ref, b_ref, kap_ref, o_ref, acc_sc, sc_sc, wa_sc, *, sub):
    e = pl.program_id(1)
    tm = h_ref.shape[0]
    eb = u_ref.shape[0]
    ni = eb // N_KEYS

    @pl.when(e == 0)
    def _():
        acc_sc[...] = jnp.zeros(acc_sc.shape, F32)

    for t in range(tm // sub):
        tok = slice(t * sub, (t + 1) * sub)
        sc_sc[...] = lax.dot_general(u_ref[...], h_ref[tok, :], NT_DIMS, preferred_element_type=F32)
        kaps = [jnp.broadcast_to(kap_ref[h:h + 1, tok], (N_KEYS, sub)) for h in range(PK_HEADS)]

        def body(il, carry):
            r0 = pl.multiple_of(il * N_KEYS, N_KEYS)
            w = jnp.zeros((N_KEYS, sub), F32)
            for h in range(PK_HEADS):
                prod = b_ref[h, :, tok] * a_ref[h, pl.ds(il, 1), tok]
                w = w + jnp.where(prod >= kaps[h], prod, 0.0)
            s = sc_sc[pl.ds(r0, N_KEYS), :]
            act = 0.5 * s * (1.0 + lax.erf(s * (2.0 ** -0.5)))
            wa_sc[pl.ds(r0, N_KEYS), :] = (w * act).astype(BF16)
            return carry

        lax.fori_loop(0, ni, body, 0)
        acc_sc[:, tok] += jnp.dot(vt_ref[...], wa_sc[...], preferred_element_type=F32)

    @pl.when(e == pl.num_programs(1) - 1)
    def _():
        o_ref[...] = acc_sc[...].T


def _peer_dense(h2, u, vt, a, b, kap):
    M, D = h2.shape
    E = u.shape[0]
    tm = _pick(M, (1024, 512, 256, 128))
    sub = _pick(tm, (256, 128))
    eb = SUBLANES * N_KEYS
    kern = functools.partial(_peer_dense_kernel, sub=sub)
    return pl.pallas_call(
        kern, grid=(M // tm, E // eb),
        in_specs=[pl.BlockSpec((tm, D), lambda i, e: (i, 0)),
                  pl.BlockSpec((eb, D), lambda i, e: (e, 0)),
                  pl.BlockSpec((D, eb), lambda i, e: (0, e)),
                  pl.BlockSpec((PK_HEADS, SUBLANES, tm), lambda i, e: (0, e, i)),
                  pl.BlockSpec((PK_HEADS, N_KEYS, tm), lambda i, e: (0, 0, i)),
                  pl.BlockSpec((PK_HEADS, tm), lambda i, e: (0, i))],
        out_specs=pl.BlockSpec((tm, D), lambda i, e: (i, 0)),
        out_shape=jax.ShapeDtypeStruct((M, D), F32),
        scratch_shapes=[pltpu.VMEM((D, tm), F32), pltpu.VMEM((eb, sub), F32), pltpu.VMEM((eb, sub), BF16)],
        compiler_params=_cparams("parallel", "arbitrary"), name="peer_dense",
    )(h2, u, vt, a, b, kap)


def _rope_tables(n_ctx, S):
    n_freq = DA_DIM // 4
    n_rows = S // GRID_W
    row = jnp.repeat(jnp.arange(n_rows, dtype=F32), GRID_W)
    col = jnp.tile(jnp.arange(GRID_W, dtype=F32), n_rows)
    inv = ROPE_BASE ** (-jnp.arange(n_freq, dtype=F32) / n_freq)
    ang = jnp.concatenate([row[:, None] * inv, col[:, None] * inv], axis=-1)
    cos = jnp.concatenate([jnp.ones((n_ctx, DA_DIM // 2), F32), jnp.cos(ang)], axis=0)
    sin = jnp.concatenate([jnp.zeros((n_ctx, DA_DIM // 2), F32), jnp.sin(ang)], axis=0)
    return jnp.concatenate([cos] * 4, axis=1), jnp.concatenate([-sin, sin, -sin, sin], axis=1)


def _deinterleave_perm():
    idx = np.arange(DA_HEADS * 2 * DA_DIM).reshape(DA_HEADS * 2, DA_DIM // 2, 2)
    return np.concatenate([idx[:, :, 0], idx[:, :, 1]], axis=1).reshape(-1)


def kernel(x, c, ctx, c_ctx, w_ada, b_ada, norm1_g, norm2_g, w_in, da_lambda, da_subln, dn_conv, dn_a_log,
           dn_dt_bias, dn_norm, w_branch_a, w_branch_b, w_out, peer_wq, peer_keys, peer_u, peer_v, final_g):
    B, S, D = x.shape
    n_ctx = ctx.shape[1]
    T = n_ctx + S
    L = w_ada.shape[0]
    M = B * T
    assert B < SUBLANES and S % GRID_W == 0 and n_ctx % DN_CHUNK == 0 and S % DN_CHUNK == 0
    da_qk = DA_HEADS * 2 * DA_DIM
    da_w = DA_HEADS * DA_VDIM
    dn_qkv = 3 * DN_HEADS * DN_DK
    dn_w = DN_HEADS * DN_DV
    nh2 = 2 * DN_HEADS

    c_all = jnp.zeros((SUBLANES, D), F32).at[:B].set(c).at[B].set(c_ctx)
    mod = _ada(c_all, w_ada, b_ada).reshape(L, SUBLANES, 6, D)
    modsel = jnp.stack([jnp.broadcast_to(mod[:, B:B + 1], (L, B, 6, D)), mod[:, :B]], axis=2)

    cos_t, sin_t = _rope_tables(n_ctx, S)
    cos_m = jnp.tile(cos_t, (B, 1))
    sin_m = jnp.tile(sin_t, (B, 1))
    perm = _deinterleave_perm()

    xs = jnp.concatenate([ctx, x], axis=1)
    f = None
    for l in range(L):
        lam_init = 0.8 - 0.6 * math.exp(-0.3 * l)
        w = w_in[l]
        o = 0
        w_q = w[:, o:o + da_qk][:, perm]; o += da_qk
        w_k = w[:, o:o + da_qk][:, perm]; o += da_qk
        w_v = w[:, o:o + da_w]; o += da_w
        w_dn = w[:, o:o + dn_qkv]; o += dn_qkv
        w_z = w[:, o:o + dn_w]; o += dn_w
        w_b = w[:, o:o + nh2]; o += nh2
        w_a = w[:, o:o + nh2]; o += nh2
        w_ga = w[:, o:o + D]; o += D
        w_gb = w[:, o:o + D]; o += D
        w_qk = jnp.concatenate([w_q, w_k], axis=1).astype(BF16)
        w_rest = jnp.concatenate([w_v, w_dn, w_z, w_ga, w_gb], axis=1).astype(BF16)
        dn_col0, z_col0, ga_col0 = da_w, da_w + dn_qkv, da_w + dn_qkv + dn_w
        w_ba_pad = jnp.zeros((D, LANES), F32).at[:, :nh2].set(w_b).at[:, nh2:2 * nh2].set(w_a).astype(BF16)
        gate_par = (jnp.zeros((SUBLANES, LANES), F32)
                    .at[0, nh2:2 * nh2].set(dn_a_log[l].reshape(-1))
                    .at[1, nh2:2 * nh2].set(dn_dt_bias[l].reshape(-1)))

        if l == 0:
            h = _modulate(xs, norm1_g[l], modsel[l], n_ctx)
        else:
            xs, h = _resid_modulate(xs, f, modsel[l - 1], norm1_g[l], modsel[l], n_ctx)
        hm = h.reshape(M, D)

        qk = _proj(hm, w_qk, out_dtype=BF16, kernel=_proj_rope_kernel, extra=(cos_m, sin_m),
                   extra_specs=lambda tm: [pl.BlockSpec((tm, LANES), lambda j, i: (i, 0))] * 2,
                   name="proj_qk_rope").reshape(B, T, 2 * da_qk)
        rest = _proj(hm, w_rest, out_dtype=BF16, extra_specs=lambda tm: [], name="proj_rest").reshape(B, T, -1)
        gates = _proj(hm, w_ba_pad, out_dtype=F32, kernel=_proj_gates_kernel, extra=(gate_par,),
                      extra_specs=lambda tm: [pl.BlockSpec((SUBLANES, LANES), lambda j, i: (0, 0))],
                      name="proj_gates").reshape(B, T, LANES)

        da = _attention(qk, rest, da_lambda[l], da_subln[l], n_ctx, lam_init)
        dnq = _dn_prep(rest, dn_conv[l], n_ctx, dn_col0)
        dn = _gdn(dnq, gates, rest, z_col0, dn_norm[l], n_ctx)

        xs, h2 = _merge(da, dn, rest, ga_col0, xs, w_branch_a[l].astype(BF16), w_branch_b[l].astype(BF16),
                        w_out[l].astype(BF16), norm2_g[l], modsel[l], n_ctx)
        h2m = h2.reshape(M, D)
        pa, pb, kap = _peer_prep(h2m, peer_wq[l].astype(BF16),
                                 peer_keys[l].reshape(2 * PK_HEADS, N_KEYS, PK_HALF).astype(BF16))
        f = _peer_dense(h2m, peer_u[l].astype(BF16), peer_v[l].T.astype(BF16), pa, pb, kap).reshape(B, T, D)

    return _final(xs, f, modsel[L - 1], final_g, n_ctx)
```

```python
import functools
import math

import jax
import jax.numpy as jnp
import numpy as np
from jax import lax
from jax.experimental import pallas as pl
from jax.experimental.pallas import tpu as pltpu

F32 = jnp.float32
BF16 = jnp.bfloat16
HIGHEST = lax.Precision.HIGHEST

GRID_W = 64
EPS = 1e-6
ROPE_BASE = 10000.0
DA_HEADS = 8
DA_DIM = 64
DA_VDIM = 2 * DA_DIM
DN_HEADS = 8
DN_DK = 128
DN_DV = 128
DN_CONV = 5
DN_CHUNK = 64
PK_HEADS = 8
N_KEYS = 128
PK_TOPK = 16
PK_HALF = 128

LANES = 128
SUBLANES = 8
VMEM_LIMIT_BYTES = 56 * 1024 * 1024

NT_DIMS = (((1,), (1,)), ((), ()))


def _cparams(*sem):
    return pltpu.CompilerParams(dimension_semantics=sem, vmem_limit_bytes=VMEM_LIMIT_BYTES)


def _pick(n, cands):
    for c in cands:
        if n % c == 0:
            return c
    raise ValueError(f"no tile in {cands} divides {n}")


def _silu(x):
    return x * jax.nn.sigmoid(x)


def _ada_kernel(c_ref, w_ref, b_ref, o_ref):
    sc = _silu(c_ref[...])
    o_ref[0] = jnp.dot(sc, w_ref[0], preferred_element_type=F32, precision=HIGHEST) + b_ref[0]


def _ada(c_all, w_ada, b_ada):
    L, D, N = w_ada.shape
    tn = _pick(N, (1536, 1024, 512, 256, 128))
    return pl.pallas_call(
        _ada_kernel,
        grid=(L, N // tn),
        in_specs=[pl.BlockSpec((SUBLANES, D), lambda l, j: (0, 0)),
                  pl.BlockSpec((1, D, tn), lambda l, j: (l, 0, j)),
                  pl.BlockSpec((1, 1, tn), lambda l, j: (l, 0, j))],
        out_specs=pl.BlockSpec((1, SUBLANES, tn), lambda l, j: (l, 0, j)),
        out_shape=jax.ShapeDtypeStruct((L, SUBLANES, N), F32),
        compiler_params=_cparams("parallel", "parallel"),
        name="ada",
    )(c_all, w_ada, b_ada.reshape(L, 1, N))


def _rms_mod(x, g, m, shift_row, scale_row):
    ms = jnp.mean(x * x, axis=-1, keepdims=True)
    y = x * lax.rsqrt(ms + EPS) * g
    return y * (1.0 + m[scale_row:scale_row + 1]) + m[shift_row:shift_row + 1]


def _modulate_kernel(x_ref, g_ref, m_ref, h_ref):
    h_ref[...] = _rms_mod(x_ref[...], g_ref[...], m_ref[...], 0, 1).astype(BF16)


def _resid_modulate_kernel(x_ref, f_ref, mp_ref, g_ref, m_ref, xo_ref, h_ref):
    x = x_ref[...] + mp_ref[...][5:6] * f_ref[...]
    xo_ref[...] = x
    h_ref[...] = _rms_mod(x, g_ref[...], m_ref[...], 0, 1).astype(BF16)


def _seg_specs(tm, n_ctx_tiles, D):
    tok = pl.BlockSpec((None, tm, D), lambda b, t: (b, t, 0))
    gain = pl.BlockSpec((1, D), lambda b, t: (0, 0))
    mod = pl.BlockSpec((None, None, 6, D), lambda b, t: (b, jnp.where(t >= n_ctx_tiles, 1, 0), 0, 0))
    return tok, gain, mod


def _modulate(xs, g, modsel, n_ctx):
    B, T, D = xs.shape
    tm = _pick(math.gcd(n_ctx, T - n_ctx), (256, 128))
    tok, gain, mod = _seg_specs(tm, n_ctx // tm, D)
    return pl.pallas_call(
        _modulate_kernel, grid=(B, T // tm),
        in_specs=[tok, gain, mod], out_specs=tok,
        out_shape=jax.ShapeDtypeStruct((B, T, D), BF16),
        compiler_params=_cparams("parallel", "parallel"), name="modulate",
    )(xs, g.reshape(1, D), modsel)


def _resid_modulate(xs, f, modsel_prev, g, modsel, n_ctx):
    B, T, D = xs.shape
    tm = _pick(math.gcd(n_ctx, T - n_ctx), (256, 128))
    tok, gain, mod = _seg_specs(tm, n_ctx // tm, D)
    return pl.pallas_call(
        _resid_modulate_kernel, grid=(B, T // tm),
        in_specs=[tok, tok, mod, gain, mod], out_specs=[tok, tok],
        out_shape=[jax.ShapeDtypeStruct((B, T, D), F32), jax.ShapeDtypeStruct((B, T, D), BF16)],
        compiler_params=_cparams("parallel", "parallel"), name="resid_modulate",
    )(xs, f, modsel_prev, g.reshape(1, D), modsel)


def _final_kernel(x_ref, f_ref, mp_ref, g_ref, o_ref):
    x = x_ref[...] + mp_ref[...][5:6] * f_ref[...]
    ms = jnp.mean(x * x, axis=-1, keepdims=True)
    o_ref[...] = x * lax.rsqrt(ms + EPS) * g_ref[...]


def _final(xs, f, modsel_prev, g, n_ctx):
    B, T, D = xs.shape
    S = T - n_ctx
    tm = _pick(math.gcd(n_ctx, S), (256, 128))
    off = n_ctx // tm
    tok_in = pl.BlockSpec((None, tm, D), lambda b, t: (b, t + off, 0))
    return pl.pallas_call(
        _final_kernel, grid=(B, S // tm),
        in_specs=[tok_in, tok_in,
                  pl.BlockSpec((None, None, 6, D), lambda b, t: (b, 1, 0, 0)),
                  pl.BlockSpec((1, D), lambda b, t: (0, 0))],
        out_specs=pl.BlockSpec((None, tm, D), lambda b, t: (b, t, 0)),
        out_shape=jax.ShapeDtypeStruct((B, S, D), F32),
        compiler_params=_cparams("parallel", "parallel"), name="final_norm",
    )(xs, f, modsel_prev, g.reshape(1, D))


def _proj_kernel(h_ref, w_ref, o_ref):
    o_ref[...] = jnp.dot(h_ref[...], w_ref[...], preferred_element_type=F32).astype(o_ref.dtype)


def _proj_rope_kernel(h_ref, w_ref, cos_ref, sin_ref, o_ref):
    acc = jnp.dot(h_ref[...], w_ref[...], preferred_element_type=F32)
    tn = acc.shape[1]
    half = DA_DIM // 2
    lane = lax.broadcasted_iota(jnp.int32, acc.shape, 1)
    partner = jnp.where((lane & (DA_DIM - 1)) < half,
                        pltpu.roll(acc, tn - half, axis=1),
                        pltpu.roll(acc, half, axis=1))
    reps = tn // LANES
    cos = jnp.concatenate([cos_ref[...]] * reps, axis=1)
    sin = jnp.concatenate([sin_ref[...]] * reps, axis=1)
    y = acc * cos + partner * sin
    scale = jnp.where(pl.program_id(0) == 0, DA_DIM ** -0.5 * math.log2(math.e), 1.0).astype(F32)
    o_ref[...] = (y * scale).astype(o_ref.dtype)


def _proj_gates_kernel(h_ref, w_ref, p_ref, o_ref):
    acc = jnp.dot(h_ref[...], w_ref[...], preferred_element_type=F32)
    p = p_ref[...]
    lane = lax.broadcasted_iota(jnp.int32, acc.shape, 1)
    beta = jax.nn.sigmoid(acc)
    g = -jnp.exp(p[0:1]) * jax.nn.softplus(acc + p[1:2])
    o_ref[...] = jnp.where(lane < 2 * DN_HEADS, beta, jnp.where(lane < 4 * DN_HEADS, g, 0.0))


def _proj(h, w, *, out_dtype, kernel=_proj_kernel, extra=(), extra_specs=(), name="proj"):
    M, D = h.shape
    N = w.shape[1]
    tm = _pick(M, (512, 256, 128))
    tn = _pick(N, (1024, 512, 256, 128))
    return pl.pallas_call(
        kernel, grid=(N // tn, M // tm),
        in_specs=[pl.BlockSpec((tm, D), lambda j, i: (i, 0)),
                  pl.BlockSpec((D, tn), lambda j, i: (0, j))] + list(extra_specs(tm)),
        out_specs=pl.BlockSpec((tm, tn), lambda j, i: (i, j)),
        out_shape=jax.ShapeDtypeStruct((M, N), out_dtype),
        compiler_params=_cparams("parallel", "parallel"), name=name,
    )(h, w, *extra)


def _attn_kernel(lam_ref, q_ref, k_ref, v_ref, g_ref, o_ref, m_sc, l_sc, acc_sc, *, n_ctx, tk, lam_init):
    tq = q_ref.shape[0]
    T = k_ref.shape[0]
    qi = pl.program_id(2)
    q = q_ref[...]
    lane = lax.broadcasted_iota(jnp.int32, q.shape, 1)
    zero = jnp.zeros_like(q)
    qq = jnp.concatenate([jnp.where(lane < DA_DIM, q, zero), jnp.where(lane >= DA_DIM, q, zero)], axis=0)

    m_sc[...] = jnp.full(m_sc.shape, -jnp.inf, F32)
    l_sc[...] = jnp.zeros(l_sc.shape, F32)
    acc_sc[...] = jnp.zeros(acc_sc.shape, F32)

    def step(r0, rows):
        s = lax.dot_general(qq, k_ref[r0:r0 + rows, :], NT_DIMS, preferred_element_type=F32)
        m_prev = m_sc[...]
        m_new = jnp.maximum(m_prev, jnp.max(s, axis=-1, keepdims=True))
        alpha = jnp.exp2(m_prev - m_new)
        p = jnp.exp2(s - jnp.concatenate([m_new] * (rows // LANES), axis=1))
        l_sc[...] = alpha * l_sc[...] + jnp.sum(p, axis=-1, keepdims=True)
        acc_sc[...] = alpha * acc_sc[...] + jnp.dot(p.astype(BF16), v_ref[r0:r0 + rows, :],
                                                    preferred_element_type=F32)
        m_sc[...] = m_new

    step(0, n_ctx)

    @pl.when(qi * tq >= n_ctx)
    def _():
        for j in range((T - n_ctx) // tk):
            step(n_ctx + j * tk, tk)

    lam = lam_ref[...]
    lam_val = (jnp.exp(jnp.sum(lam[0:1] * lam[1:2], axis=-1, keepdims=True))
               - jnp.exp(jnp.sum(lam[2:3] * lam[3:4], axis=-1, keepdims=True)) + lam_init)
    o_all = acc_sc[...] / l_sc[...]
    o = o_all[0:tq] - lam_val * o_all[tq:2 * tq]
    ms = jnp.mean(o * o, axis=-1, keepdims=True)
    o_ref[...] = (o * lax.rsqrt(ms + EPS) * g_ref[...] * (1.0 - lam_init)).astype(o_ref.dtype)


def _attention(qk, rest, lam, subln, n_ctx, lam_init):
    B, T, _ = qk.shape
    H = DA_HEADS
    S = T - n_ctx
    tq = _pick(math.gcd(n_ctx, S), (256, 128))
    tk = _pick(S, (512, 256, 128))
    kern = functools.partial(_attn_kernel, n_ctx=n_ctx, tk=tk, lam_init=lam_init)
    return pl.pallas_call(
        kern, grid=(B, H, T // tq),
        in_specs=[pl.BlockSpec((4, DA_DIM), lambda b, h, i: (0, 0)),
                  pl.BlockSpec((None, tq, DA_VDIM), lambda b, h, i: (b, i, h)),
                  pl.BlockSpec((None, T, DA_VDIM), lambda b, h, i: (b, 0, H + h)),
                  pl.BlockSpec((None, T, DA_VDIM), lambda b, h, i: (b, 0, h)),
                  pl.BlockSpec((1, DA_VDIM), lambda b, h, i: (0, 0))],
        out_specs=pl.BlockSpec((None, tq, DA_VDIM), lambda b, h, i: (b, i, h)),
        out_shape=jax.ShapeDtypeStruct((B, T, H * DA_VDIM), BF16),
        scratch_shapes=[pltpu.VMEM((2 * tq, LANES), F32), pltpu.VMEM((2 * tq, LANES), F32),
                        pltpu.VMEM((2 * tq, DA_VDIM), F32)],
        compiler_params=_cparams("parallel", "parallel", "arbitrary"), name="diff_attention",
    )(lam, qk, qk, rest, subln.reshape(1, DA_VDIM))


def _dn_prep_kernel(x_ref, w_ref, o_ref, pad_sc, *, n_ctx, rows):
    T = x_ref.shape[0]
    halo = SUBLANES
    cb = pl.program_id(1)
    pad_sc[0:halo, :] = jnp.zeros((halo, LANES), F32)
    pad_sc[halo + T:2 * halo + T, :] = jnp.zeros((halo, LANES), F32)
    pad_sc[halo:halo + T, :] = x_ref[...].astype(F32)
    w = w_ref[...]
    is_qk = cb < 2 * DN_HEADS
    post = jnp.where(cb < DN_HEADS, DN_DK ** -0.5, 1.0).astype(F32)
    pad = DN_CONV // 2

    def body(c, carry):
        r0 = pl.multiple_of(c * rows, rows)
        win = pad_sc[pl.ds(r0, rows + 2 * halo), :]
        t = r0 + lax.broadcasted_iota(jnp.int32, (rows, LANES), 0)
        lo = jnp.where(t < n_ctx, 0, n_ctx)
        hi = jnp.where(t < n_ctx, n_ctx, T)
        y = jnp.zeros((rows, LANES), F32)
        for j in range(DN_CONV):
            d = j - pad
            xs = win[halo + d:halo + d + rows, :]
            ok = (t + d >= lo) & (t + d < hi)
            y = y + jnp.where(ok, xs, 0.0) * w[j:j + 1]
        y = _silu(y)
        yn = y * lax.rsqrt(jnp.sum(y * y, axis=-1, keepdims=True) + EPS) * post
        o_ref[pl.ds(r0, rows), :] = jnp.where(is_qk, yn, y).astype(o_ref.dtype)
        return carry

    lax.fori_loop(0, T // rows, body, 0)


def _dn_prep(rest, conv_w, n_ctx, col0):
    B, T, _ = rest.shape
    nblk = 3 * DN_HEADS
    rows = _pick(T, (256, 128))
    cw = jnp.zeros((SUBLANES, nblk * LANES), F32).at[:DN_CONV].set(conv_w)
    kern = functools.partial(_dn_prep_kernel, n_ctx=n_ctx, rows=rows)
    return pl.pallas_call(
        kern, grid=(B, nblk),
        in_specs=[pl.BlockSpec((None, T, LANES), lambda b, c: (b, 0, col0 // LANES + c)),
                  pl.BlockSpec((SUBLANES, LANES), lambda b, c: (0, c))],
        out_specs=pl.BlockSpec((None, T, LANES), lambda b, c: (b, 0, c)),
        out_shape=jax.ShapeDtypeStruct((B, T, nblk * LANES), BF16),
        scratch_shapes=[pltpu.VMEM((T + 2 * SUBLANES, LANES), F32)],
        compiler_params=_cparams("parallel", "parallel"), name="dn_prep",
    )(rest, cw)


GDN_HEADS_PER_STEP = 2
GDN_CHUNKS_PER_ITER = 2
GDN_INV_PASSES = 3


def _split_dot(a, b, passes):
    a_hi = a.astype(BF16)
    b_hi = b.astype(BF16)
    out = jnp.dot(a_hi, b_hi, preferred_element_type=F32)
    if passes >= 3:
        a_lo = (a - a_hi.astype(F32)).astype(BF16)
        b_lo = (b - b_hi.astype(F32)).astype(BF16)
        out = out + jnp.dot(a_hi, b_lo, preferred_element_type=F32) + jnp.dot(a_lo, b_hi, preferred_element_type=F32)
    return out


def _gdn_kernel(q_ref, k_ref, v_ref, gt_ref, z_ref, ng_ref, o_ref,
                w_sc, qg_sc, kt_sc, a_sc, u_sc, egl_sc, oacc_sc, s_sc, *, n_ctx):
    C = DN_CHUNK
    C2 = 2 * C
    T = q_ref.shape[0]
    nc = T // C
    ncc = n_ctx // C
    hp = pl.program_id(1)

    row = lax.broadcasted_iota(jnp.int32, (C2, C2), 0)
    col = lax.broadcasted_iota(jnp.int32, (C2, C2), 1)
    fwd_row = row < C
    rc_xor = row ^ col
    ahead = (col - row) * jnp.where(fwd_row, 1, -1)
    same_dir = rc_xor < C
    incl = same_dir & (ahead <= 0)
    strict = same_dir & (ahead < 0)
    eye = (row == col).astype(F32)
    mcs = incl.astype(F32)

    def p1_load(c, s):
        r0 = pl.multiple_of(c * C, C)
        head = hp * GDN_HEADS_PER_STEP + s
        lo, hi = s * LANES, (s + 1) * LANES
        k = k_ref[pl.ds(r0, C), lo:hi]
        q = q_ref[pl.ds(r0, C), lo:hi]
        v = v_ref[pl.ds(r0, C), lo:hi]
        kk = jnp.concatenate([k, k], axis=0)
        qq = jnp.concatenate([q, q], axis=0)
        vf = jnp.concatenate([v, v], axis=0).astype(F32)
        x = gt_ref[pl.ds(r0, C), :]
        x2 = jnp.concatenate([x, x], axis=0)
        bsel = jnp.where(fwd_row, head, DN_HEADS + head)
        gsel = bsel + 2 * DN_HEADS
        beta = jnp.sum(jnp.where(col == bsel, x2, 0.0), axis=-1, keepdims=True)
        glog = jnp.sum(jnp.where(col == gsel, x2, 0.0), axis=-1, keepdims=True)
        return dict(c=c, s=s, kk=kk, qq=qq, vf=vf, beta=beta, glog=glog)

    def p1_prep(d):
        gcum = jnp.dot(mcs, jnp.broadcast_to(d["glog"], (C2, C2)), preferred_element_type=F32,
                       precision=HIGHEST)
        decay = jnp.exp(jnp.where(incl, gcum - gcum.T, -jnp.inf))
        kkt = lax.dot_general(d["kk"], d["kk"], NT_DIMS, preferred_element_type=F32)
        qkt = lax.dot_general(d["qq"], d["kk"], NT_DIMS, preferred_element_type=F32)
        d.update(gcum=gcum, decay=decay, qkt=qkt, lmat=jnp.where(strict, d["beta"] * kkt * decay, 0.0), tinv=eye)

    def p1_finish(d):
        c, s, gcum, beta = d["c"], d["s"], d["gcum"], d["beta"]
        kf = d["kk"].astype(F32)
        eg = jnp.exp(gcum)
        rhs = jnp.concatenate([(d["vf"] * beta).astype(BF16), (kf * beta * eg).astype(BF16)], axis=1)
        uw = jnp.dot(d["tinv"].astype(BF16), rhs, preferred_element_type=F32)
        glast = jnp.where(fwd_row, gcum[C - 1:C, :], gcum[C:C + 1, :])
        ktail = kf * jnp.exp(glast - gcum)
        u_sc[s, c] = uw[:, 0:LANES]
        w_sc[s, c] = uw[:, LANES:2 * LANES].astype(BF16)
        qg_sc[s, c] = (d["qq"].astype(F32) * eg).astype(BF16)
        a_sc[s, c] = (d["qkt"] * d["decay"]).astype(BF16)
        kt_sc[s, c] = ktail.T.astype(BF16)
        egl_sc[s, c] = jnp.exp(jnp.concatenate([jnp.broadcast_to(gcum[C - 1:C, :], (4, LANES)),
                                                jnp.broadcast_to(gcum[C:C + 1, :], (4, LANES))], axis=0))

    def p1_body(i, carry):
        probs = [p1_load(i * GDN_CHUNKS_PER_ITER + j, s)
                 for j in range(GDN_CHUNKS_PER_ITER) for s in range(GDN_HEADS_PER_STEP)]
        for d in probs:
            p1_prep(d)
        for lvl in range(int(math.log2(C))):
            for d in probs:
                d["x"] = _split_dot(d["tinv"], jnp.where((rc_xor >> lvl) == 1, d["lmat"], 0.0), GDN_INV_PASSES)
            for d in probs:
                d["tinv"] = d["tinv"] - _split_dot(d["x"], d["tinv"], GDN_INV_PASSES)
        for d in probs:
            p1_finish(d)
        return carry

    lax.fori_loop(0, nc // GDN_CHUNKS_PER_ITER, p1_body, 0)

    oacc_sc[...] = jnp.zeros(oacc_sc.shape, F32)
    s_sc[...] = jnp.zeros(s_sc.shape, F32)
    lane_b = lax.broadcasted_iota(jnp.int32, (LANES, LANES), 1)
    zpad = jnp.zeros((C, LANES), BF16)

    def p2_body(i, carry):
        cf = i
        cb = jnp.where(i < ncc, ncc - 1 - i, nc - 1 + ncc - i)
        rf = pl.multiple_of(cf * C, C)
        rb = pl.multiple_of(cb * C, C)
        heads = range(GDN_HEADS_PER_STEP)
        st = [s_sc[s] for s in heads]
        lhs1 = [jnp.concatenate([
            jnp.concatenate([w_sc[s, cf, 0:C, :], zpad], axis=1),
            jnp.concatenate([zpad, w_sc[s, cb, C:C2, :]], axis=1),
            jnp.concatenate([qg_sc[s, cf, 0:C, :], zpad], axis=1),
            jnp.concatenate([zpad, qg_sc[s, cb, C:C2, :]], axis=1)], axis=0) for s in heads]
        zk = jnp.zeros((LANES, C2), BF16)
        lhs2 = [jnp.concatenate([a_sc[s, cf, 0:C, :], a_sc[s, cb, C:C2, :],
                                 jnp.where(lane_b < C, kt_sc[s, cf], zk),
                                 jnp.where(lane_b >= C, kt_sc[s, cb], zk)], axis=0) for s in heads]
        u = [jnp.concatenate([u_sc[s, cf, 0:C, :], u_sc[s, cb, C:C2, :]], axis=0) for s in heads]
        scale = [jnp.concatenate([jnp.broadcast_to(egl_sc[s, cf][0:1], (DN_DK, DN_DV)),
                                  jnp.broadcast_to(egl_sc[s, cb][4:5], (DN_DK, DN_DV))], axis=0) for s in heads]
        r1 = [jnp.dot(lhs1[s], st[s].astype(BF16), preferred_element_type=F32) for s in heads]
        vnew = [(u[s] - r1[s][0:C2]).astype(BF16) for s in heads]
        r2 = [jnp.dot(lhs2[s], vnew[s], preferred_element_type=F32) for s in heads]
        for s in heads:
            s_sc[s] = st[s] * scale[s] + r2[s][C2:C2 + 2 * DN_DK]
        for s in heads:
            oacc_sc[s, pl.ds(rf, C), :] += r1[s][C2:C2 + C] + r2[s][0:C]
            oacc_sc[s, pl.ds(rb, C), :] += r1[s][C2 + C:2 * C2] + r2[s][C:C2]
        return carry

    lax.fori_loop(0, nc, p2_body, 0)

    ng = ng_ref[...]
    rows = _pick(T, (256, 128))

    def fin_body(c, carry):
        r0 = pl.multiple_of(c * rows, rows)
        for s in range(GDN_HEADS_PER_STEP):
            o = oacc_sc[s, pl.ds(r0, rows), :]
            zf = z_ref[pl.ds(r0, rows), s * LANES:(s + 1) * LANES].astype(F32)
            ms = jnp.mean(o * o, axis=-1, keepdims=True)
            o_ref[pl.ds(r0, rows), s * LANES:(s + 1) * LANES] = (o * lax.rsqrt(ms + EPS) * ng * _silu(zf)).astype(o_ref.dtype)
        return carry

    lax.fori_loop(0, T // rows, fin_body, 0)


def _gdn(dnq, gates, rest, z_col0, norm_g, n_ctx):
    B, T, _ = dnq.shape
    H = DN_HEADS
    hps = GDN_HEADS_PER_STEP
    wblk = hps * LANES
    nc = T // DN_CHUNK
    assert nc % GDN_CHUNKS_PER_ITER == 0
    C2 = 2 * DN_CHUNK
    kern = functools.partial(_gdn_kernel, n_ctx=n_ctx)
    nb = H // hps
    once = dict(pipeline_mode=pl.Buffered(1))
    return pl.pallas_call(
        kern, grid=(B, nb),
        in_specs=[pl.BlockSpec((None, T, wblk), lambda b, h: (b, 0, h), **once),
                  pl.BlockSpec((None, T, wblk), lambda b, h: (b, 0, nb + h), **once),
                  pl.BlockSpec((None, T, wblk), lambda b, h: (b, 0, 2 * nb + h), **once),
                  pl.BlockSpec((None, T, LANES), lambda b, h: (b, 0, 0), **once),
                  pl.BlockSpec((None, T, wblk), lambda b, h: (b, 0, z_col0 // wblk + h), **once),
                  pl.BlockSpec((1, DN_DV), lambda b, h: (0, 0))],
        out_specs=pl.BlockSpec((None, T, wblk), lambda b, h: (b, 0, h)),
        out_shape=jax.ShapeDtypeStruct((B, T, H * DN_DV), BF16),
        scratch_shapes=[pltpu.VMEM((hps, nc, C2, LANES), BF16),
                        pltpu.VMEM((hps, nc, C2, LANES), BF16),
                        pltpu.VMEM((hps, nc, LANES, C2), BF16),
                        pltpu.VMEM((hps, nc, C2, C2), BF16),
                        pltpu.VMEM((hps, nc, C2, LANES), F32),
                        pltpu.VMEM((hps, nc, SUBLANES, LANES), F32),
                        pltpu.VMEM((hps, T, LANES), F32),
                        pltpu.VMEM((hps, 2 * DN_DK, DN_DV), F32)],
        compiler_params=_cparams("parallel", "arbitrary"), name="gated_deltanet",
    )(dnq, dnq, dnq, gates, rest, norm_g.reshape(1, DN_DV))


def _merge_kernel(da_ref, dn_ref, ga_ref, gb_ref, x_ref, wa_ref, wb_ref, wo_ref, g_ref, m_ref, xo_ref, h_ref):
    ya = jnp.dot(da_ref[...], wa_ref[...], preferred_element_type=F32)
    yb = jnp.dot(dn_ref[...], wb_ref[...], preferred_element_type=F32)
    y = jax.nn.sigmoid(ga_ref[...].astype(F32)) * ya + jax.nn.sigmoid(gb_ref[...].astype(F32)) * yb
    y2 = jnp.dot(y.astype(BF16), wo_ref[...], preferred_element_type=F32)
    m = m_ref[...]
    x = x_ref[...] + m[2:3] * y2
    xo_ref[...] = x
    h_ref[...] = _rms_mod(x, g_ref[...], m, 3, 4).astype(BF16)


def _merge(da, dn, rest, ga_col0, xs, w_ba, w_bb, w_o, g2, modsel, n_ctx):
    B, T, D = xs.shape
    tm = _pick(math.gcd(n_ctx, T - n_ctx), (256, 128))
    tok, gain, mod = _seg_specs(tm, n_ctx // tm, D)
    wspec = pl.BlockSpec((D, D), lambda b, t: (0, 0))
    ga_blk = ga_col0 // D
    return pl.pallas_call(
        _merge_kernel, grid=(B, T // tm),
        in_specs=[tok, tok,
                  pl.BlockSpec((None, tm, D), lambda b, t: (b, t, ga_blk)),
                  pl.BlockSpec((None, tm, D), lambda b, t: (b, t, ga_blk + 1)),
                  tok, wspec, wspec, wspec, gain, mod],
        out_specs=[tok, tok],
        out_shape=[jax.ShapeDtypeStruct((B, T, D), F32), jax.ShapeDtypeStruct((B, T, D), BF16)],
        compiler_params=_cparams("parallel", "parallel"), name="merge",
    )(da, dn, rest, rest, xs, w_ba, w_bb, w_o, g2.reshape(1, D), modsel)


def _desc_tops(x, n):
    tops = []
    for _ in range(n):
        m = jnp.max(x, axis=0, keepdims=True)
        tops.append(m)
        x = jnp.where(x == m, -jnp.inf, x)
    return tops


def _peer_prep_kernel(h_ref, wq_ref, keys_ref, a_ref, b_ref, kap_ref):
    tm = h_ref.shape[0]
    q = jnp.dot(h_ref[...], wq_ref[...], preferred_element_type=F32).astype(BF16)
    for h in range(PK_HEADS):
        st = [lax.dot_general(keys_ref[2 * h + p], q[:, (2 * h + p) * PK_HALF:(2 * h + p + 1) * PK_HALF],
                              NT_DIMS, preferred_element_type=F32) for p in range(2)]
        ta = _desc_tops(st[0], PK_TOPK + 1)
        tb = _desc_tops(st[1], PK_TOPK + 1)
        tbs = jnp.concatenate(tb[:PK_TOPK], axis=0)
        cand = jnp.concatenate([ta[r] + tbs for r in range(PK_TOPK)], axis=0)
        best = _desc_tops(cand, PK_TOPK + 1)
        mx = best[0]
        zsum = best[0] * 0.0
        for r in range(PK_TOPK):
            zsum = zsum + jnp.exp(best[r] - mx)
        nxt = jnp.maximum(best[PK_TOPK], jnp.maximum(ta[PK_TOPK] + tb[0], ta[0] + tb[PK_TOPK]))
        thr = 0.5 * (best[PK_TOPK - 1] + nxt)
        a_ref[h] = jnp.exp(st[0] - ta[0])
        b_ref[h] = jnp.exp(st[1] - tb[0]) / zsum
        kap_ref[h:h + 1, :] = jnp.exp(thr - mx) / zsum


def _peer_prep(h2, wq, keys):
    M, D = h2.shape
    tm = _pick(M, (256, 128))
    nk = 2 * PK_HEADS
    return pl.pallas_call(
        _peer_prep_kernel, grid=(M // tm,),
        in_specs=[pl.BlockSpec((tm, D), lambda i: (i, 0)),
                  pl.BlockSpec((D, nk * PK_HALF), lambda i: (0, 0)),
                  pl.BlockSpec((nk, N_KEYS, PK_HALF), lambda i: (0, 0, 0))],
        out_specs=[pl.BlockSpec((PK_HEADS, N_KEYS, tm), lambda i: (0, 0, i)),
                   pl.BlockSpec((PK_HEADS, N_KEYS, tm), lambda i: (0, 0, i)),
                   pl.BlockSpec((PK_HEADS, tm), lambda i: (0, i))],
        out_shape=[jax.ShapeDtypeStruct((PK_HEADS, N_KEYS, M), F32),
                   jax.ShapeDtypeStruct((PK_HEADS, N_KEYS, M), F32),
                   jax.ShapeDtypeStruct((PK_HEADS, M), F32)],
        compiler_params=_cparams("parallel"), name="peer_prep",
    )(h2, wq, keys)


def _peer_dense_kernel(h_ref, u_ref, vt_ref, a_ref, b_ref, kap_ref, o_ref, acc_sc, *, sub):
    e = pl.program_id(1)
    tm = h_ref.shape[0]
    eb = u_ref.shape[0]
    ni = eb // N_KEYS

    @pl.when(e == 0)
    def _():
        acc_sc[...] = jnp.zeros(acc_sc.shape, F32)

    for t in range(tm // sub):
        tok = slice(t * sub, (t + 1) * sub)
        sc = lax.dot_general(u_ref[...], h_ref[tok, :], NT_DIMS, preferred_element_type=F32)
        was = []
        for il in range(ni):
            w = jnp.zeros((N_KEYS, sub), F32)
            for h in range(PK_HEADS):
                prod = b_ref[h, :, tok] * a_ref[h, il:il + 1, tok]
                w = w + jnp.where(prod >= kap_ref[h:h + 1, tok], prod, 0.0)
            s = sc[il * N_KEYS:(il + 1) * N_KEYS, :]
            act = 0.5 * s * (1.0 + lax.erf(s * (2.0 ** -0.5)))
            was.append((w * act).astype(BF16))
        acc_sc[:, tok] += jnp.dot(vt_ref[...], jnp.concatenate(was, axis=0), preferred_element_type=F32)

    @pl.when(e == pl.num_programs(1) - 1)
    def _():
        o_ref[...] = acc_sc[...].T


def _peer_dense(h2, u, vt, a, b, kap):
    M, D = h2.shape
    E = u.shape[0]
    tm = _pick(M, (1024, 512, 256, 128))
    sub = _pick(tm, (256, 128))
    eb = SUBLANES * N_KEYS
    kern = functools.partial(_peer_dense_kernel, sub=sub)
    return pl.pallas_call(
        kern, grid=(M // tm, E // eb),
        in_specs=[pl.BlockSpec((tm, D), lambda i, e: (i, 0)),
                  pl.BlockSpec((eb, D), lambda i, e: (e, 0)),
                  pl.BlockSpec((D, eb), lambda i, e: (0, e)),
                  pl.BlockSpec((PK_HEADS, SUBLANES, tm), lambda i, e: (0, e, i)),
                  pl.BlockSpec((PK_HEADS, N_KEYS, tm), lambda i, e: (0, 0, i)),
                  pl.BlockSpec((PK_HEADS, tm), lambda i, e: (0, i))],
        out_specs=pl.BlockSpec((tm, D), lambda i, e: (i, 0)),
        out_shape=jax.ShapeDtypeStruct((M, D), F32),
        scratch_shapes=[pltpu.VMEM((D, tm), F32)],
        compiler_params=_cparams("parallel", "arbitrary"), name="peer_dense",
    )(h2, u, vt, a, b, kap)


def _rope_tables(n_ctx, S):
    n_freq = DA_DIM // 4
    n_rows = S // GRID_W
    row = jnp.repeat(jnp.arange(n_rows, dtype=F32), GRID_W)
    col = jnp.tile(jnp.arange(GRID_W, dtype=F32), n_rows)
    inv = ROPE_BASE ** (-jnp.arange(n_freq, dtype=F32) / n_freq)
    ang = jnp.concatenate([row[:, None] * inv, col[:, None] * inv], axis=-1)
    cos = jnp.concatenate([jnp.ones((n_ctx, DA_DIM // 2), F32), jnp.cos(ang)], axis=0)
    sin = jnp.concatenate([jnp.zeros((n_ctx, DA_DIM // 2), F32), jnp.sin(ang)], axis=0)
    return jnp.concatenate([cos] * 4, axis=1), jnp.concatenate([-sin, sin, -sin, sin], axis=1)


def _deinterleave_perm():
    idx = np.arange(DA_HEADS * 2 * DA_DIM).reshape(DA_HEADS * 2, DA_DIM // 2, 2)
    return np.concatenate([idx[:, :, 0], idx[:, :, 1]], axis=1).reshape(-1)


def kernel(x, c, ctx, c_ctx, w_ada, b_ada, norm1_g, norm2_g, w_in, da_lambda, da_subln, dn_conv, dn_a_log,
           dn_dt_bias, dn_norm, w_branch_a, w_branch_b, w_out, peer_wq, peer_keys, peer_u, peer_v, final_g):
    B, S, D = x.shape
    n_ctx = ctx.shape[1]
    T = n_ctx + S
    L = w_ada.shape[0]
    M = B * T
    assert B < SUBLANES and S % GRID_W == 0 and n_ctx % DN_CHUNK == 0 and S % DN_CHUNK == 0
    da_qk = DA_HEADS * 2 * DA_DIM
    da_w = DA_HEADS * DA_VDIM
    dn_qkv = 3 * DN_HEADS * DN_DK
    dn_w = DN_HEADS * DN_DV
    nh2 = 2 * DN_HEADS

    c_all = jnp.zeros((SUBLANES, D), F32).at[:B].set(c).at[B].set(c_ctx)
    mod = _ada(c_all, w_ada, b_ada).reshape(L, SUBLANES, 6, D)
    modsel = jnp.stack([jnp.broadcast_to(mod[:, B:B + 1], (L, B, 6, D)), mod[:, :B]], axis=2)

    cos_t, sin_t = _rope_tables(n_ctx, S)
    cos_m = jnp.tile(cos_t, (B, 1))
    sin_m = jnp.tile(sin_t, (B, 1))
    perm = _deinterleave_perm()

    xs = jnp.concatenate([ctx, x], axis=1)
    f = None
    for l in range(L):
        lam_init = 0.8 - 0.6 * math.exp(-0.3 * l)
        w = w_in[l]
        o = 0
        w_q = w[:, o:o + da_qk][:, perm]; o += da_qk
        w_k = w[:, o:o + da_qk][:, perm]; o += da_qk
        w_v = w[:, o:o + da_w]; o += da_w
        w_dn = w[:, o:o + dn_qkv]; o += dn_qkv
        w_z = w[:, o:o + dn_w]; o += dn_w
        w_b = w[:, o:o + nh2]; o += nh2
        w_a = w[:, o:o + nh2]; o += nh2
        w_ga = w[:, o:o + D]; o += D
        w_gb = w[:, o:o + D]; o += D
        w_qk = jnp.concatenate([w_q, w_k], axis=1).astype(BF16)
        w_rest = jnp.concatenate([w_v, w_dn, w_z, w_ga, w_gb], axis=1).astype(BF16)
        dn_col0, z_col0, ga_col0 = da_w, da_w + dn_qkv, da_w + dn_qkv + dn_w
        w_ba_pad = jnp.zeros((D, LANES), F32).at[:, :nh2].set(w_b).at[:, nh2:2 * nh2].set(w_a).astype(BF16)
        gate_par = (jnp.zeros((SUBLANES, LANES), F32)
                    .at[0, nh2:2 * nh2].set(dn_a_log[l].reshape(-1))
                    .at[1, nh2:2 * nh2].set(dn_dt_bias[l].reshape(-1)))

        if l == 0:
            h = _modulate(xs, norm1_g[l], modsel[l], n_ctx)
        else:
            xs, h = _resid_modulate(xs, f, modsel[l - 1], norm1_g[l], modsel[l], n_ctx)
        hm = h.reshape(M, D)

        qk = _proj(hm, w_qk, out_dtype=BF16, kernel=_proj_rope_kernel, extra=(cos_m, sin_m),
                   extra_specs=lambda tm: [pl.BlockSpec((tm, LANES), lambda j, i: (i, 0))] * 2,
                   name="proj_qk_rope").reshape(B, T, 2 * da_qk)
        rest = _proj(hm, w_rest, out_dtype=BF16, extra_specs=lambda tm: [], name="proj_rest").reshape(B, T, -1)
        gates = _proj(hm, w_ba_pad, out_dtype=F32, kernel=_proj_gates_kernel, extra=(gate_par,),
                      extra_specs=lambda tm: [pl.BlockSpec((SUBLANES, LANES), lambda j, i: (0, 0))],
                      name="proj_gates").reshape(B, T, LANES)

        da = _attention(qk, rest, da_lambda[l], da_subln[l], n_ctx, lam_init)
        dnq = _dn_prep(rest, dn_conv[l], n_ctx, dn_col0)
        dn = _gdn(dnq, gates, rest, z_col0, dn_norm[l], n_ctx)

        xs, h2 = _merge(da, dn, rest, ga_col0, xs, w_branch_a[l].astype(BF16), w_branch_b[l].astype(BF16),
                        w_out[l].astype(BF16), norm2_g[l], modsel[l], n_ctx)
        h2m = h2.reshape(M, D)
        pa, pb, kap = _peer_prep(h2m, peer_wq[l].astype(BF16),
                                 peer_keys[l].reshape(2 * PK_HEADS, N_KEYS, PK_HALF).astype(BF16))
        f = _peer_dense(h2m, peer_u[l].astype(BF16), peer_v[l].T.astype(BF16), pa, pb, kap).reshape(B, T, D)

    return _final(xs, f, modsel[L - 1], final_g, n_ctx)
```

```python
import functools
import math

import jax
import jax.numpy as jnp
import numpy as np
from jax import lax
from jax.experimental import pallas as pl
from jax.experimental.pallas import tpu as pltpu

F32 = jnp.float32
BF16 = jnp.bfloat16
HIGHEST = lax.Precision.HIGHEST

GRID_W = 64
EPS = 1e-6
ROPE_BASE = 10000.0
DA_HEADS = 8
DA_DIM = 64
DA_VDIM = 2 * DA_DIM
DN_HEADS = 8
DN_DK = 128
DN_DV = 128
DN_CONV = 5
DN_CHUNK = 64
PK_HEADS = 8
N_KEYS = 128
PK_TOPK = 16
PK_HALF = 128

LANES = 128
SUBLANES = 8
VMEM_LIMIT_BYTES = 56 * 1024 * 1024

NT_DIMS = (((1,), (1,)), ((), ()))


def _cparams(*sem, flags=None):
    return pltpu.CompilerParams(dimension_semantics=sem, vmem_limit_bytes=VMEM_LIMIT_BYTES, flags=flags)


def _pick(n, cands):
    for c in cands:
        if n % c == 0:
            return c
    raise ValueError(f"no tile in {cands} divides {n}")


def _silu(x):
    return x * jax.nn.sigmoid(x)


def _ada_kernel(c_ref, w_ref, b_ref, o_ref):
    sc = _silu(c_ref[...])
    o_ref[0] = jnp.dot(sc, w_ref[0], preferred_element_type=F32, precision=HIGHEST) + b_ref[0]


def _ada(c_all, w_ada, b_ada):
    L, D, N = w_ada.shape
    tn = _pick(N, (1536, 1024, 512, 256, 128))
    return pl.pallas_call(
        _ada_kernel,
        grid=(L, N // tn),
        in_specs=[pl.BlockSpec((SUBLANES, D), lambda l, j: (0, 0)),
                  pl.BlockSpec((1, D, tn), lambda l, j: (l, 0, j)),
                  pl.BlockSpec((1, 1, tn), lambda l, j: (l, 0, j))],
        out_specs=pl.BlockSpec((1, SUBLANES, tn), lambda l, j: (l, 0, j)),
        out_shape=jax.ShapeDtypeStruct((L, SUBLANES, N), F32),
        compiler_params=_cparams("parallel", "parallel"),
        name="ada",
    )(c_all, w_ada, b_ada.reshape(L, 1, N))


def _rms_mod(x, g, m, shift_row, scale_row):
    ms = jnp.mean(x * x, axis=-1, keepdims=True)
    y = x * lax.rsqrt(ms + EPS) * g
    return y * (1.0 + m[scale_row:scale_row + 1]) + m[shift_row:shift_row + 1]


def _modulate_kernel(x_ref, g_ref, m_ref, h_ref):
    h_ref[...] = _rms_mod(x_ref[...], g_ref[...], m_ref[...], 0, 1).astype(BF16)


def _resid_modulate_kernel(x_ref, f_ref, mp_ref, g_ref, m_ref, xo_ref, h_ref):
    x = x_ref[...] + mp_ref[...][5:6] * f_ref[...]
    xo_ref[...] = x
    h_ref[...] = _rms_mod(x, g_ref[...], m_ref[...], 0, 1).astype(BF16)


def _seg_specs(tm, n_ctx_tiles, D):
    tok = pl.BlockSpec((None, tm, D), lambda b, t: (b, t, 0))
    gain = pl.BlockSpec((1, D), lambda b, t: (0, 0))
    mod = pl.BlockSpec((None, None, 6, D), lambda b, t: (b, jnp.where(t >= n_ctx_tiles, 1, 0), 0, 0))
    return tok, gain, mod


def _modulate(xs, g, modsel, n_ctx):
    B, T, D = xs.shape
    tm = _pick(math.gcd(n_ctx, T - n_ctx), (256, 128))
    tok, gain, mod = _seg_specs(tm, n_ctx // tm, D)
    return pl.pallas_call(
        _modulate_kernel, grid=(B, T // tm),
        in_specs=[tok, gain, mod], out_specs=tok,
        out_shape=jax.ShapeDtypeStruct((B, T, D), BF16),
        compiler_params=_cparams("parallel", "parallel"), name="modulate",
    )(xs, g.reshape(1, D), modsel)


def _resid_modulate(xs, f, modsel_prev, g, modsel, n_ctx):
    B, T, D = xs.shape
    tm = _pick(math.gcd(n_ctx, T - n_ctx), (256, 128))
    tok, gain, mod = _seg_specs(tm, n_ctx // tm, D)
    return pl.pallas_call(
        _resid_modulate_kernel, grid=(B, T // tm),
        in_specs=[tok, tok, mod, gain, mod], out_specs=[tok, tok],
        out_shape=[jax.ShapeDtypeStruct((B, T, D), F32), jax.ShapeDtypeStruct((B, T, D), BF16)],
        compiler_params=_cparams("parallel", "parallel"), name="resid_modulate",
    )(xs, f, modsel_prev, g.reshape(1, D), modsel)


def _final_kernel(x_ref, f_ref, mp_ref, g_ref, o_ref):
    x = x_ref[...] + mp_ref[...][5:6] * f_ref[...]
    ms = jnp.mean(x * x, axis=-1, keepdims=True)
    o_ref[...] = x * lax.rsqrt(ms + EPS) * g_ref[...]


def _final(xs, f, modsel_prev, g, n_ctx):
    B, T, D = xs.shape
    S = T - n_ctx
    tm = _pick(math.gcd(n_ctx, S), (256, 128))
    off = n_ctx // tm
    tok_in = pl.BlockSpec((None, tm, D), lambda b, t: (b, t + off, 0))
    return pl.pallas_call(
        _final_kernel, grid=(B, S // tm),
        in_specs=[tok_in, tok_in,
                  pl.BlockSpec((None, None, 6, D), lambda b, t: (b, 1, 0, 0)),
                  pl.BlockSpec((1, D), lambda b, t: (0, 0))],
        out_specs=pl.BlockSpec((None, tm, D), lambda b, t: (b, t, 0)),
        out_shape=jax.ShapeDtypeStruct((B, S, D), F32),
        compiler_params=_cparams("parallel", "parallel"), name="final_norm",
    )(xs, f, modsel_prev, g.reshape(1, D))


def _proj_kernel(h_ref, w_ref, o_ref):
    o_ref[...] = jnp.dot(h_ref[...], w_ref[...], preferred_element_type=F32).astype(o_ref.dtype)


def _proj_rope_kernel(h_ref, w_ref, cos_ref, sin_ref, o_ref):
    acc = jnp.dot(h_ref[...], w_ref[...], preferred_element_type=F32)
    tn = acc.shape[1]
    half = DA_DIM // 2
    lane = lax.broadcasted_iota(jnp.int32, acc.shape, 1)
    partner = jnp.where((lane & (DA_DIM - 1)) < half,
                        pltpu.roll(acc, tn - half, axis=1),
                        pltpu.roll(acc, half, axis=1))
    reps = tn // LANES
    cos = jnp.concatenate([cos_ref[...]] * reps, axis=1)
    sin = jnp.concatenate([sin_ref[...]] * reps, axis=1)
    y = acc * cos + partner * sin
    scale = jnp.where(pl.program_id(0) == 0, DA_DIM ** -0.5 * math.log2(math.e), 1.0).astype(F32)
    o_ref[...] = (y * scale).astype(o_ref.dtype)


def _proj_gates_kernel(h_ref, w_ref, p_ref, o_ref):
    acc = jnp.dot(h_ref[...], w_ref[...], preferred_element_type=F32)
    p = p_ref[...]
    lane = lax.broadcasted_iota(jnp.int32, acc.shape, 1)
    beta = jax.nn.sigmoid(acc)
    g = -jnp.exp(p[0:1]) * jax.nn.softplus(acc + p[1:2])
    o_ref[...] = jnp.where(lane < 2 * DN_HEADS, beta, jnp.where(lane < 4 * DN_HEADS, g, 0.0))


def _proj(h, w, *, out_dtype, kernel=_proj_kernel, extra=(), extra_specs=(), name="proj"):
    M, D = h.shape
    N = w.shape[1]
    tm = _pick(M, (512, 256, 128))
    tn = _pick(N, (1024, 512, 256, 128))
    return pl.pallas_call(
        kernel, grid=(N // tn, M // tm),
        in_specs=[pl.BlockSpec((tm, D), lambda j, i: (i, 0)),
                  pl.BlockSpec((D, tn), lambda j, i: (0, j))] + list(extra_specs(tm)),
        out_specs=pl.BlockSpec((tm, tn), lambda j, i: (i, j)),
        out_shape=jax.ShapeDtypeStruct((M, N), out_dtype),
        compiler_params=_cparams("parallel", "parallel"), name=name,
    )(h, w, *extra)


def _attn_kernel(lam_ref, q_ref, k_ref, v_ref, g_ref, o_ref, m_sc, l_sc, acc_sc, *, n_ctx, tk, lam_init):
    tq = q_ref.shape[0]
    T = k_ref.shape[0]
    qi = pl.program_id(2)
    q = q_ref[...]
    lane = lax.broadcasted_iota(jnp.int32, q.shape, 1)
    zero = jnp.zeros_like(q)
    qq = jnp.concatenate([jnp.where(lane < DA_DIM, q, zero), jnp.where(lane >= DA_DIM, q, zero)], axis=0)

    m_sc[...] = jnp.full(m_sc.shape, -jnp.inf, F32)
    l_sc[...] = jnp.zeros(l_sc.shape, F32)
    acc_sc[...] = jnp.zeros(acc_sc.shape, F32)

    def step(r0, rows):
        s = lax.dot_general(qq, k_ref[r0:r0 + rows, :], NT_DIMS, preferred_element_type=F32)
        m_prev = m_sc[...]
        m_new = jnp.maximum(m_prev, jnp.max(s, axis=-1, keepdims=True))
        alpha = jnp.exp2(m_prev - m_new)
        p = jnp.exp2(s - jnp.concatenate([m_new] * (rows // LANES), axis=1))
        l_sc[...] = alpha * l_sc[...] + jnp.sum(p, axis=-1, keepdims=True)
        acc_sc[...] = alpha * acc_sc[...] + jnp.dot(p.astype(BF16), v_ref[r0:r0 + rows, :],
                                                    preferred_element_type=F32)
        m_sc[...] = m_new

    step(0, n_ctx)

    @pl.when(qi * tq >= n_ctx)
    def _():
        for j in range((T - n_ctx) // tk):
            step(n_ctx + j * tk, tk)

    lam = lam_ref[...]
    lam_val = (jnp.exp(jnp.sum(lam[0:1] * lam[1:2], axis=-1, keepdims=True))
               - jnp.exp(jnp.sum(lam[2:3] * lam[3:4], axis=-1, keepdims=True)) + lam_init)
    o_all = acc_sc[...] / l_sc[...]
    o = o_all[0:tq] - lam_val * o_all[tq:2 * tq]
    ms = jnp.mean(o * o, axis=-1, keepdims=True)
    o_ref[...] = (o * lax.rsqrt(ms + EPS) * g_ref[...] * (1.0 - lam_init)).astype(o_ref.dtype)


def _attention(qk, rest, lam, subln, n_ctx, lam_init):
    B, T, _ = qk.shape
    H = DA_HEADS
    S = T - n_ctx
    tq = _pick(math.gcd(n_ctx, S), (256, 128))
    tk = _pick(S, (2048, 1024, 512, 256, 128))
    kern = functools.partial(_attn_kernel, n_ctx=n_ctx, tk=tk, lam_init=lam_init)
    return pl.pallas_call(
        kern, grid=(B, H, T // tq),
        in_specs=[pl.BlockSpec((4, DA_DIM), lambda b, h, i: (0, 0)),
                  pl.BlockSpec((None, tq, DA_VDIM), lambda b, h, i: (b, i, h)),
                  pl.BlockSpec((None, T, DA_VDIM), lambda b, h, i: (b, 0, H + h)),
                  pl.BlockSpec((None, T, DA_VDIM), lambda b, h, i: (b, 0, h)),
                  pl.BlockSpec((1, DA_VDIM), lambda b, h, i: (0, 0))],
        out_specs=pl.BlockSpec((None, tq, DA_VDIM), lambda b, h, i: (b, i, h)),
        out_shape=jax.ShapeDtypeStruct((B, T, H * DA_VDIM), BF16),
        scratch_shapes=[pltpu.VMEM((2 * tq, LANES), F32), pltpu.VMEM((2 * tq, LANES), F32),
                        pltpu.VMEM((2 * tq, DA_VDIM), F32)],
        compiler_params=_cparams("parallel", "parallel", "arbitrary"), name="diff_attention",
    )(lam, qk, qk, rest, subln.reshape(1, DA_VDIM))


def _dn_prep_kernel(x_ref, w_ref, o_ref, pad_sc, *, n_ctx, rows):
    T = x_ref.shape[0]
    halo = SUBLANES
    cb = pl.program_id(1)
    pad_sc[0:halo, :] = jnp.zeros((halo, LANES), F32)
    pad_sc[halo + T:2 * halo + T, :] = jnp.zeros((halo, LANES), F32)
    pad_sc[halo:halo + T, :] = x_ref[...].astype(F32)
    w = w_ref[...]
    is_qk = cb < 2 * DN_HEADS
    post = jnp.where(cb < DN_HEADS, DN_DK ** -0.5, 1.0).astype(F32)
    pad = DN_CONV // 2

    def body(c, carry):
        r0 = pl.multiple_of(c * rows, rows)
        win = pad_sc[pl.ds(r0, rows + 2 * halo), :]
        first = (r0 == 0) | (r0 == n_ctx)
        last = (r0 + rows == n_ctx) | (r0 + rows == T)
        ridx = lax.broadcasted_iota(jnp.int32, win.shape, 0)
        outside = (ridx < jnp.where(first, halo, 0)) | (ridx >= jnp.where(last, halo + rows, rows + 2 * halo))
        win = jnp.where(outside, 0.0, win)
        y = jnp.zeros((rows, LANES), F32)
        for j in range(DN_CONV):
            d = j - pad
            y = y + win[halo + d:halo + d + rows, :] * w[j:j + 1]
        y = _silu(y)
        yn = y * lax.rsqrt(jnp.sum(y * y, axis=-1, keepdims=True) + EPS) * post
        o_ref[pl.ds(r0, rows), :] = jnp.where(is_qk, yn, y).astype(o_ref.dtype)
        return carry

    lax.fori_loop(0, T // rows, body, 0)


def _dn_prep(rest, conv_w, n_ctx, col0):
    B, T, _ = rest.shape
    nblk = 3 * DN_HEADS
    rows = _pick(math.gcd(n_ctx, T - n_ctx), (256, 128))
    cw = jnp.zeros((SUBLANES, nblk * LANES), F32).at[:DN_CONV].set(conv_w)
    kern = functools.partial(_dn_prep_kernel, n_ctx=n_ctx, rows=rows)
    return pl.pallas_call(
        kern, grid=(B, nblk),
        in_specs=[pl.BlockSpec((None, T, LANES), lambda b, c: (b, 0, col0 // LANES + c)),
                  pl.BlockSpec((SUBLANES, LANES), lambda b, c: (0, c))],
        out_specs=pl.BlockSpec((None, T, LANES), lambda b, c: (b, 0, c)),
        out_shape=jax.ShapeDtypeStruct((B, T, nblk * LANES), BF16),
        scratch_shapes=[pltpu.VMEM((T + 2 * SUBLANES, LANES), F32)],
        compiler_params=_cparams("parallel", "parallel"), name="dn_prep",
    )(rest, cw)


GDN_HEADS_PER_STEP = 2
GDN_CHUNKS_PER_ITER = 4
GDN_INV_PASSES = 1


def _split_dot(a, b, passes):
    a_hi = a.astype(BF16)
    b_hi = b.astype(BF16)
    out = jnp.dot(a_hi, b_hi, preferred_element_type=F32)
    if passes >= 3:
        a_lo = (a - a_hi.astype(F32)).astype(BF16)
        b_lo = (b - b_hi.astype(F32)).astype(BF16)
        out = out + jnp.dot(a_hi, b_lo, preferred_element_type=F32) + jnp.dot(a_lo, b_hi, preferred_element_type=F32)
    return out


def _gdn_kernel(q_ref, k_ref, v_ref, gt_ref, z_ref, ng_ref, o_ref,
                w_sc, qg_sc, kt_sc, a_sc, u_sc, egl_sc, oacc_sc, s_sc, *, n_ctx):
    C = DN_CHUNK
    C2 = 2 * C
    T = q_ref.shape[0]
    nc = T // C
    ncc = n_ctx // C
    hp = pl.program_id(1)

    row = lax.broadcasted_iota(jnp.int32, (C2, C2), 0)
    col = lax.broadcasted_iota(jnp.int32, (C2, C2), 1)
    fwd_row = row < C
    rc_xor = row ^ col
    ahead = (col - row) * jnp.where(fwd_row, 1, -1)
    same_dir = rc_xor < C
    incl = same_dir & (ahead <= 0)
    strict = same_dir & (ahead < 0)
    eye = (row == col).astype(F32)
    mcs = jnp.where(incl, 1.0, 0.0).astype(BF16)

    def p1_load(c, s):
        r0 = pl.multiple_of(c * C, C)
        head = hp * GDN_HEADS_PER_STEP + s
        lo, hi = s * LANES, (s + 1) * LANES
        k = k_ref[pl.ds(r0, C), lo:hi]
        q = q_ref[pl.ds(r0, C), lo:hi]
        v = v_ref[pl.ds(r0, C), lo:hi]
        kk = jnp.concatenate([k, k], axis=0)
        qq = jnp.concatenate([q, q], axis=0)
        vf = jnp.concatenate([v, v], axis=0).astype(F32)
        x = gt_ref[pl.ds(r0, C), :]
        x2 = jnp.concatenate([x, x], axis=0)
        bsel = jnp.where(fwd_row, head, DN_HEADS + head)
        gsel = bsel + 2 * DN_HEADS
        beta = jnp.sum(jnp.where(col == bsel, x2, 0.0), axis=-1, keepdims=True)
        glog = jnp.sum(jnp.where(col == gsel, x2, 0.0), axis=-1, keepdims=True)
        return dict(c=c, s=s, kk=kk, qq=qq, vf=vf, beta=beta, glog=glog)

    def p1_prep(d):
        g_rem = jnp.broadcast_to(d["glog"], (C2, C2))
        gcum = jnp.zeros((C2, C2), F32)
        for _ in range(3):
            piece = g_rem.astype(BF16)
            gcum = gcum + jnp.dot(mcs, piece, preferred_element_type=F32)
            g_rem = g_rem - piece.astype(F32)
        decay = jnp.exp(jnp.where(incl, gcum - gcum.T, -jnp.inf))
        kkt = lax.dot_general(d["kk"], d["kk"], NT_DIMS, preferred_element_type=F32)
        qkt = lax.dot_general(d["qq"], d["kk"], NT_DIMS, preferred_element_type=F32)
        d.update(gcum=gcum, decay=decay, qkt=qkt, lmat=jnp.where(strict, d["beta"] * kkt * decay, 0.0), tinv=eye)

    def p1_finish(d):
        c, s, gcum, beta = d["c"], d["s"], d["gcum"], d["beta"]
        kf = d["kk"].astype(F32)
        eg = jnp.exp(gcum)
        rhs = jnp.concatenate([(d["vf"] * beta).astype(BF16), (kf * beta * eg).astype(BF16)], axis=1)
        uw = jnp.dot(d["tinv"].astype(BF16), rhs, preferred_element_type=F32)
        glast = jnp.where(fwd_row, gcum[C - 1:C, :], gcum[C:C + 1, :])
        ktail = kf * jnp.exp(glast - gcum)
        u_sc[s, c] = uw[:, 0:LANES]
        w_sc[s, c] = uw[:, LANES:2 * LANES].astype(BF16)
        qg_sc[s, c] = (d["qq"].astype(F32) * eg).astype(BF16)
        a_sc[s, c] = (d["qkt"] * d["decay"]).astype(BF16)
        kt_sc[s, c] = ktail.T.astype(BF16)
        egl_sc[s, c] = jnp.exp(jnp.concatenate([jnp.broadcast_to(gcum[C - 1:C, :], (4, LANES)),
                                                jnp.broadcast_to(gcum[C:C + 1, :], (4, LANES))], axis=0))

    def p1_body(i, carry):
        probs = [p1_load(i * GDN_CHUNKS_PER_ITER + j, s)
                 for j in range(GDN_CHUNKS_PER_ITER) for s in range(GDN_HEADS_PER_STEP)]
        for d in probs:
            p1_prep(d)
        for lvl in range(int(math.log2(C))):
            for d in probs:
                d["x"] = _split_dot(d["tinv"], jnp.where((rc_xor >> lvl) == 1, d["lmat"], 0.0), GDN_INV_PASSES)
            for d in probs:
                d["tinv"] = d["tinv"] - _split_dot(d["x"], d["tinv"], GDN_INV_PASSES)
        for d in probs:
            p1_finish(d)
        return carry

    lax.fori_loop(0, nc // GDN_CHUNKS_PER_ITER, p1_body, 0)

    oacc_sc[...] = jnp.zeros(oacc_sc.shape, F32)
    s_sc[...] = jnp.zeros(s_sc.shape, F32)
    lane_b = lax.broadcasted_iota(jnp.int32, (LANES, LANES), 1)
    zpad = jnp.zeros((C, LANES), BF16)

    def p2_body(i, carry):
        cf = i
        cb = jnp.where(i < ncc, ncc - 1 - i, nc - 1 + ncc - i)
        rf = pl.multiple_of(cf * C, C)
        rb = pl.multiple_of(cb * C, C)
        heads = range(GDN_HEADS_PER_STEP)
        st = [s_sc[s] for s in heads]
        lhs1 = [jnp.concatenate([
            jnp.concatenate([w_sc[s, cf, 0:C, :], zpad], axis=1),
            jnp.concatenate([zpad, w_sc[s, cb, C:C2, :]], axis=1),
            jnp.concatenate([qg_sc[s, cf, 0:C, :], zpad], axis=1),
            jnp.concatenate([zpad, qg_sc[s, cb, C:C2, :]], axis=1)], axis=0) for s in heads]
        zk = jnp.zeros((LANES, C2), BF16)
        lhs2 = [jnp.concatenate([a_sc[s, cf, 0:C, :], a_sc[s, cb, C:C2, :],
                                 jnp.where(lane_b < C, kt_sc[s, cf], zk),
                                 jnp.where(lane_b >= C, kt_sc[s, cb], zk)], axis=0) for s in heads]
        u = [jnp.concatenate([u_sc[s, cf, 0:C, :], u_sc[s, cb, C:C2, :]], axis=0) for s in heads]
        scale = [jnp.concatenate([jnp.broadcast_to(egl_sc[s, cf][0:1], (DN_DK, DN_DV)),
                                  jnp.broadcast_to(egl_sc[s, cb][4:5], (DN_DK, DN_DV))], axis=0) for s in heads]
        r1 = [jnp.dot(lhs1[s], st[s].astype(BF16), preferred_element_type=F32) for s in heads]
        vnew = [(u[s] - r1[s][0:C2]).astype(BF16) for s in heads]
        r2 = [jnp.dot(lhs2[s], vnew[s], preferred_element_type=F32) for s in heads]
        for s in heads:
            s_sc[s] = st[s] * scale[s] + r2[s][C2:C2 + 2 * DN_DK]
        for s in heads:
            oacc_sc[s, pl.ds(rf, C), :] += r1[s][C2:C2 + C] + r2[s][0:C]
            oacc_sc[s, pl.ds(rb, C), :] += r1[s][C2 + C:2 * C2] + r2[s][C:C2]
        return carry

    lax.fori_loop(0, nc, p2_body, 0)

    ng = ng_ref[...]
    rows = _pick(T, (256, 128))

    def fin_body(c, carry):
        r0 = pl.multiple_of(c * rows, rows)
        for s in range(GDN_HEADS_PER_STEP):
            o = oacc_sc[s, pl.ds(r0, rows), :]
            zf = z_ref[pl.ds(r0, rows), s * LANES:(s + 1) * LANES].astype(F32)
            ms = jnp.mean(o * o, axis=-1, keepdims=True)
            o_ref[pl.ds(r0, rows), s * LANES:(s + 1) * LANES] = (o * lax.rsqrt(ms + EPS) * ng * _silu(zf)).astype(o_ref.dtype)
        return carry

    lax.fori_loop(0, T // rows, fin_body, 0)


def _gdn(dnq, gates, rest, z_col0, norm_g, n_ctx):
    B, T, _ = dnq.shape
    H = DN_HEADS
    hps = GDN_HEADS_PER_STEP
    wblk = hps * LANES
    nc = T // DN_CHUNK
    assert nc % GDN_CHUNKS_PER_ITER == 0
    C2 = 2 * DN_CHUNK
    kern = functools.partial(_gdn_kernel, n_ctx=n_ctx)
    nb = H // hps
    once = dict(pipeline_mode=pl.Buffered(1))
    return pl.pallas_call(
        kern, grid=(B, nb),
        in_specs=[pl.BlockSpec((None, T, wblk), lambda b, h: (b, 0, h), **once),
                  pl.BlockSpec((None, T, wblk), lambda b, h: (b, 0, nb + h), **once),
                  pl.BlockSpec((None, T, wblk), lambda b, h: (b, 0, 2 * nb + h), **once),
                  pl.BlockSpec((None, T, LANES), lambda b, h: (b, 0, 0), **once),
                  pl.BlockSpec((None, T, wblk), lambda b, h: (b, 0, z_col0 // wblk + h), **once),
                  pl.BlockSpec((1, DN_DV), lambda b, h: (0, 0))],
        out_specs=pl.BlockSpec((None, T, wblk), lambda b, h: (b, 0, h)),
        out_shape=jax.ShapeDtypeStruct((B, T, H * DN_DV), BF16),
        scratch_shapes=[pltpu.VMEM((hps, nc, C2, LANES), BF16),
                        pltpu.VMEM((hps, nc, C2, LANES), BF16),
                        pltpu.VMEM((hps, nc, LANES, C2), BF16),
                        pltpu.VMEM((hps, nc, C2, C2), BF16),
                        pltpu.VMEM((hps, nc, C2, LANES), F32),
                        pltpu.VMEM((hps, nc, SUBLANES, LANES), F32),
                        pltpu.VMEM((hps, T, LANES), F32),
                        pltpu.VMEM((hps, 2 * DN_DK, DN_DV), F32)],
        compiler_params=_cparams("parallel", "arbitrary"), name="gated_deltanet",
    )(dnq, dnq, dnq, gates, rest, norm_g.reshape(1, DN_DV))


def _merge_kernel(da_ref, dn_ref, ga_ref, gb_ref, x_ref, wa_ref, wb_ref, wo_ref, g_ref, m_ref, xo_ref, h_ref):
    ya = jnp.dot(da_ref[...], wa_ref[...], preferred_element_type=F32)
    yb = jnp.dot(dn_ref[...], wb_ref[...], preferred_element_type=F32)
    y = jax.nn.sigmoid(ga_ref[...].astype(F32)) * ya + jax.nn.sigmoid(gb_ref[...].astype(F32)) * yb
    y2 = jnp.dot(y.astype(BF16), wo_ref[...], preferred_element_type=F32)
    m = m_ref[...]
    x = x_ref[...] + m[2:3] * y2
    xo_ref[...] = x
    h_ref[...] = _rms_mod(x, g_ref[...], m, 3, 4).astype(BF16)


def _merge(da, dn, rest, ga_col0, xs, w_ba, w_bb, w_o, g2, modsel, n_ctx):
    B, T, D = xs.shape
    tm = _pick(math.gcd(n_ctx, T - n_ctx), (256, 128))
    tok, gain, mod = _seg_specs(tm, n_ctx // tm, D)
    wspec = pl.BlockSpec((D, D), lambda b, t: (0, 0))
    ga_blk = ga_col0 // D
    return pl.pallas_call(
        _merge_kernel, grid=(B, T // tm),
        in_specs=[tok, tok,
                  pl.BlockSpec((None, tm, D), lambda b, t: (b, t, ga_blk)),
                  pl.BlockSpec((None, tm, D), lambda b, t: (b, t, ga_blk + 1)),
                  tok, wspec, wspec, wspec, gain, mod],
        out_specs=[tok, tok],
        out_shape=[jax.ShapeDtypeStruct((B, T, D), F32), jax.ShapeDtypeStruct((B, T, D), BF16)],
        compiler_params=_cparams("parallel", "parallel"), name="merge",
    )(da, dn, rest, rest, xs, w_ba, w_bb, w_o, g2.reshape(1, D), modsel)


def _desc_tops(x, n):
    tops = []
    for _ in range(n):
        m = jnp.max(x, axis=0, keepdims=True)
        tops.append(m)
        x = jnp.where(x == m, -jnp.inf, x)
    return tops


def _peer_prep_kernel(h_ref, wq_ref, keys_ref, a_ref, b_ref, kap_ref):
    tm = h_ref.shape[0]
    q = jnp.dot(h_ref[...], wq_ref[...], preferred_element_type=F32).astype(BF16)
    for h in range(PK_HEADS):
        st = [lax.dot_general(keys_ref[2 * h + p], q[:, (2 * h + p) * PK_HALF:(2 * h + p + 1) * PK_HALF],
                              NT_DIMS, preferred_element_type=F32) for p in range(2)]
        ta = _desc_tops(st[0], PK_TOPK + 1)
        tb = _desc_tops(st[1], PK_TOPK + 1)
        tbs = jnp.concatenate(tb[:PK_TOPK], axis=0)
        tb8 = tbs[0:SUBLANES]
        row8 = lax.broadcasted_iota(jnp.int32, tb8.shape, 0)
        pieces = [ta[0] + tbs, ta[1] + tb8]
        for r in range(2, SUBLANES):
            pieces.append(jnp.where(row8 < PK_TOPK // (r + 1), ta[r] + tb8, -jnp.inf))
        pieces.append(jnp.concatenate(ta[SUBLANES:PK_TOPK], axis=0) + tb[0])
        cand = jnp.concatenate(pieces, axis=0)
        best = _desc_tops(cand, PK_TOPK + 1)
        mx = best[0]
        zsum = best[0] * 0.0
        for r in range(PK_TOPK):
            zsum = zsum + jnp.exp(best[r] - mx)
        nxt = jnp.maximum(best[PK_TOPK], jnp.maximum(ta[PK_TOPK] + tb[0], ta[0] + tb[PK_TOPK]))
        thr = 0.5 * (best[PK_TOPK - 1] + nxt)
        a_ref[h] = jnp.exp(st[0] - ta[0])
        b_ref[h] = (jnp.exp(st[1] - tb[0]) / zsum).astype(b_ref.dtype)
        kap_ref[h:h + 1, :] = jnp.exp(thr - mx) / zsum


def _peer_prep(h2, wq, keys):
    M, D = h2.shape
    tm = _pick(M, (256, 128))
    nk = 2 * PK_HEADS
    return pl.pallas_call(
        _peer_prep_kernel, grid=(M // tm,),
        in_specs=[pl.BlockSpec((tm, D), lambda i: (i, 0)),
                  pl.BlockSpec((D, nk * PK_HALF), lambda i: (0, 0)),
                  pl.BlockSpec((nk, N_KEYS, PK_HALF), lambda i: (0, 0, 0))],
        out_specs=[pl.BlockSpec((PK_HEADS, N_KEYS, tm), lambda i: (0, 0, i)),
                   pl.BlockSpec((PK_HEADS, N_KEYS, tm), lambda i: (0, 0, i)),
                   pl.BlockSpec((PK_HEADS, tm), lambda i: (0, i))],
        out_shape=[jax.ShapeDtypeStruct((PK_HEADS, N_KEYS, M), F32),
                   jax.ShapeDtypeStruct((PK_HEADS, N_KEYS, M), BF16),
                   jax.ShapeDtypeStruct((PK_HEADS, M), F32)],
        compiler_params=_cparams("parallel"), name="peer_prep",
    )(h2, wq, keys)


def _peer_dense_kernel(h_ref, u_ref, vt_ref, a_ref, b_ref, kap_ref, o_ref, acc_sc, *, sub):
    e = pl.program_id(1)
    tm = h_ref.shape[0]
    eb = u_ref.shape[0]
    ni = eb // N_KEYS

    @pl.when(e == 0)
    def _():
        acc_sc[...] = jnp.zeros(acc_sc.shape, F32)

    for t in range(tm // sub):
        tok = slice(t * sub, (t + 1) * sub)
        sc = lax.dot_general(u_ref[...], h_ref[tok, :], NT_DIMS, preferred_element_type=F32)

        def rows16(x):
            x16 = jnp.broadcast_to(x, (2 * SUBLANES, sub)).astype(BF16)
            return jnp.concatenate([x16] * (N_KEYS // (2 * SUBLANES)), axis=0)

        kaps = [rows16(kap_ref[h:h + 1, tok]) for h in range(PK_HEADS)]
        zero = jnp.zeros((N_KEYS, sub), BF16)
        was = []
        for il in range(ni):
            w = zero
            for h in range(PK_HEADS):
                prod = b_ref[h, :, tok] * rows16(a_ref[h, il:il + 1, tok])
                w = w + jnp.where(prod >= kaps[h], prod, zero)
            s = sc[il * N_KEYS:(il + 1) * N_KEYS, :]
            act = 0.5 * s * (1.0 + lax.erf(s * (2.0 ** -0.5)))
            was.append(w * act.astype(BF16))
        acc_sc[:, tok] += jnp.dot(vt_ref[...], jnp.concatenate(was, axis=0), preferred_element_type=F32)

    @pl.when(e == pl.num_programs(1) - 1)
    def _():
        o_ref[...] = acc_sc[...].T


def _peer_dense(h2, u, vt, a, b, kap):
    M, D = h2.shape
    E = u.shape[0]
    tm = _pick(M, (1024, 512, 256, 128))
    sub = _pick(tm, (256, 128))
    eb = SUBLANES * N_KEYS
    kern = functools.partial(_peer_dense_kernel, sub=sub)
    return pl.pallas_call(
        kern, grid=(M // tm, E // eb),
        in_specs=[pl.BlockSpec((tm, D), lambda i, e: (i, 0)),
                  pl.BlockSpec((eb, D), lambda i, e: (e, 0)),
                  pl.BlockSpec((D, eb), lambda i, e: (0, e)),
                  pl.BlockSpec((PK_HEADS, SUBLANES, tm), lambda i, e: (0, e, i)),
                  pl.BlockSpec((PK_HEADS, N_KEYS, tm), lambda i, e: (0, 0, i)),
                  pl.BlockSpec((PK_HEADS, tm), lambda i, e: (0, i))],
        out_specs=pl.BlockSpec((tm, D), lambda i, e: (i, 0)),
        out_shape=jax.ShapeDtypeStruct((M, D), F32),
        scratch_shapes=[pltpu.VMEM((D, tm), F32)],
        compiler_params=_cparams("parallel", "arbitrary"), name="peer_dense",
    )(h2, u, vt, a, b, kap)


def _rope_tables(n_ctx, S):
    n_freq = DA_DIM // 4
    n_rows = S // GRID_W
    row = jnp.repeat(jnp.arange(n_rows, dtype=F32), GRID_W)
    col = jnp.tile(jnp.arange(GRID_W, dtype=F32), n_rows)
    inv = ROPE_BASE ** (-jnp.arange(n_freq, dtype=F32) / n_freq)
    ang = jnp.concatenate([row[:, None] * inv, col[:, None] * inv], axis=-1)
    cos = jnp.concatenate([jnp.ones((n_ctx, DA_DIM // 2), F32), jnp.cos(ang)], axis=0)
    sin = jnp.concatenate([jnp.zeros((n_ctx, DA_DIM // 2), F32), jnp.sin(ang)], axis=0)
    return jnp.concatenate([cos] * 4, axis=1), jnp.concatenate([-sin, sin, -sin, sin], axis=1)


def _deinterleave_perm():
    idx = np.arange(DA_HEADS * 2 * DA_DIM).reshape(DA_HEADS * 2, DA_DIM // 2, 2)
    return np.concatenate([idx[:, :, 0], idx[:, :, 1]], axis=1).reshape(-1)


def kernel(x, c, ctx, c_ctx, w_ada, b_ada, norm1_g, norm2_g, w_in, da_lambda, da_subln, dn_conv, dn_a_log,
           dn_dt_bias, dn_norm, w_branch_a, w_branch_b, w_out, peer_wq, peer_keys, peer_u, peer_v, final_g):
    B, S, D = x.shape
    n_ctx = ctx.shape[1]
    T = n_ctx + S
    L = w_ada.shape[0]
    M = B * T
    assert B < SUBLANES and S % GRID_W == 0 and n_ctx % DN_CHUNK == 0 and S % DN_CHUNK == 0
    da_qk = DA_HEADS * 2 * DA_DIM
    da_w = DA_HEADS * DA_VDIM
    dn_qkv = 3 * DN_HEADS * DN_DK
    dn_w = DN_HEADS * DN_DV
    nh2 = 2 * DN_HEADS

    c_all = jnp.zeros((SUBLANES, D), F32).at[:B].set(c).at[B].set(c_ctx)
    mod = _ada(c_all, w_ada, b_ada).reshape(L, SUBLANES, 6, D)
    modsel = jnp.stack([jnp.broadcast_to(mod[:, B:B + 1], (L, B, 6, D)), mod[:, :B]], axis=2)

    cos_t, sin_t = _rope_tables(n_ctx, S)
    cos_m = jnp.tile(cos_t, (B, 1))
    sin_m = jnp.tile(sin_t, (B, 1))
    perm = _deinterleave_perm()

    xs = jnp.concatenate([ctx, x], axis=1)
    f = None
    for l in range(L):
        lam_init = 0.8 - 0.6 * math.exp(-0.3 * l)
        w = w_in[l]
        o = 0
        w_q = w[:, o:o + da_qk][:, perm]; o += da_qk
        w_k = w[:, o:o + da_qk][:, perm]; o += da_qk
        w_v = w[:, o:o + da_w]; o += da_w
        w_dn = w[:, o:o + dn_qkv]; o += dn_qkv
        w_z = w[:, o:o + dn_w]; o += dn_w
        w_b = w[:, o:o + nh2]; o += nh2
        w_a = w[:, o:o + nh2]; o += nh2
        w_ga = w[:, o:o + D]; o += D
        w_gb = w[:, o:o + D]; o += D
        w_qk = jnp.concatenate([w_q, w_k], axis=1).astype(BF16)
        w_rest = jnp.concatenate([w_v, w_dn, w_z, w_ga, w_gb], axis=1).astype(BF16)
        dn_col0, z_col0, ga_col0 = da_w, da_w + dn_qkv, da_w + dn_qkv + dn_w
        w_ba_pad = jnp.zeros((D, LANES), F32).at[:, :nh2].set(w_b).at[:, nh2:2 * nh2].set(w_a).astype(BF16)
        gate_par = (jnp.zeros((SUBLANES, LANES), F32)
                    .at[0, nh2:2 * nh2].set(dn_a_log[l].reshape(-1))
                    .at[1, nh2:2 * nh2].set(dn_dt_bias[l].reshape(-1)))

        if l == 0:
            h = _modulate(xs, norm1_g[l], modsel[l], n_ctx)
        else:
            xs, h = _resid_modulate(xs, f, modsel[l - 1], norm1_g[l], modsel[l], n_ctx)
        hm = h.reshape(M, D)

        qk = _proj(hm, w_qk, out_dtype=BF16, kernel=_proj_rope_kernel, extra=(cos_m, sin_m),
                   extra_specs=lambda tm: [pl.BlockSpec((tm, LANES), lambda j, i: (i, 0))] * 2,
                   name="proj_qk_rope").reshape(B, T, 2 * da_qk)
        rest = _proj(hm, w_rest, out_dtype=BF16, extra_specs=lambda tm: [], name="proj_rest").reshape(B, T, -1)
        gates = _proj(hm, w_ba_pad, out_dtype=F32, kernel=_proj_gates_kernel, extra=(gate_par,),
                      extra_specs=lambda tm: [pl.BlockSpec((SUBLANES, LANES), lambda j, i: (0, 0))],
                      name="proj_gates").reshape(B, T, LANES)

        da = _attention(qk, rest, da_lambda[l], da_subln[l], n_ctx, lam_init)
        dnq = _dn_prep(rest, dn_conv[l], n_ctx, dn_col0)
        dn = _gdn(dnq, gates, rest, z_col0, dn_norm[l], n_ctx)

        xs, h2 = _merge(da, dn, rest, ga_col0, xs, w_branch_a[l].astype(BF16), w_branch_b[l].astype(BF16),
                        w_out[l].astype(BF16), norm2_g[l], modsel[l], n_ctx)
        h2m = h2.reshape(M, D)
        pa, pb, kap = _peer_prep(h2m, peer_wq[l].astype(BF16),
                                 peer_keys[l].reshape(2 * PK_HEADS, N_KEYS, PK_HALF).astype(BF16))
        f = _peer_dense(h2m, peer_u[l].astype(BF16), peer_v[l].T.astype(BF16), pa, pb, kap).reshape(B, T, D)

    return _final(xs, f, modsel[L - 1], final_g, n_ctx)
```

```python
import functools
import math

import jax
import jax.numpy as jnp
import numpy as np
from jax import lax
from jax.experimental import pallas as pl
from jax.experimental.pallas import tpu as pltpu

F32 = jnp.float32
BF16 = jnp.bfloat16
HIGHEST = lax.Precision.HIGHEST

GRID_W = 64
EPS = 1e-6
ROPE_BASE = 10000.0
DA_HEADS = 8
DA_DIM = 64
DA_VDIM = 2 * DA_DIM
DN_HEADS = 8
DN_DK = 128
DN_DV = 128
DN_CONV = 5
DN_CHUNK = 64
PK_HEADS = 8
N_KEYS = 128
PK_TOPK = 16
PK_HALF = 128
GELU_FOLD = 2.0 ** -0.5

LANES = 128
SUBLANES = 8
VMEM_LIMIT_BYTES = 56 * 1024 * 1024

NT_DIMS = (((1,), (1,)), ((), ()))


def _cparams(*sem, flags=None):
    return pltpu.CompilerParams(dimension_semantics=sem, vmem_limit_bytes=VMEM_LIMIT_BYTES, flags=flags)


def _pick(n, cands):
    for c in cands:
        if n % c == 0:
            return c
    raise ValueError(f"no tile in {cands} divides {n}")


def _silu(x):
    return x * jax.nn.sigmoid(x)


def _ada_kernel(c_ref, w_ref, b_ref, o_ref):
    sc = _silu(c_ref[...])
    o_ref[0] = jnp.dot(sc, w_ref[0], preferred_element_type=F32, precision=HIGHEST) + b_ref[0]


def _ada(c_all, w_ada, b_ada):
    L, D, N = w_ada.shape
    tn = _pick(N, (1536, 1024, 512, 256, 128))
    return pl.pallas_call(
        _ada_kernel,
        grid=(L, N // tn),
        in_specs=[pl.BlockSpec((SUBLANES, D), lambda l, j: (0, 0)),
                  pl.BlockSpec((1, D, tn), lambda l, j: (l, 0, j)),
                  pl.BlockSpec((1, 1, tn), lambda l, j: (l, 0, j))],
        out_specs=pl.BlockSpec((1, SUBLANES, tn), lambda l, j: (l, 0, j)),
        out_shape=jax.ShapeDtypeStruct((L, SUBLANES, N), F32),
        compiler_params=_cparams("parallel", "parallel"),
        name="ada",
    )(c_all, w_ada, b_ada.reshape(L, 1, N))


def _rms_mod(x, g, m, shift_row, scale_row):
    ms = jnp.mean(x * x, axis=-1, keepdims=True)
    y = x * lax.rsqrt(ms + EPS) * g
    return y * (1.0 + m[scale_row:scale_row + 1]) + m[shift_row:shift_row + 1]


def _modulate_kernel(x_ref, g_ref, m_ref, h_ref):
    h_ref[...] = _rms_mod(x_ref[...], g_ref[...], m_ref[...], 0, 1).astype(BF16)


def _resid_modulate_kernel(x_ref, f_ref, mp_ref, g_ref, m_ref, xo_ref, h_ref):
    x = x_ref[...] + mp_ref[...][5:6] * f_ref[...]
    xo_ref[...] = x
    h_ref[...] = _rms_mod(x, g_ref[...], m_ref[...], 0, 1).astype(BF16)


def _seg_specs(tm, n_ctx_tiles, D):
    tok = pl.BlockSpec((None, tm, D), lambda b, t: (b, t, 0))
    gain = pl.BlockSpec((1, D), lambda b, t: (0, 0))
    mod = pl.BlockSpec((None, None, 6, D), lambda b, t: (b, jnp.where(t >= n_ctx_tiles, 1, 0), 0, 0))
    return tok, gain, mod


def _modulate(xs, g, modsel, n_ctx):
    B, T, D = xs.shape
    tm = _pick(math.gcd(n_ctx, T - n_ctx), (256, 128))
    tok, gain, mod = _seg_specs(tm, n_ctx // tm, D)
    return pl.pallas_call(
        _modulate_kernel, grid=(B, T // tm),
        in_specs=[tok, gain, mod], out_specs=tok,
        out_shape=jax.ShapeDtypeStruct((B, T, D), BF16),
        compiler_params=_cparams("parallel", "parallel"), name="modulate",
    )(xs, g.reshape(1, D), modsel)


def _resid_modulate(xs, f, modsel_prev, g, modsel, n_ctx):
    B, T, D = xs.shape
    tm = _pick(math.gcd(n_ctx, T - n_ctx), (256, 128))
    tok, gain, mod = _seg_specs(tm, n_ctx // tm, D)
    return pl.pallas_call(
        _resid_modulate_kernel, grid=(B, T // tm),
        in_specs=[tok, tok, mod, gain, mod], out_specs=[tok, tok],
        out_shape=[jax.ShapeDtypeStruct((B, T, D), F32), jax.ShapeDtypeStruct((B, T, D), BF16)],
        compiler_params=_cparams("parallel", "parallel"), name="resid_modulate",
    )(xs, f, modsel_prev, g.reshape(1, D), modsel)


def _final_kernel(x_ref, f_ref, mp_ref, g_ref, o_ref):
    x = x_ref[...] + mp_ref[...][5:6] * f_ref[...]
    ms = jnp.mean(x * x, axis=-1, keepdims=True)
    o_ref[...] = x * lax.rsqrt(ms + EPS) * g_ref[...]


def _final(xs, f, modsel_prev, g, n_ctx):
    B, T, D = xs.shape
    S = T - n_ctx
    tm = _pick(math.gcd(n_ctx, S), (256, 128))
    off = n_ctx // tm
    tok_in = pl.BlockSpec((None, tm, D), lambda b, t: (b, t + off, 0))
    return pl.pallas_call(
        _final_kernel, grid=(B, S // tm),
        in_specs=[tok_in, tok_in,
                  pl.BlockSpec((None, None, 6, D), lambda b, t: (b, 1, 0, 0)),
                  pl.BlockSpec((1, D), lambda b, t: (0, 0))],
        out_specs=pl.BlockSpec((None, tm, D), lambda b, t: (b, t, 0)),
        out_shape=jax.ShapeDtypeStruct((B, S, D), F32),
        compiler_params=_cparams("parallel", "parallel"), name="final_norm",
    )(xs, f, modsel_prev, g.reshape(1, D))


def _proj_kernel(h_ref, w_ref, o_ref):
    o_ref[...] = jnp.dot(h_ref[...], w_ref[...], preferred_element_type=F32).astype(o_ref.dtype)


def _proj_rope_kernel(h_ref, w_ref, cos_ref, sin_ref, o_ref):
    acc = jnp.dot(h_ref[...], w_ref[...], preferred_element_type=F32)
    tn = acc.shape[1]
    half = DA_DIM // 2
    lane = lax.broadcasted_iota(jnp.int32, acc.shape, 1)
    partner = jnp.where((lane & (DA_DIM - 1)) < half,
                        pltpu.roll(acc, tn - half, axis=1),
                        pltpu.roll(acc, half, axis=1))
    reps = tn // LANES
    cos = jnp.concatenate([cos_ref[...]] * reps, axis=1)
    sin = jnp.concatenate([sin_ref[...]] * reps, axis=1)
    y = acc * cos + partner * sin
    scale = jnp.where(pl.program_id(0) == 0, DA_DIM ** -0.5 * math.log2(math.e), 1.0).astype(F32)
    o_ref[...] = (y * scale).astype(o_ref.dtype)


def _proj_gates_kernel(h_ref, w_ref, p_ref, o_ref):
    acc = jnp.dot(h_ref[...], w_ref[...], preferred_element_type=F32)
    p = p_ref[...]
    lane = lax.broadcasted_iota(jnp.int32, acc.shape, 1)
    beta = jax.nn.sigmoid(acc)
    g = -jnp.exp(p[0:1]) * jax.nn.softplus(acc + p[1:2])
    o_ref[...] = jnp.where(lane < 2 * DN_HEADS, beta, jnp.where(lane < 4 * DN_HEADS, g, 0.0))


def _proj(h, w, *, out_dtype, kernel=_proj_kernel, extra=(), extra_specs=(), name="proj"):
    M, D = h.shape
    N = w.shape[1]
    tm = _pick(M, (512, 256, 128))
    tn = _pick(N, (1024, 512, 256, 128))
    return pl.pallas_call(
        kernel, grid=(N // tn, M // tm),
        in_specs=[pl.BlockSpec((tm, D), lambda j, i: (i, 0)),
                  pl.BlockSpec((D, tn), lambda j, i: (0, j))] + list(extra_specs(tm)),
        out_specs=pl.BlockSpec((tm, tn), lambda j, i: (i, j)),
        out_shape=jax.ShapeDtypeStruct((M, N), out_dtype),
        compiler_params=_cparams("parallel", "parallel"), name=name,
    )(h, w, *extra)


def _proj_t_kernel(h_ref, wt_ref, o_ref):
    o_ref[...] = lax.dot_general(wt_ref[...], h_ref[...], NT_DIMS, preferred_element_type=F32).astype(o_ref.dtype)


def _proj_t(h, wt, *, out_dtype, name):
    M, D = h.shape
    N = wt.shape[0]
    tm = _pick(M, (512, 256, 128))
    tn = _pick(N, (1024, 512, 256, 128))
    return pl.pallas_call(
        _proj_t_kernel, grid=(N // tn, M // tm),
        in_specs=[pl.BlockSpec((tm, D), lambda j, i: (i, 0)),
                  pl.BlockSpec((tn, D), lambda j, i: (j, 0))],
        out_specs=pl.BlockSpec((tn, tm), lambda j, i: (j, i)),
        out_shape=jax.ShapeDtypeStruct((N, M), out_dtype),
        compiler_params=_cparams("parallel", "parallel"), name=name,
    )(h, wt)


def _attn_kernel(lam_ref, q_ref, k_ref, vt_ref, g_ref, o_ref, *, n_ctx, tk, lam_init):
    tq = q_ref.shape[0]
    T = k_ref.shape[0]
    qi = pl.program_id(2)
    q = q_ref[...]
    lane = lax.broadcasted_iota(jnp.int32, q.shape, 1)
    zero = jnp.zeros_like(q)
    qq = jnp.concatenate([jnp.where(lane < DA_DIM, q, zero), jnp.where(lane >= DA_DIM, q, zero)], axis=0)

    def scores(r0, rows):
        return lax.dot_general(k_ref[r0:r0 + rows, :], qq, NT_DIMS, preferred_element_type=F32)

    def step(state, s, r0, rows):
        m_prev, l_prev, acc = state
        m_new = jnp.maximum(m_prev, jnp.max(s, axis=0, keepdims=True))
        alpha = jnp.exp2(m_prev - m_new)
        p = jnp.exp2(s - m_new)
        pv = jnp.dot(vt_ref[:, r0:r0 + rows], p.astype(BF16), preferred_element_type=F32)
        return m_new, alpha * l_prev + jnp.sum(p, axis=0, keepdims=True), alpha * acc + pv

    lam = lam_ref[...]
    lam_val = (jnp.exp(jnp.sum(lam[0:1] * lam[1:2], axis=-1, keepdims=True))
               - jnp.exp(jnp.sum(lam[2:3] * lam[3:4], axis=-1, keepdims=True)) + lam_init)

    def finish(state):
        _, l_fin, acc = state
        o_all = acc / l_fin
        o = o_all[:, 0:tq] - lam_val * o_all[:, tq:2 * tq]
        ms = jnp.mean(o * o, axis=0, keepdims=True)
        y = o * lax.rsqrt(ms + EPS) * (1.0 - lam_init)
        o_ref[...] = (y.T * g_ref[...]).astype(o_ref.dtype)

    init = (jnp.full((1, 2 * tq), -jnp.inf, F32), jnp.zeros((1, 2 * tq), F32), jnp.zeros((DA_VDIM, 2 * tq), F32))
    def attend(blocks):
        state = init
        s_next = scores(*blocks[0])
        for j, (r0, rows) in enumerate(blocks):
            s_cur = s_next
            if j + 1 < len(blocks):
                s_next = scores(*blocks[j + 1])
            state = step(state, s_cur, r0, rows)
        finish(state)

    @pl.when(qi * tq < n_ctx)
    def _():
        attend([(0, n_ctx)])

    @pl.when(qi * tq >= n_ctx)
    def _():
        attend([(0, n_ctx)] + [(n_ctx + j * tk, tk) for j in range((T - n_ctx) // tk)])


def _attention(qk, vt, lam, subln, n_ctx, lam_init):
    B, T, _ = qk.shape
    H = DA_HEADS
    S = T - n_ctx
    tq = _pick(math.gcd(n_ctx, S), (256, 128))
    tk = _pick(S, (2048, 1024, 512, 256, 128))
    kern = functools.partial(_attn_kernel, n_ctx=n_ctx, tk=tk, lam_init=lam_init)
    return pl.pallas_call(
        kern, grid=(B, H, T // tq),
        in_specs=[pl.BlockSpec((4, DA_DIM), lambda b, h, i: (0, 0)),
                  pl.BlockSpec((None, tq, DA_VDIM), lambda b, h, i: (b, i, h)),
                  pl.BlockSpec((None, T, DA_VDIM), lambda b, h, i: (b, 0, H + h)),
                  pl.BlockSpec((DA_VDIM, T), lambda b, h, i: (h, b)),
                  pl.BlockSpec((1, DA_VDIM), lambda b, h, i: (0, 0))],
        out_specs=pl.BlockSpec((None, tq, DA_VDIM), lambda b, h, i: (b, i, h)),
        out_shape=jax.ShapeDtypeStruct((B, T, H * DA_VDIM), BF16),
        compiler_params=_cparams("parallel", "parallel", "arbitrary"), name="diff_attention",
    )(lam, qk, qk, vt, subln.reshape(1, DA_VDIM))


def _dn_prep_kernel(x_ref, w_ref, o_ref, pad_sc, *, n_ctx, rows):
    T = x_ref.shape[0]
    halo = SUBLANES
    cb = pl.program_id(1)
    pad_sc[0:halo, :] = jnp.zeros((halo, LANES), F32)
    pad_sc[halo + T:2 * halo + T, :] = jnp.zeros((halo, LANES), F32)
    pad_sc[halo:halo + T, :] = x_ref[...].astype(F32)
    w = w_ref[...]
    is_qk = cb < 2 * DN_HEADS
    post = jnp.where(cb < DN_HEADS, DN_DK ** -0.5, 1.0).astype(F32)
    pad = DN_CONV // 2

    def body(c, carry):
        r0 = pl.multiple_of(c * rows, rows)
        win = pad_sc[pl.ds(r0, rows + 2 * halo), :]
        first = (r0 == 0) | (r0 == n_ctx)
        last = (r0 + rows == n_ctx) | (r0 + rows == T)
        ridx = lax.broadcasted_iota(jnp.int32, win.shape, 0)
        outside = (ridx < jnp.where(first, halo, 0)) | (ridx >= jnp.where(last, halo + rows, rows + 2 * halo))
        win = jnp.where(outside, 0.0, win)
        y = jnp.zeros((rows, LANES), F32)
        for j in range(DN_CONV):
            d = j - pad
            y = y + win[halo + d:halo + d + rows, :] * w[j:j + 1]
        y = _silu(y)
        yn = y * lax.rsqrt(jnp.sum(y * y, axis=-1, keepdims=True) + EPS) * post
        o_ref[pl.ds(r0, rows), :] = jnp.where(is_qk, yn, y).astype(o_ref.dtype)
        return carry

    lax.fori_loop(0, T // rows, body, 0)


def _dn_prep(rest, conv_w, n_ctx, col0):
    B, T, _ = rest.shape
    nblk = 3 * DN_HEADS
    rows = _pick(math.gcd(n_ctx, T - n_ctx), (256, 128))
    cw = jnp.zeros((SUBLANES, nblk * LANES), F32).at[:DN_CONV].set(conv_w)
    kern = functools.partial(_dn_prep_kernel, n_ctx=n_ctx, rows=rows)
    return pl.pallas_call(
        kern, grid=(B, nblk),
        in_specs=[pl.BlockSpec((None, T, LANES), lambda b, c: (b, 0, col0 // LANES + c)),
                  pl.BlockSpec((SUBLANES, LANES), lambda b, c: (0, c))],
        out_specs=pl.BlockSpec((None, T, LANES), lambda b, c: (b, 0, c)),
        out_shape=jax.ShapeDtypeStruct((B, T, nblk * LANES), BF16),
        scratch_shapes=[pltpu.VMEM((T + 2 * SUBLANES, LANES), F32)],
        compiler_params=_cparams("parallel", "parallel"), name="dn_prep",
    )(rest, cw)


GDN_HEADS_PER_STEP = 2
GDN_CHUNKS_PER_ITER = 4
GDN_INV_PASSES = 1


def _split_dot(a, b, passes):
    a_hi = a.astype(BF16)
    b_hi = b.astype(BF16)
    out = jnp.dot(a_hi, b_hi, preferred_element_type=F32)
    if passes >= 3:
        a_lo = (a - a_hi.astype(F32)).astype(BF16)
        b_lo = (b - b_hi.astype(F32)).astype(BF16)
        out = out + jnp.dot(a_hi, b_lo, preferred_element_type=F32) + jnp.dot(a_lo, b_hi, preferred_element_type=F32)
    return out


def _gdn_kernel(q_ref, k_ref, v_ref, gt_ref, z_ref, ng_ref, o_ref,
                w_sc, qg_sc, kt_sc, a_sc, u_sc, egl_sc, oacc_sc, s_sc, *, n_ctx):
    C = DN_CHUNK
    C2 = 2 * C
    T = q_ref.shape[0]
    nc = T // C
    ncc = n_ctx // C
    hp = pl.program_id(1)

    row = lax.broadcasted_iota(jnp.int32, (C2, C2), 0)
    col = lax.broadcasted_iota(jnp.int32, (C2, C2), 1)
    fwd_row = row < C
    rc_xor = row ^ col
    ahead = (col - row) * jnp.where(fwd_row, 1, -1)
    same_dir = rc_xor < C
    incl = same_dir & (ahead <= 0)
    strict = same_dir & (ahead < 0)
    eye = (row == col).astype(F32)
    mcs = jnp.where(incl, 1.0, 0.0).astype(BF16)

    def p1_load(c, s):
        r0 = pl.multiple_of(c * C, C)
        head = hp * GDN_HEADS_PER_STEP + s
        lo, hi = s * LANES, (s + 1) * LANES
        k = k_ref[pl.ds(r0, C), lo:hi]
        q = q_ref[pl.ds(r0, C), lo:hi]
        v = v_ref[pl.ds(r0, C), lo:hi]
        kk = jnp.concatenate([k, k], axis=0)
        qq = jnp.concatenate([q, q], axis=0)
        vf = jnp.concatenate([v, v], axis=0).astype(F32)
        x = gt_ref[pl.ds(r0, C), :]
        x2 = jnp.concatenate([x, x], axis=0)
        bsel = jnp.where(fwd_row, head, DN_HEADS + head)
        gsel = bsel + 2 * DN_HEADS
        beta = jnp.sum(jnp.where(col == bsel, x2, 0.0), axis=-1, keepdims=True)
        glog = jnp.sum(jnp.where(col == gsel, x2, 0.0), axis=-1, keepdims=True)
        return dict(c=c, s=s, kk=kk, qq=qq, vf=vf, beta=beta, glog=glog)

    def p1_prep(d):
        g_rem = jnp.broadcast_to(d["glog"], (C2, C2))
        gcum = jnp.zeros((C2, C2), F32)
        for _ in range(3):
            piece = g_rem.astype(BF16)
            gcum = gcum + jnp.dot(mcs, piece, preferred_element_type=F32)
            g_rem = g_rem - piece.astype(F32)
        decay = jnp.exp(jnp.where(incl, gcum - gcum.T, -jnp.inf))
        kkt = lax.dot_general(d["kk"], d["kk"], NT_DIMS, preferred_element_type=F32)
        qkt = lax.dot_general(d["qq"], d["kk"], NT_DIMS, preferred_element_type=F32)
        d.update(gcum=gcum, decay=decay, qkt=qkt, lmat=jnp.where(strict, d["beta"] * kkt * decay, 0.0), tinv=eye)

    def p1_finish(d):
        c, s, gcum, beta = d["c"], d["s"], d["gcum"], d["beta"]
        kf = d["kk"].astype(F32)
        eg = jnp.exp(gcum)
        rhs = jnp.concatenate([(d["vf"] * beta).astype(BF16), (kf * beta * eg).astype(BF16)], axis=1)
        uw = jnp.dot(d["tinv"].astype(BF16), rhs, preferred_element_type=F32)
        glast = jnp.where(fwd_row, gcum[C - 1:C, :], gcum[C:C + 1, :])
        ktail = kf * jnp.exp(glast - gcum)
        u_sc[s, c] = uw[:, 0:LANES]
        w_sc[s, c] = uw[:, LANES:2 * LANES].astype(BF16)
        qg_sc[s, c] = (d["qq"].astype(F32) * eg).astype(BF16)
        a_sc[s, c] = (d["qkt"] * d["decay"]).astype(BF16)
        kt_sc[s, c] = ktail.T.astype(BF16)
        egl_sc[s, c] = jnp.exp(jnp.concatenate([jnp.broadcast_to(gcum[C - 1:C, :], (4, LANES)),
                                                jnp.broadcast_to(gcum[C:C + 1, :], (4, LANES))], axis=0))

    def p1_body(i, carry):
        probs = [p1_load(i * GDN_CHUNKS_PER_ITER + j, s)
                 for j in range(GDN_CHUNKS_PER_ITER) for s in range(GDN_HEADS_PER_STEP)]
        for d in probs:
            p1_prep(d)
        for lvl in range(int(math.log2(C))):
            for d in probs:
                d["x"] = _split_dot(d["tinv"], jnp.where((rc_xor >> lvl) == 1, d["lmat"], 0.0), GDN_INV_PASSES)
            for d in probs:
                d["tinv"] = d["tinv"] - _split_dot(d["x"], d["tinv"], GDN_INV_PASSES)
        for d in probs:
            p1_finish(d)
        return carry

    lax.fori_loop(0, nc // GDN_CHUNKS_PER_ITER, p1_body, 0)

    oacc_sc[...] = jnp.zeros(oacc_sc.shape, F32)
    s_sc[...] = jnp.zeros(s_sc.shape, F32)
    lane_b = lax.broadcasted_iota(jnp.int32, (LANES, LANES), 1)
    zpad = jnp.zeros((C, LANES), BF16)

    def p2_body(i, carry):
        cf = i
        cb = jnp.where(i < ncc, ncc - 1 - i, nc - 1 + ncc - i)
        rf = pl.multiple_of(cf * C, C)
        rb = pl.multiple_of(cb * C, C)
        heads = range(GDN_HEADS_PER_STEP)
        st = [s_sc[s] for s in heads]
        lhs1 = [jnp.concatenate([
            jnp.concatenate([w_sc[s, cf, 0:C, :], zpad], axis=1),
            jnp.concatenate([zpad, w_sc[s, cb, C:C2, :]], axis=1),
            jnp.concatenate([qg_sc[s, cf, 0:C, :], zpad], axis=1),
            jnp.concatenate([zpad, qg_sc[s, cb, C:C2, :]], axis=1)], axis=0) for s in heads]
        zk = jnp.zeros((LANES, C2), BF16)
        lhs2 = [jnp.concatenate([a_sc[s, cf, 0:C, :], a_sc[s, cb, C:C2, :],
                                 jnp.where(lane_b < C, kt_sc[s, cf], zk),
                                 jnp.where(lane_b >= C, kt_sc[s, cb], zk)], axis=0) for s in heads]
        u = [jnp.concatenate([u_sc[s, cf, 0:C, :], u_sc[s, cb, C:C2, :]], axis=0) for s in heads]
        scale = [jnp.concatenate([jnp.broadcast_to(egl_sc[s, cf][0:1], (DN_DK, DN_DV)),
                                  jnp.broadcast_to(egl_sc[s, cb][4:5], (DN_DK, DN_DV))], axis=0) for s in heads]
        r1 = [jnp.dot(lhs1[s], st[s].astype(BF16), preferred_element_type=F32) for s in heads]
        vnew = [(u[s] - r1[s][0:C2]).astype(BF16) for s in heads]
        r2 = [jnp.dot(lhs2[s], vnew[s], preferred_element_type=F32) for s in heads]
        for s in heads:
            s_sc[s] = st[s] * scale[s] + r2[s][C2:C2 + 2 * DN_DK]
        for s in heads:
            oacc_sc[s, pl.ds(rf, C), :] += r1[s][C2:C2 + C] + r2[s][0:C]
            oacc_sc[s, pl.ds(rb, C), :] += r1[s][C2 + C:2 * C2] + r2[s][C:C2]
        return carry

    lax.fori_loop(0, nc, p2_body, 0)

    ng = ng_ref[...]
    rows = _pick(T, (256, 128))

    def fin_body(c, carry):
        r0 = pl.multiple_of(c * rows, rows)
        for s in range(GDN_HEADS_PER_STEP):
            o = oacc_sc[s, pl.ds(r0, rows), :]
            zf = z_ref[pl.ds(r0, rows), s * LANES:(s + 1) * LANES].astype(F32)
            ms = jnp.mean(o * o, axis=-1, keepdims=True)
            o_ref[pl.ds(r0, rows), s * LANES:(s + 1) * LANES] = (o * lax.rsqrt(ms + EPS) * ng * _silu(zf)).astype(o_ref.dtype)
        return carry

    lax.fori_loop(0, T // rows, fin_body, 0)


def _gdn(dnq, gates, rest, z_col0, norm_g, n_ctx):
    B, T, _ = dnq.shape
    H = DN_HEADS
    hps = GDN_HEADS_PER_STEP
    wblk = hps * LANES
    nc = T // DN_CHUNK
    assert nc % GDN_CHUNKS_PER_ITER == 0
    C2 = 2 * DN_CHUNK
    kern = functools.partial(_gdn_kernel, n_ctx=n_ctx)
    nb = H // hps
    once = dict(pipeline_mode=pl.Buffered(1))
    return pl.pallas_call(
        kern, grid=(B, nb),
        in_specs=[pl.BlockSpec((None, T, wblk), lambda b, h: (b, 0, h), **once),
                  pl.BlockSpec((None, T, wblk), lambda b, h: (b, 0, nb + h), **once),
                  pl.BlockSpec((None, T, wblk), lambda b, h: (b, 0, 2 * nb + h), **once),
                  pl.BlockSpec((None, T, LANES), lambda b, h: (b, 0, 0), **once),
                  pl.BlockSpec((None, T, wblk), lambda b, h: (b, 0, z_col0 // wblk + h), **once),
                  pl.BlockSpec((1, DN_DV), lambda b, h: (0, 0))],
        out_specs=pl.BlockSpec((None, T, wblk), lambda b, h: (b, 0, h)),
        out_shape=jax.ShapeDtypeStruct((B, T, H * DN_DV), BF16),
        scratch_shapes=[pltpu.VMEM((hps, nc, C2, LANES), BF16),
                        pltpu.VMEM((hps, nc, C2, LANES), BF16),
                        pltpu.VMEM((hps, nc, LANES, C2), BF16),
                        pltpu.VMEM((hps, nc, C2, C2), BF16),
                        pltpu.VMEM((hps, nc, C2, LANES), F32),
                        pltpu.VMEM((hps, nc, SUBLANES, LANES), F32),
                        pltpu.VMEM((hps, T, LANES), F32),
                        pltpu.VMEM((hps, 2 * DN_DK, DN_DV), F32)],
        compiler_params=_cparams("parallel", "arbitrary"), name="gated_deltanet",
    )(dnq, dnq, dnq, gates, rest, norm_g.reshape(1, DN_DV))


def _merge_kernel(da_ref, dn_ref, ga_ref, gb_ref, x_ref, wa_ref, wb_ref, wo_ref, g_ref, m_ref, xo_ref, h_ref):
    ya = jnp.dot(da_ref[...], wa_ref[...], preferred_element_type=F32)
    yb = jnp.dot(dn_ref[...], wb_ref[...], preferred_element_type=F32)
    y = jax.nn.sigmoid(ga_ref[...].astype(F32)) * ya + jax.nn.sigmoid(gb_ref[...].astype(F32)) * yb
    y2 = jnp.dot(y.astype(BF16), wo_ref[...], preferred_element_type=F32)
    m = m_ref[...]
    x = x_ref[...] + m[2:3] * y2
    xo_ref[...] = x
    h_ref[...] = _rms_mod(x, g_ref[...], m, 3, 4).astype(BF16)


def _merge(da, dn, rest, ga_col0, xs, w_ba, w_bb, w_o, g2, modsel, n_ctx):
    B, T, D = xs.shape
    tm = _pick(math.gcd(n_ctx, T - n_ctx), (256, 128))
    tok, gain, mod = _seg_specs(tm, n_ctx // tm, D)
    wspec = pl.BlockSpec((D, D), lambda b, t: (0, 0))
    ga_blk = ga_col0 // D
    return pl.pallas_call(
        _merge_kernel, grid=(B, T // tm),
        in_specs=[tok, tok,
                  pl.BlockSpec((None, tm, D), lambda b, t: (b, t, ga_blk)),
                  pl.BlockSpec((None, tm, D), lambda b, t: (b, t, ga_blk + 1)),
                  tok, wspec, wspec, wspec, gain, mod],
        out_specs=[tok, tok],
        out_shape=[jax.ShapeDtypeStruct((B, T, D), F32), jax.ShapeDtypeStruct((B, T, D), BF16)],
        compiler_params=_cparams("parallel", "parallel"), name="merge",
    )(da, dn, rest, rest, xs, w_ba, w_bb, w_o, g2.reshape(1, D), modsel)


def _desc_tops(x, n):
    tops = []
    for _ in range(n):
        m = jnp.max(x, axis=0, keepdims=True)
        tops.append(m)
        x = jnp.where(x == m, -jnp.inf, x)
    return tops


def _peer_prep_kernel(h_ref, wq_ref, keys_ref, a_ref, b_ref, kap_ref):
    tm = h_ref.shape[0]
    q = jnp.dot(h_ref[...], wq_ref[...], preferred_element_type=F32).astype(BF16)
    for h in range(PK_HEADS):
        st = [lax.dot_general(keys_ref[2 * h + p], q[:, (2 * h + p) * PK_HALF:(2 * h + p + 1) * PK_HALF],
                              NT_DIMS, preferred_element_type=F32) for p in range(2)]
        ta = _desc_tops(st[0], PK_TOPK + 1)
        tb = _desc_tops(st[1], PK_TOPK + 1)
        tbs = jnp.concatenate(tb[:PK_TOPK], axis=0)
        tb8 = tbs[0:SUBLANES]
        row8 = lax.broadcasted_iota(jnp.int32, tb8.shape, 0)
        pieces = [ta[0] + tbs, ta[1] + tb8]
        for r in range(2, SUBLANES):
            pieces.append(jnp.where(row8 < PK_TOPK // (r + 1), ta[r] + tb8, -jnp.inf))
        pieces.append(jnp.concatenate(ta[SUBLANES:PK_TOPK], axis=0) + tb[0])
        cand = jnp.concatenate(pieces, axis=0)
        best = _desc_tops(cand, PK_TOPK + 1)
        mx = best[0]
        zsum = best[0] * 0.0
        for r in range(PK_TOPK):
            zsum = zsum + jnp.exp(best[r] - mx)
        nxt = jnp.maximum(best[PK_TOPK], jnp.maximum(ta[PK_TOPK] + tb[0], ta[0] + tb[PK_TOPK]))
        thr = 0.5 * (best[PK_TOPK - 1] + nxt)
        rz = GELU_FOLD / zsum
        a_ref[h] = jnp.exp(st[0] - ta[0])
        b_ref[h] = (jnp.exp(st[1] - tb[0]) * rz).astype(b_ref.dtype)
        kap_ref[h:h + 1, :] = jnp.exp(thr - mx) * rz


def _peer_prep(h2, wq, keys):
    M, D = h2.shape
    tm = _pick(M, (256, 128))
    nk = 2 * PK_HEADS
    return pl.pallas_call(
        _peer_prep_kernel, grid=(M // tm,),
        in_specs=[pl.BlockSpec((tm, D), lambda i: (i, 0)),
                  pl.BlockSpec((D, nk * PK_HALF), lambda i: (0, 0)),
                  pl.BlockSpec((nk, N_KEYS, PK_HALF), lambda i: (0, 0, 0))],
        out_specs=[pl.BlockSpec((PK_HEADS, N_KEYS, tm), lambda i: (0, 0, i)),
                   pl.BlockSpec((PK_HEADS, N_KEYS, tm), lambda i: (0, 0, i)),
                   pl.BlockSpec((PK_HEADS, tm), lambda i: (0, i))],
        out_shape=[jax.ShapeDtypeStruct((PK_HEADS, N_KEYS, M), F32),
                   jax.ShapeDtypeStruct((PK_HEADS, N_KEYS, M), BF16),
                   jax.ShapeDtypeStruct((PK_HEADS, M), F32)],
        compiler_params=_cparams("parallel"), name="peer_prep",
    )(h2, wq, keys)


def _peer_dense_kernel(h_ref, u_ref, vt_ref, a_ref, b_ref, kap_ref, o_ref, acc_sc, *, sub):
    e = pl.program_id(1)
    tm = h_ref.shape[0]
    eb = u_ref.shape[0]
    ni = eb // N_KEYS

    @pl.when(e == 0)
    def _():
        acc_sc[...] = jnp.zeros(acc_sc.shape, F32)

    for t in range(tm // sub):
        tok = slice(t * sub, (t + 1) * sub)
        sc = lax.dot_general(u_ref[...], h_ref[tok, :], NT_DIMS, preferred_element_type=F32)

        def rows16(x):
            x16 = jnp.broadcast_to(x, (2 * SUBLANES, sub)).astype(BF16)
            return jnp.concatenate([x16] * (N_KEYS // (2 * SUBLANES)), axis=0)

        kaps = [rows16(kap_ref[h:h + 1, tok]) for h in range(PK_HEADS)]
        zero = jnp.zeros((N_KEYS, sub), BF16)
        was = []
        for il in range(ni):
            w = zero
            for h in range(PK_HEADS):
                prod = b_ref[h, :, tok] * rows16(a_ref[h, il:il + 1, tok])
                w = w + jnp.where(prod >= kaps[h], prod, zero)
            t_ = sc[il * N_KEYS:(il + 1) * N_KEYS, :]
            was.append(w * (t_ + t_ * lax.erf(t_)).astype(BF16))
        acc_sc[:, tok] += jnp.dot(vt_ref[...], jnp.concatenate(was, axis=0), preferred_element_type=F32)

    @pl.when(e == pl.num_programs(1) - 1)
    def _():
        o_ref[...] = acc_sc[...].T


def _peer_dense(h2, u, vt, a, b, kap):
    M, D = h2.shape
    E = u.shape[0]
    tm = _pick(M, (1024, 512, 256, 128))
    sub = _pick(tm, (256, 128))
    eb = SUBLANES * N_KEYS
    kern = functools.partial(_peer_dense_kernel, sub=sub)
    return pl.pallas_call(
        kern, grid=(M // tm, E // eb),
        in_specs=[pl.BlockSpec((tm, D), lambda i, e: (i, 0)),
                  pl.BlockSpec((eb, D), lambda i, e: (e, 0)),
                  pl.BlockSpec((D, eb), lambda i, e: (0, e)),
                  pl.BlockSpec((PK_HEADS, SUBLANES, tm), lambda i, e: (0, e, i)),
                  pl.BlockSpec((PK_HEADS, N_KEYS, tm), lambda i, e: (0, 0, i)),
                  pl.BlockSpec((PK_HEADS, tm), lambda i, e: (0, i))],
        out_specs=pl.BlockSpec((tm, D), lambda i, e: (i, 0)),
        out_shape=jax.ShapeDtypeStruct((M, D), F32),
        scratch_shapes=[pltpu.VMEM((D, tm), F32)],
        compiler_params=_cparams("parallel", "arbitrary"), name="peer_dense",
    )(h2, u, vt, a, b, kap)


def _rope_tables(n_ctx, S):
    n_freq = DA_DIM // 4
    n_rows = S // GRID_W
    row = jnp.repeat(jnp.arange(n_rows, dtype=F32), GRID_W)
    col = jnp.tile(jnp.arange(GRID_W, dtype=F32), n_rows)
    inv = ROPE_BASE ** (-jnp.arange(n_freq, dtype=F32) / n_freq)
    ang = jnp.concatenate([row[:, None] * inv, col[:, None] * inv], axis=-1)
    cos = jnp.concatenate([jnp.ones((n_ctx, DA_DIM // 2), F32), jnp.cos(ang)], axis=0)
    sin = jnp.concatenate([jnp.zeros((n_ctx, DA_DIM // 2), F32), jnp.sin(ang)], axis=0)
    return jnp.concatenate([cos] * 4, axis=1), jnp.concatenate([-sin, sin, -sin, sin], axis=1)


def _deinterleave_perm():
    idx = np.arange(DA_HEADS * 2 * DA_DIM).reshape(DA_HEADS * 2, DA_DIM // 2, 2)
    return np.concatenate([idx[:, :, 0], idx[:, :, 1]], axis=1).reshape(-1)


def kernel(x, c, ctx, c_ctx, w_ada, b_ada, norm1_g, norm2_g, w_in, da_lambda, da_subln, dn_conv, dn_a_log,
           dn_dt_bias, dn_norm, w_branch_a, w_branch_b, w_out, peer_wq, peer_keys, peer_u, peer_v, final_g):
    B, S, D = x.shape
    n_ctx = ctx.shape[1]
    T = n_ctx + S
    L = w_ada.shape[0]
    M = B * T
    assert B < SUBLANES and S % GRID_W == 0 and n_ctx % DN_CHUNK == 0 and S % DN_CHUNK == 0
    da_qk = DA_HEADS * 2 * DA_DIM
    da_w = DA_HEADS * DA_VDIM
    dn_qkv = 3 * DN_HEADS * DN_DK
    dn_w = DN_HEADS * DN_DV
    nh2 = 2 * DN_HEADS

    c_all = jnp.zeros((SUBLANES, D), F32).at[:B].set(c).at[B].set(c_ctx)
    mod = _ada(c_all, w_ada, b_ada).reshape(L, SUBLANES, 6, D)
    modsel = jnp.stack([jnp.broadcast_to(mod[:, B:B + 1], (L, B, 6, D)), mod[:, :B]], axis=2)

    cos_t, sin_t = _rope_tables(n_ctx, S)
    cos_m = jnp.tile(cos_t, (B, 1))
    sin_m = jnp.tile(sin_t, (B, 1))
    perm = _deinterleave_perm()

    xs = jnp.concatenate([ctx, x], axis=1)
    f = None
    for l in range(L):
        lam_init = 0.8 - 0.6 * math.exp(-0.3 * l)
        w = w_in[l]
        o = 0
        w_q = w[:, o:o + da_qk][:, perm]; o += da_qk
        w_k = w[:, o:o + da_qk][:, perm]; o += da_qk
        w_v = w[:, o:o + da_w]; o += da_w
        w_dn = w[:, o:o + dn_qkv]; o += dn_qkv
        w_z = w[:, o:o + dn_w]; o += dn_w
        w_b = w[:, o:o + nh2]; o += nh2
        w_a = w[:, o:o + nh2]; o += nh2
        w_ga = w[:, o:o + D]; o += D
        w_gb = w[:, o:o + D]; o += D
        w_qk = jnp.concatenate([w_q, w_k], axis=1).astype(BF16)
        w_rest = jnp.concatenate([w_dn, w_z, w_ga, w_gb], axis=1).astype(BF16)
        dn_col0, z_col0, ga_col0 = 0, dn_qkv, dn_qkv + dn_w
        w_ba_pad = jnp.zeros((D, LANES), F32).at[:, :nh2].set(w_b).at[:, nh2:2 * nh2].set(w_a).astype(BF16)
        gate_par = (jnp.zeros((SUBLANES, LANES), F32)
                    .at[0, nh2:2 * nh2].set(dn_a_log[l].reshape(-1))
                    .at[1, nh2:2 * nh2].set(dn_dt_bias[l].reshape(-1)))

        if l == 0:
            h = _modulate(xs, norm1_g[l], modsel[l], n_ctx)
        else:
            xs, h = _resid_modulate(xs, f, modsel[l - 1], norm1_g[l], modsel[l], n_ctx)
        hm = h.reshape(M, D)

        qk = _proj(hm, w_qk, out_dtype=BF16, kernel=_proj_rope_kernel, extra=(cos_m, sin_m),
                   extra_specs=lambda tm: [pl.BlockSpec((tm, LANES), lambda j, i: (i, 0))] * 2,
                   name="proj_qk_rope").reshape(B, T, 2 * da_qk)
        rest = _proj(hm, w_rest, out_dtype=BF16, extra_specs=lambda tm: [], name="proj_rest").reshape(B, T, -1)
        gates = _proj(hm, w_ba_pad, out_dtype=F32, kernel=_proj_gates_kernel, extra=(gate_par,),
                      extra_specs=lambda tm: [pl.BlockSpec((SUBLANES, LANES), lambda j, i: (0, 0))],
                      name="proj_gates").reshape(B, T, LANES)

        vt = _proj_t(hm, w_v.T.astype(BF16), out_dtype=BF16, name="proj_v_t")

        da = _attention(qk, vt, da_lambda[l], da_subln[l], n_ctx, lam_init)
        dnq = _dn_prep(rest, dn_conv[l], n_ctx, dn_col0)
        dn = _gdn(dnq, gates, rest, z_col0, dn_norm[l], n_ctx)

        xs, h2 = _merge(da, dn, rest, ga_col0, xs, w_branch_a[l].astype(BF16), w_branch_b[l].astype(BF16),
                        w_out[l].astype(BF16), norm2_g[l], modsel[l], n_ctx)
        h2m = h2.reshape(M, D)
        pa, pb, kap = _peer_prep(h2m, peer_wq[l].astype(BF16),
                                 peer_keys[l].reshape(2 * PK_HEADS, N_KEYS, PK_HALF).astype(BF16))
        f = _peer_dense(h2m, (peer_u[l] * GELU_FOLD).astype(BF16), peer_v[l].T.astype(BF16),
                        pa, pb, kap).reshape(B, T, D)

    return _final(xs, f, modsel[L - 1], final_g, n_ctx)
```

```python
import functools
import math

import jax
import jax.numpy as jnp
import numpy as np
from jax import lax
from jax.experimental import pallas as pl
from jax.experimental.pallas import tpu as pltpu

F32 = jnp.float32
BF16 = jnp.bfloat16
HIGHEST = lax.Precision.HIGHEST

GRID_W = 64
EPS = 1e-6
ROPE_BASE = 10000.0
DA_HEADS = 8
DA_DIM = 64
DA_VDIM = 2 * DA_DIM
DN_HEADS = 8
DN_DK = 128
DN_DV = 128
DN_CONV = 5
DN_CHUNK = 64
PK_HEADS = 8
N_KEYS = 128
PK_TOPK = 16
PK_HALF = 128
GELU_FOLD = 2.0 ** -0.5

LANES = 128
SUBLANES = 8
VMEM_LIMIT_BYTES = 56 * 1024 * 1024

NT_DIMS = (((1,), (1,)), ((), ()))


def _cparams(*sem, flags=None):
    return pltpu.CompilerParams(dimension_semantics=sem, vmem_limit_bytes=VMEM_LIMIT_BYTES, flags=flags)


def _pick(n, cands):
    for c in cands:
        if n % c == 0:
            return c
    raise ValueError(f"no tile in {cands} divides {n}")


def _silu(x):
    return x * jax.nn.sigmoid(x)


def _ada_kernel(c_ref, w_ref, b_ref, o_ref):
    sc = _silu(c_ref[...])
    o_ref[0] = jnp.dot(sc, w_ref[0], preferred_element_type=F32, precision=HIGHEST) + b_ref[0]


def _ada(c_all, w_ada, b_ada):
    L, D, N = w_ada.shape
    tn = _pick(N, (1536, 1024, 512, 256, 128))
    return pl.pallas_call(
        _ada_kernel,
        grid=(L, N // tn),
        in_specs=[pl.BlockSpec((SUBLANES, D), lambda l, j: (0, 0)),
                  pl.BlockSpec((1, D, tn), lambda l, j: (l, 0, j)),
                  pl.BlockSpec((1, 1, tn), lambda l, j: (l, 0, j))],
        out_specs=pl.BlockSpec((1, SUBLANES, tn), lambda l, j: (l, 0, j)),
        out_shape=jax.ShapeDtypeStruct((L, SUBLANES, N), F32),
        compiler_params=_cparams("parallel", "parallel"),
        name="ada",
    )(c_all, w_ada, b_ada.reshape(L, 1, N))


def _rms_mod(x, g, m, shift_row, scale_row):
    ms = jnp.mean(x * x, axis=-1, keepdims=True)
    y = x * lax.rsqrt(ms + EPS) * g
    return y * (1.0 + m[scale_row:scale_row + 1]) + m[shift_row:shift_row + 1]


def _modulate_kernel(x_ref, g_ref, m_ref, h_ref):
    h_ref[...] = _rms_mod(x_ref[...], g_ref[...], m_ref[...], 0, 1).astype(BF16)


def _resid_modulate_kernel(x_ref, f_ref, mp_ref, g_ref, m_ref, xo_ref, h_ref):
    x = x_ref[...] + mp_ref[...][5:6] * f_ref[...]
    xo_ref[...] = x
    h_ref[...] = _rms_mod(x, g_ref[...], m_ref[...], 0, 1).astype(BF16)


def _seg_specs(tm, n_ctx_tiles, D):
    tok = pl.BlockSpec((None, tm, D), lambda b, t: (b, t, 0))
    gain = pl.BlockSpec((1, D), lambda b, t: (0, 0))
    mod = pl.BlockSpec((None, None, 6, D), lambda b, t: (b, jnp.where(t >= n_ctx_tiles, 1, 0), 0, 0))
    return tok, gain, mod


def _modulate(xs, g, modsel, n_ctx):
    B, T, D = xs.shape
    tm = _pick(math.gcd(n_ctx, T - n_ctx), (256, 128))
    tok, gain, mod = _seg_specs(tm, n_ctx // tm, D)
    return pl.pallas_call(
        _modulate_kernel, grid=(B, T // tm),
        in_specs=[tok, gain, mod], out_specs=tok,
        out_shape=jax.ShapeDtypeStruct((B, T, D), BF16),
        compiler_params=_cparams("parallel", "parallel"), name="modulate",
    )(xs, g.reshape(1, D), modsel)


def _resid_modulate(xs, f, modsel_prev, g, modsel, n_ctx):
    B, T, D = xs.shape
    tm = _pick(math.gcd(n_ctx, T - n_ctx), (256, 128))
    tok, gain, mod = _seg_specs(tm, n_ctx // tm, D)
    return pl.pallas_call(
        _resid_modulate_kernel, grid=(B, T // tm),
        in_specs=[tok, tok, mod, gain, mod], out_specs=[tok, tok],
        out_shape=[jax.ShapeDtypeStruct((B, T, D), F32), jax.ShapeDtypeStruct((B, T, D), BF16)],
        compiler_params=_cparams("parallel", "parallel"), name="resid_modulate",
    )(xs, f, modsel_prev, g.reshape(1, D), modsel)


def _final_kernel(x_ref, f_ref, mp_ref, g_ref, o_ref):
    x = x_ref[...] + mp_ref[...][5:6] * f_ref[...]
    ms = jnp.mean(x * x, axis=-1, keepdims=True)
    o_ref[...] = x * lax.rsqrt(ms + EPS) * g_ref[...]


def _final(xs, f, modsel_prev, g, n_ctx):
    B, T, D = xs.shape
    S = T - n_ctx
    tm = _pick(math.gcd(n_ctx, S), (256, 128))
    off = n_ctx // tm
    tok_in = pl.BlockSpec((None, tm, D), lambda b, t: (b, t + off, 0))
    return pl.pallas_call(
        _final_kernel, grid=(B, S // tm),
        in_specs=[tok_in, tok_in,
                  pl.BlockSpec((None, None, 6, D), lambda b, t: (b, 1, 0, 0)),
                  pl.BlockSpec((1, D), lambda b, t: (0, 0))],
        out_specs=pl.BlockSpec((None, tm, D), lambda b, t: (b, t, 0)),
        out_shape=jax.ShapeDtypeStruct((B, S, D), F32),
        compiler_params=_cparams("parallel", "parallel"), name="final_norm",
    )(xs, f, modsel_prev, g.reshape(1, D))


def _proj_kernel(h_ref, w_ref, o_ref):
    o_ref[...] = jnp.dot(h_ref[...], w_ref[...], preferred_element_type=F32).astype(o_ref.dtype)


def _proj_rope_kernel(h_ref, w_ref, cos_ref, sin_ref, o_ref):
    acc = jnp.dot(h_ref[...], w_ref[...], preferred_element_type=F32)
    tn = acc.shape[1]
    half = DA_DIM // 2
    lane = lax.broadcasted_iota(jnp.int32, acc.shape, 1)
    partner = jnp.where((lane & (DA_DIM - 1)) < half,
                        pltpu.roll(acc, tn - half, axis=1),
                        pltpu.roll(acc, half, axis=1))
    reps = tn // LANES
    cos = jnp.concatenate([cos_ref[...]] * reps, axis=1)
    sin = jnp.concatenate([sin_ref[...]] * reps, axis=1)
    y = acc * cos + partner * sin
    scale = jnp.where(pl.program_id(0) == 0, DA_DIM ** -0.5 * math.log2(math.e), 1.0).astype(F32)
    o_ref[...] = (y * scale).astype(o_ref.dtype)


def _proj_gates_kernel(h_ref, w_ref, p_ref, o_ref):
    acc = jnp.dot(h_ref[...], w_ref[...], preferred_element_type=F32)
    p = p_ref[...]
    lane = lax.broadcasted_iota(jnp.int32, acc.shape, 1)
    beta = jax.nn.sigmoid(acc)
    g = -jnp.exp(p[0:1]) * jax.nn.softplus(acc + p[1:2])
    o_ref[...] = jnp.where(lane < 2 * DN_HEADS, beta, jnp.where(lane < 4 * DN_HEADS, g, 0.0))


def _proj(h, w, *, out_dtype, kernel=_proj_kernel, extra=(), extra_specs=(), name="proj"):
    M, D = h.shape
    N = w.shape[1]
    tm = _pick(M, (512, 256, 128))
    tn = _pick(N, (1024, 512, 256, 128))
    return pl.pallas_call(
        kernel, grid=(N // tn, M // tm),
        in_specs=[pl.BlockSpec((tm, D), lambda j, i: (i, 0)),
                  pl.BlockSpec((D, tn), lambda j, i: (0, j))] + list(extra_specs(tm)),
        out_specs=pl.BlockSpec((tm, tn), lambda j, i: (i, j)),
        out_shape=jax.ShapeDtypeStruct((M, N), out_dtype),
        compiler_params=_cparams("parallel", "parallel"), name=name,
    )(h, w, *extra)


def _proj_t_kernel(h_ref, wt_ref, o_ref):
    o_ref[...] = lax.dot_general(wt_ref[...], h_ref[...], NT_DIMS, preferred_element_type=F32).astype(o_ref.dtype)


def _proj_t(h, wt, *, out_dtype, name):
    M, D = h.shape
    N = wt.shape[0]
    tm = _pick(M, (512, 256, 128))
    tn = _pick(N, (1024, 512, 256, 128))
    return pl.pallas_call(
        _proj_t_kernel, grid=(N // tn, M // tm),
        in_specs=[pl.BlockSpec((tm, D), lambda j, i: (i, 0)),
                  pl.BlockSpec((tn, D), lambda j, i: (j, 0))],
        out_specs=pl.BlockSpec((tn, tm), lambda j, i: (j, i)),
        out_shape=jax.ShapeDtypeStruct((N, M), out_dtype),
        compiler_params=_cparams("parallel", "parallel"), name=name,
    )(h, wt)


def _attn_kernel(lam_ref, q_ref, k_ref, vt_ref, g_ref, o_ref, *, n_ctx, tk, lam_init):
    tq = q_ref.shape[0]
    T = k_ref.shape[0]
    qi = pl.program_id(2)
    q = q_ref[...]
    lane = lax.broadcasted_iota(jnp.int32, q.shape, 1)
    zero = jnp.zeros_like(q)
    qq = jnp.concatenate([jnp.where(lane < DA_DIM, q, zero), jnp.where(lane >= DA_DIM, q, zero)], axis=0)

    def scores(r0, rows):
        return lax.dot_general(k_ref[r0:r0 + rows, :], qq, NT_DIMS, preferred_element_type=F32)

    def step(state, s, r0, rows):
        m_prev, l_prev, acc = state
        m_new = jnp.maximum(m_prev, jnp.max(s, axis=0, keepdims=True))
        alpha = jnp.exp2(m_prev - m_new)
        p = jnp.exp2(s - m_new)
        pv = jnp.dot(vt_ref[:, r0:r0 + rows], p.astype(BF16), preferred_element_type=F32)
        return m_new, alpha * l_prev + jnp.sum(p, axis=0, keepdims=True), alpha * acc + pv

    lam = lam_ref[...]
    lam_val = (jnp.exp(jnp.sum(lam[0:1] * lam[1:2], axis=-1, keepdims=True))
               - jnp.exp(jnp.sum(lam[2:3] * lam[3:4], axis=-1, keepdims=True)) + lam_init)

    def finish(state):
        _, l_fin, acc = state
        o_all = acc / l_fin
        o = o_all[:, 0:tq] - lam_val * o_all[:, tq:2 * tq]
        ms = jnp.mean(o * o, axis=0, keepdims=True)
        y = o * lax.rsqrt(ms + EPS) * (1.0 - lam_init)
        o_ref[...] = (y.T * g_ref[...]).astype(o_ref.dtype)

    init = (jnp.full((1, 2 * tq), -jnp.inf, F32), jnp.zeros((1, 2 * tq), F32), jnp.zeros((DA_VDIM, 2 * tq), F32))
    def attend(blocks):
        state = init
        s_next = scores(*blocks[0])
        for j, (r0, rows) in enumerate(blocks):
            s_cur = s_next
            if j + 1 < len(blocks):
                s_next = scores(*blocks[j + 1])
            state = step(state, s_cur, r0, rows)
        finish(state)

    @pl.when(qi * tq < n_ctx)
    def _():
        attend([(0, n_ctx)])

    @pl.when(qi * tq >= n_ctx)
    def _():
        attend([(0, n_ctx)] + [(n_ctx + j * tk, tk) for j in range((T - n_ctx) // tk)])


def _attention(qk, vt, lam, subln, n_ctx, lam_init):
    B, T, _ = qk.shape
    H = DA_HEADS
    S = T - n_ctx
    tq = _pick(math.gcd(n_ctx, S), (256, 128))
    tk = _pick(S, (2048, 1024, 512, 256, 128))
    kern = functools.partial(_attn_kernel, n_ctx=n_ctx, tk=tk, lam_init=lam_init)
    return pl.pallas_call(
        kern, grid=(B, H, T // tq),
        in_specs=[pl.BlockSpec((4, DA_DIM), lambda b, h, i: (0, 0)),
                  pl.BlockSpec((None, tq, DA_VDIM), lambda b, h, i: (b, i, h)),
                  pl.BlockSpec((None, T, DA_VDIM), lambda b, h, i: (b, 0, H + h)),
                  pl.BlockSpec((DA_VDIM, T), lambda b, h, i: (h, b)),
                  pl.BlockSpec((1, DA_VDIM), lambda b, h, i: (0, 0))],
        out_specs=pl.BlockSpec((None, tq, DA_VDIM), lambda b, h, i: (b, i, h)),
        out_shape=jax.ShapeDtypeStruct((B, T, H * DA_VDIM), BF16),
        compiler_params=_cparams("parallel", "parallel", "arbitrary"), name="diff_attention",
    )(lam, qk, qk, vt, subln.reshape(1, DA_VDIM))


def _dn_prep_kernel(x_ref, w_ref, o_ref, pad_sc, *, n_ctx, rows):
    T = x_ref.shape[0]
    halo = SUBLANES
    cb = pl.program_id(1)
    zpad = jnp.zeros((halo, LANES), F32)
    pad_sc[0:halo, :] = zpad
    pad_sc[halo + n_ctx:2 * halo + n_ctx, :] = zpad
    pad_sc[2 * halo + T:3 * halo + T, :] = zpad
    pad_sc[halo:halo + n_ctx, :] = x_ref[0:n_ctx, :].astype(F32)
    pad_sc[2 * halo + n_ctx:2 * halo + T, :] = x_ref[n_ctx:T, :].astype(F32)
    w = w_ref[...]
    is_qk = cb < 2 * DN_HEADS
    post = jnp.where(cb < DN_HEADS, DN_DK ** -0.5, 1.0).astype(F32)
    pad = DN_CONV // 2

    def body(c, carry):
        r0 = pl.multiple_of(c * rows, rows)
        base = pl.multiple_of(r0 + jnp.where(r0 >= n_ctx, 2 * halo, halo), halo)
        y = jnp.zeros((rows, LANES), F32)
        for j in range(DN_CONV):
            y = y + pad_sc[pl.ds(base + (j - pad), rows), :] * w[j:j + 1]
        y = _silu(y)
        yn = y * lax.rsqrt(jnp.sum(y * y, axis=-1, keepdims=True) + EPS) * post
        o_ref[pl.ds(r0, rows), :] = jnp.where(is_qk, yn, y).astype(o_ref.dtype)
        return carry

    lax.fori_loop(0, T // rows, body, 0)


def _dn_prep(rest, conv_w, n_ctx, col0):
    B, T, _ = rest.shape
    nblk = 3 * DN_HEADS
    rows = _pick(math.gcd(n_ctx, T - n_ctx), (256, 128))
    cw = jnp.zeros((SUBLANES, nblk * LANES), F32).at[:DN_CONV].set(conv_w)
    kern = functools.partial(_dn_prep_kernel, n_ctx=n_ctx, rows=rows)
    return pl.pallas_call(
        kern, grid=(B, nblk),
        in_specs=[pl.BlockSpec((None, T, LANES), lambda b, c: (b, 0, col0 // LANES + c)),
                  pl.BlockSpec((SUBLANES, LANES), lambda b, c: (0, c))],
        out_specs=pl.BlockSpec((None, T, LANES), lambda b, c: (b, 0, c)),
        out_shape=jax.ShapeDtypeStruct((B, T, nblk * LANES), BF16),
        scratch_shapes=[pltpu.VMEM((T + 3 * SUBLANES, LANES), F32)],
        compiler_params=_cparams("parallel", "parallel"), name="dn_prep",
    )(rest, cw)


GDN_HEADS_PER_STEP = 2
GDN_CHUNKS_PER_ITER = 4
GDN_INV_PASSES = 1


def _split_dot(a, b, passes):
    a_hi = a.astype(BF16)
    b_hi = b.astype(BF16)
    out = jnp.dot(a_hi, b_hi, preferred_element_type=F32)
    if passes >= 3:
        a_lo = (a - a_hi.astype(F32)).astype(BF16)
        b_lo = (b - b_hi.astype(F32)).astype(BF16)
        out = out + jnp.dot(a_hi, b_lo, preferred_element_type=F32) + jnp.dot(a_lo, b_hi, preferred_element_type=F32)
    return out


def _gdn_kernel(q_ref, k_ref, v_ref, gt_ref, z_ref, ng_ref, o_ref,
                w_sc, qg_sc, kt_sc, a_sc, u_sc, egl_sc, oacc_sc, s_sc, *, n_ctx):
    C = DN_CHUNK
    C2 = 2 * C
    T = q_ref.shape[0]
    nc = T // C
    ncc = n_ctx // C
    hp = pl.program_id(1)

    row = lax.broadcasted_iota(jnp.int32, (C2, C2), 0)
    col = lax.broadcasted_iota(jnp.int32, (C2, C2), 1)
    fwd_row = row < C
    rc_xor = row ^ col
    ahead = (col - row) * jnp.where(fwd_row, 1, -1)
    same_dir = rc_xor < C
    incl = same_dir & (ahead <= 0)
    strict = same_dir & (ahead < 0)
    eye = (row == col).astype(F32)
    mcs = jnp.where(incl, 1.0, 0.0).astype(BF16)

    def p1_load(c, s):
        r0 = pl.multiple_of(c * C, C)
        head = hp * GDN_HEADS_PER_STEP + s
        lo, hi = s * LANES, (s + 1) * LANES
        k = k_ref[pl.ds(r0, C), lo:hi]
        q = q_ref[pl.ds(r0, C), lo:hi]
        v = v_ref[pl.ds(r0, C), lo:hi]
        kk = jnp.concatenate([k, k], axis=0)
        qq = jnp.concatenate([q, q], axis=0)
        vf = jnp.concatenate([v, v], axis=0).astype(F32)
        x = gt_ref[pl.ds(r0, C), :]
        x2 = jnp.concatenate([x, x], axis=0)
        bsel = jnp.where(fwd_row, head, DN_HEADS + head)
        gsel = bsel + 2 * DN_HEADS
        beta = jnp.sum(jnp.where(col == bsel, x2, 0.0), axis=-1, keepdims=True)
        glog = jnp.sum(jnp.where(col == gsel, x2, 0.0), axis=-1, keepdims=True)
        return dict(c=c, s=s, kk=kk, qq=qq, vf=vf, beta=beta, glog=glog)

    def p1_prep(d):
        g_rem = jnp.broadcast_to(d["glog"], (C2, C2))
        gcum = jnp.zeros((C2, C2), F32)
        for _ in range(3):
            piece = g_rem.astype(BF16)
            gcum = gcum + jnp.dot(mcs, piece, preferred_element_type=F32)
            g_rem = g_rem - piece.astype(F32)
        decay = jnp.exp(jnp.where(incl, gcum - gcum.T, -jnp.inf))
        kkt = lax.dot_general(d["kk"], d["kk"], NT_DIMS, preferred_element_type=F32)
        qkt = lax.dot_general(d["qq"], d["kk"], NT_DIMS, preferred_element_type=F32)
        d.update(gcum=gcum, decay=decay, qkt=qkt, lmat=jnp.where(strict, d["beta"] * kkt * decay, 0.0), tinv=eye)

    def p1_finish(d):
        c, s, gcum, beta = d["c"], d["s"], d["gcum"], d["beta"]
        kf = d["kk"].astype(F32)
        eg = jnp.exp(gcum)
        rhs = jnp.concatenate([(d["vf"] * beta).astype(BF16), (kf * beta * eg).astype(BF16)], axis=1)
        uw = jnp.dot(d["tinv"].astype(BF16), rhs, preferred_element_type=F32)
        glast = jnp.where(fwd_row, gcum[C - 1:C, :], gcum[C:C + 1, :])
        ktail = kf * jnp.exp(glast - gcum)
        u_sc[s, c] = uw[:, 0:LANES]
        w_sc[s, c] = uw[:, LANES:2 * LANES].astype(BF16)
        qg_sc[s, c] = (d["qq"].astype(F32) * eg).astype(BF16)
        a_sc[s, c] = (d["qkt"] * d["decay"]).astype(BF16)
        kt_sc[s, c] = ktail.T.astype(BF16)
        egl_sc[s, c] = jnp.exp(jnp.concatenate([jnp.broadcast_to(gcum[C - 1:C, :], (4, LANES)),
                                                jnp.broadcast_to(gcum[C:C + 1, :], (4, LANES))], axis=0))

    def p1_body(i, carry):
        probs = [p1_load(i * GDN_CHUNKS_PER_ITER + j, s)
                 for j in range(GDN_CHUNKS_PER_ITER) for s in range(GDN_HEADS_PER_STEP)]
        for d in probs:
            p1_prep(d)
        for lvl in range(int(math.log2(C))):
            for d in probs:
                d["x"] = _split_dot(d["tinv"], jnp.where((rc_xor >> lvl) == 1, d["lmat"], 0.0), GDN_INV_PASSES)
            for d in probs:
                d["tinv"] = d["tinv"] - _split_dot(d["x"], d["tinv"], GDN_INV_PASSES)
        for d in probs:
            p1_finish(d)
        return carry

    lax.fori_loop(0, nc // GDN_CHUNKS_PER_ITER, p1_body, 0)

    oacc_sc[...] = jnp.zeros(oacc_sc.shape, F32)
    s_sc[...] = jnp.zeros(s_sc.shape, F32)
    lane_b = lax.broadcasted_iota(jnp.int32, (LANES, LANES), 1)
    zpad = jnp.zeros((C, LANES), BF16)

    def p2_body(i, carry):
        cf = i
        cb = jnp.where(i < ncc, ncc - 1 - i, nc - 1 + ncc - i)
        rf = pl.multiple_of(cf * C, C)
        rb = pl.multiple_of(cb * C, C)
        heads = range(GDN_HEADS_PER_STEP)
        st = [s_sc[s] for s in heads]
        lhs1 = [jnp.concatenate([
            jnp.concatenate([w_sc[s, cf, 0:C, :], zpad], axis=1),
            jnp.concatenate([zpad, w_sc[s, cb, C:C2, :]], axis=1),
            jnp.concatenate([qg_sc[s, cf, 0:C, :], zpad], axis=1),
            jnp.concatenate([zpad, qg_sc[s, cb, C:C2, :]], axis=1)], axis=0) for s in heads]
        zk = jnp.zeros((LANES, C2), BF16)
        lhs2 = [jnp.concatenate([a_sc[s, cf, 0:C, :], a_sc[s, cb, C:C2, :],
                                 jnp.where(lane_b < C, kt_sc[s, cf], zk),
                                 jnp.where(lane_b >= C, kt_sc[s, cb], zk)], axis=0) for s in heads]
        u = [jnp.concatenate([u_sc[s, cf, 0:C, :], u_sc[s, cb, C:C2, :]], axis=0) for s in heads]
        scale = [jnp.concatenate([jnp.broadcast_to(egl_sc[s, cf][0:1], (DN_DK, DN_DV)),
                                  jnp.broadcast_to(egl_sc[s, cb][4:5], (DN_DK, DN_DV))], axis=0) for s in heads]
        r1 = [jnp.dot(lhs1[s], st[s].astype(BF16), preferred_element_type=F32) for s in heads]
        vnew = [(u[s] - r1[s][0:C2]).astype(BF16) for s in heads]
        r2 = [jnp.dot(lhs2[s], vnew[s], preferred_element_type=F32) for s in heads]
        for s in heads:
            s_sc[s] = st[s] * scale[s] + r2[s][C2:C2 + 2 * DN_DK]
        for s in heads:
            oacc_sc[s, pl.ds(rf, C), :] += r1[s][C2:C2 + C] + r2[s][0:C]
            oacc_sc[s, pl.ds(rb, C), :] += r1[s][C2 + C:2 * C2] + r2[s][C:C2]
        return carry

    lax.fori_loop(0, nc, p2_body, 0)

    ng = ng_ref[...]
    rows = _pick(T, (256, 128))

    def fin_body(c, carry):
        r0 = pl.multiple_of(c * rows, rows)
        for s in range(GDN_HEADS_PER_STEP):
            o = oacc_sc[s, pl.ds(r0, rows), :]
            zf = z_ref[pl.ds(r0, rows), s * LANES:(s + 1) * LANES].astype(F32)
            ms = jnp.mean(o * o, axis=-1, keepdims=True)
            o_ref[pl.ds(r0, rows), s * LANES:(s + 1) * LANES] = (o * lax.rsqrt(ms + EPS) * ng * _silu(zf)).astype(o_ref.dtype)
        return carry

    lax.fori_loop(0, T // rows, fin_body, 0)


def _gdn(dnq, gates, rest, z_col0, norm_g, n_ctx):
    B, T, _ = dnq.shape
    H = DN_HEADS
    hps = GDN_HEADS_PER_STEP
    wblk = hps * LANES
    nc = T // DN_CHUNK
    assert nc % GDN_CHUNKS_PER_ITER == 0
    C2 = 2 * DN_CHUNK
    kern = functools.partial(_gdn_kernel, n_ctx=n_ctx)
    nb = H // hps
    once = dict(pipeline_mode=pl.Buffered(1))
    return pl.pallas_call(
        kern, grid=(B, nb),
        in_specs=[pl.BlockSpec((None, T, wblk), lambda b, h: (b, 0, h), **once),
                  pl.BlockSpec((None, T, wblk), lambda b, h: (b, 0, nb + h), **once),
                  pl.BlockSpec((None, T, wblk), lambda b, h: (b, 0, 2 * nb + h), **once),
                  pl.BlockSpec((None, T, LANES), lambda b, h: (b, 0, 0), **once),
                  pl.BlockSpec((None, T, wblk), lambda b, h: (b, 0, z_col0 // wblk + h), **once),
                  pl.BlockSpec((1, DN_DV), lambda b, h: (0, 0))],
        out_specs=pl.BlockSpec((None, T, wblk), lambda b, h: (b, 0, h)),
        out_shape=jax.ShapeDtypeStruct((B, T, H * DN_DV), BF16),
        scratch_shapes=[pltpu.VMEM((hps, nc, C2, LANES), BF16),
                        pltpu.VMEM((hps, nc, C2, LANES), BF16),
                        pltpu.VMEM((hps, nc, LANES, C2), BF16),
                        pltpu.VMEM((hps, nc, C2, C2), BF16),
                        pltpu.VMEM((hps, nc, C2, LANES), F32),
                        pltpu.VMEM((hps, nc, SUBLANES, LANES), F32),
                        pltpu.VMEM((hps, T, LANES), F32),
                        pltpu.VMEM((hps, 2 * DN_DK, DN_DV), F32)],
        compiler_params=_cparams("parallel", "arbitrary"), name="gated_deltanet",
    )(dnq, dnq, dnq, gates, rest, norm_g.reshape(1, DN_DV))


def _merge_kernel(da_ref, dn_ref, ga_ref, gb_ref, x_ref, wa_ref, wb_ref, wo_ref, g_ref, m_ref, xo_ref, h_ref):
    ya = jnp.dot(da_ref[...], wa_ref[...], preferred_element_type=F32)
    yb = jnp.dot(dn_ref[...], wb_ref[...], preferred_element_type=F32)
    y = jax.nn.sigmoid(ga_ref[...].astype(F32)) * ya + jax.nn.sigmoid(gb_ref[...].astype(F32)) * yb
    y2 = jnp.dot(y.astype(BF16), wo_ref[...], preferred_element_type=F32)
    m = m_ref[...]
    x = x_ref[...] + m[2:3] * y2
    xo_ref[...] = x
    h_ref[...] = _rms_mod(x, g_ref[...], m, 3, 4).astype(BF16)


def _merge(da, dn, rest, ga_col0, xs, w_ba, w_bb, w_o, g2, modsel, n_ctx):
    B, T, D = xs.shape
    tm = _pick(math.gcd(n_ctx, T - n_ctx), (256, 128))
    tok, gain, mod = _seg_specs(tm, n_ctx // tm, D)
    wspec = pl.BlockSpec((D, D), lambda b, t: (0, 0))
    ga_blk = ga_col0 // D
    return pl.pallas_call(
        _merge_kernel, grid=(B, T // tm),
        in_specs=[tok, tok,
                  pl.BlockSpec((None, tm, D), lambda b, t: (b, t, ga_blk)),
                  pl.BlockSpec((None, tm, D), lambda b, t: (b, t, ga_blk + 1)),
                  tok, wspec, wspec, wspec, gain, mod],
        out_specs=[tok, tok],
        out_shape=[jax.ShapeDtypeStruct((B, T, D), F32), jax.ShapeDtypeStruct((B, T, D), BF16)],
        compiler_params=_cparams("parallel", "parallel"), name="merge",
    )(da, dn, rest, rest, xs, w_ba, w_bb, w_o, g2.reshape(1, D), modsel)


def _desc_tops(x, n):
    tops = []
    for _ in range(n):
        m = jnp.max(x, axis=0, keepdims=True)
        tops.append(m)
        x = jnp.where(x == m, -jnp.inf, x)
    return tops


def _peer_prep_kernel(h_ref, wq_ref, keys_ref, a_ref, b_ref, kap_ref):
    tm = h_ref.shape[0]
    q = jnp.dot(h_ref[...], wq_ref[...], preferred_element_type=F32).astype(BF16)
    for h in range(PK_HEADS):
        st = [lax.dot_general(keys_ref[2 * h + p], q[:, (2 * h + p) * PK_HALF:(2 * h + p + 1) * PK_HALF],
                              NT_DIMS, preferred_element_type=F32) for p in range(2)]
        ta = _desc_tops(st[0], PK_TOPK + 1)
        tb = _desc_tops(st[1], PK_TOPK + 1)
        tbs = jnp.concatenate(tb[:PK_TOPK], axis=0)
        tb8 = tbs[0:SUBLANES]
        row8 = lax.broadcasted_iota(jnp.int32, tb8.shape, 0)
        pieces = [ta[0] + tbs, ta[1] + tb8]
        for r in range(2, SUBLANES):
            pieces.append(jnp.where(row8 < PK_TOPK // (r + 1), ta[r] + tb8, -jnp.inf))
        pieces.append(jnp.concatenate(ta[SUBLANES:PK_TOPK], axis=0) + tb[0])
        cand = jnp.concatenate(pieces, axis=0)
        best = _desc_tops(cand, PK_TOPK + 1)
        mx = best[0]
        zsum = best[0] * 0.0
        for r in range(PK_TOPK):
            zsum = zsum + jnp.exp(best[r] - mx)
        nxt = jnp.maximum(best[PK_TOPK], jnp.maximum(ta[PK_TOPK] + tb[0], ta[0] + tb[PK_TOPK]))
        thr = 0.5 * (best[PK_TOPK - 1] + nxt)
        rz = GELU_FOLD / zsum
        a_ref[h] = jnp.exp(st[0] - ta[0])
        b_ref[h] = (jnp.exp(st[1] - tb[0]) * rz).astype(b_ref.dtype)
        kap_ref[h:h + 1, :] = jnp.exp(thr - mx) * rz


def _peer_prep(h2, wq, keys):
    M, D = h2.shape
    tm = _pick(M, (256, 128))
    nk = 2 * PK_HEADS
    return pl.pallas_call(
        _peer_prep_kernel, grid=(M // tm,),
        in_specs=[pl.BlockSpec((tm, D), lambda i: (i, 0)),
                  pl.BlockSpec((D, nk * PK_HALF), lambda i: (0, 0)),
                  pl.BlockSpec((nk, N_KEYS, PK_HALF), lambda i: (0, 0, 0))],
        out_specs=[pl.BlockSpec((PK_HEADS, N_KEYS, tm), lambda i: (0, 0, i)),
                   pl.BlockSpec((PK_HEADS, N_KEYS, tm), lambda i: (0, 0, i)),
                   pl.BlockSpec((PK_HEADS, tm), lambda i: (0, i))],
        out_shape=[jax.ShapeDtypeStruct((PK_HEADS, N_KEYS, M), F32),
                   jax.ShapeDtypeStruct((PK_HEADS, N_KEYS, M), BF16),
                   jax.ShapeDtypeStruct((PK_HEADS, M), F32)],
        compiler_params=_cparams("parallel"), name="peer_prep",
    )(h2, wq, keys)


def _peer_dense_kernel(h_ref, u_ref, vt_ref, a_ref, b_ref, kap_ref, o_ref, acc_sc, *, sub):
    e = pl.program_id(1)
    tm = h_ref.shape[0]
    eb = u_ref.shape[0]
    ni = eb // N_KEYS

    @pl.when(e == 0)
    def _():
        acc_sc[...] = jnp.zeros(acc_sc.shape, F32)

    def rows16(x):
        x16 = jnp.broadcast_to(x, (2 * SUBLANES, sub)).astype(BF16)
        return jnp.concatenate([x16] * (N_KEYS // (2 * SUBLANES)), axis=0)

    def key_scores(t):
        return lax.dot_general(u_ref[...], h_ref[t * sub:(t + 1) * sub, :], NT_DIMS,
                               preferred_element_type=F32)

    def gates(t):
        tok = slice(t * sub, (t + 1) * sub)
        kaps = [rows16(kap_ref[h:h + 1, tok]) for h in range(PK_HEADS)]
        zero = jnp.zeros((N_KEYS, sub), BF16)
        ws = []
        for il in range(ni):
            w = zero
            for h in range(PK_HEADS):
                prod = b_ref[h, :, tok] * rows16(a_ref[h, il:il + 1, tok])
                w = w + jnp.where(prod >= kaps[h], prod, zero)
            ws.append(w)
        return ws

    n_slab = tm // sub
    sc_next = key_scores(0)
    w_next = gates(0)
    for t in range(n_slab):
        sc, ws = sc_next, w_next
        if t + 1 < n_slab:
            sc_next = key_scores(t + 1)
        was = []
        for il in range(ni):
            t_ = sc[il * N_KEYS:(il + 1) * N_KEYS, :]
            was.append(ws[il] * (t_ + t_ * lax.erf(t_)).astype(BF16))
        acc_sc[:, t * sub:(t + 1) * sub] += jnp.dot(vt_ref[...], jnp.concatenate(was, axis=0),
                                                    preferred_element_type=F32)
        if t + 1 < n_slab:
            w_next = gates(t + 1)

    @pl.when(e == pl.num_programs(1) - 1)
    def _():
        o_ref[...] = acc_sc[...].T


def _peer_dense(h2, u, vt, a, b, kap):
    M, D = h2.shape
    E = u.shape[0]
    tm = _pick(M, (1024, 512, 256, 128))
    sub = _pick(tm, (256, 128))
    eb = SUBLANES * N_KEYS
    kern = functools.partial(_peer_dense_kernel, sub=sub)
    return pl.pallas_call(
        kern, grid=(M // tm, E // eb),
        in_specs=[pl.BlockSpec((tm, D), lambda i, e: (i, 0)),
                  pl.BlockSpec((eb, D), lambda i, e: (e, 0)),
                  pl.BlockSpec((D, eb), lambda i, e: (0, e)),
                  pl.BlockSpec((PK_HEADS, SUBLANES, tm), lambda i, e: (0, e, i)),
                  pl.BlockSpec((PK_HEADS, N_KEYS, tm), lambda i, e: (0, 0, i)),
                  pl.BlockSpec((PK_HEADS, tm), lambda i, e: (0, i))],
        out_specs=pl.BlockSpec((tm, D), lambda i, e: (i, 0)),
        out_shape=jax.ShapeDtypeStruct((M, D), F32),
        scratch_shapes=[pltpu.VMEM((D, tm), F32)],
        compiler_params=_cparams("parallel", "arbitrary"), name="peer_dense",
    )(h2, u, vt, a, b, kap)


def _rope_tables(n_ctx, S):
    n_freq = DA_DIM // 4
    n_rows = S // GRID_W
    row = jnp.repeat(jnp.arange(n_rows, dtype=F32), GRID_W)
    col = jnp.tile(jnp.arange(GRID_W, dtype=F32), n_rows)
    inv = ROPE_BASE ** (-jnp.arange(n_freq, dtype=F32) / n_freq)
    ang = jnp.concatenate([row[:, None] * inv, col[:, None] * inv], axis=-1)
    cos = jnp.concatenate([jnp.ones((n_ctx, DA_DIM // 2), F32), jnp.cos(ang)], axis=0)
    sin = jnp.concatenate([jnp.zeros((n_ctx, DA_DIM // 2), F32), jnp.sin(ang)], axis=0)
    return jnp.concatenate([cos] * 4, axis=1), jnp.concatenate([-sin, sin, -sin, sin], axis=1)


def _deinterleave_perm():
    idx = np.arange(DA_HEADS * 2 * DA_DIM).reshape(DA_HEADS * 2, DA_DIM // 2, 2)
    return np.concatenate([idx[:, :, 0], idx[:, :, 1]], axis=1).reshape(-1)


def kernel(x, c, ctx, c_ctx, w_ada, b_ada, norm1_g, norm2_g, w_in, da_lambda, da_subln, dn_conv, dn_a_log,
           dn_dt_bias, dn_norm, w_branch_a, w_branch_b, w_out, peer_wq, peer_keys, peer_u, peer_v, final_g):
    B, S, D = x.shape
    n_ctx = ctx.shape[1]
    T = n_ctx + S
    L = w_ada.shape[0]
    M = B * T
    assert B < SUBLANES and S % GRID_W == 0 and n_ctx % DN_CHUNK == 0 and S % DN_CHUNK == 0
    da_qk = DA_HEADS * 2 * DA_DIM
    da_w = DA_HEADS * DA_VDIM
    dn_qkv = 3 * DN_HEADS * DN_DK
    dn_w = DN_HEADS * DN_DV
    nh2 = 2 * DN_HEADS

    c_all = jnp.zeros((SUBLANES, D), F32).at[:B].set(c).at[B].set(c_ctx)
    mod = _ada(c_all, w_ada, b_ada).reshape(L, SUBLANES, 6, D)
    modsel = jnp.stack([jnp.broadcast_to(mod[:, B:B + 1], (L, B, 6, D)), mod[:, :B]], axis=2)

    cos_t, sin_t = _rope_tables(n_ctx, S)
    cos_m = jnp.tile(cos_t, (B, 1))
    sin_m = jnp.tile(sin_t, (B, 1))
    perm = _deinterleave_perm()

    xs = jnp.concatenate([ctx, x], axis=1)
    f = None
    for l in range(L):
        lam_init = 0.8 - 0.6 * math.exp(-0.3 * l)
        w = w_in[l]
        o = 0
        w_q = w[:, o:o + da_qk][:, perm]; o += da_qk
        w_k = w[:, o:o + da_qk][:, perm]; o += da_qk
        w_v = w[:, o:o + da_w]; o += da_w
        w_dn = w[:, o:o + dn_qkv]; o += dn_qkv
        w_z = w[:, o:o + dn_w]; o += dn_w
        w_b = w[:, o:o + nh2]; o += nh2
        w_a = w[:, o:o + nh2]; o += nh2
        w_ga = w[:, o:o + D]; o += D
        w_gb = w[:, o:o + D]; o += D
        w_qk = jnp.concatenate([w_q, w_k], axis=1).astype(BF16)
        w_rest = jnp.concatenate([w_dn, w_z, w_ga, w_gb], axis=1).astype(BF16)
        dn_col0, z_col0, ga_col0 = 0, dn_qkv, dn_qkv + dn_w
        w_ba_pad = jnp.zeros((D, LANES), F32).at[:, :nh2].set(w_b).at[:, nh2:2 * nh2].set(w_a).astype(BF16)
        gate_par = (jnp.zeros((SUBLANES, LANES), F32)
                    .at[0, nh2:2 * nh2].set(dn_a_log[l].reshape(-1))
                    .at[1, nh2:2 * nh2].set(dn_dt_bias[l].reshape(-1)))

        if l == 0:
            h = _modulate(xs, norm1_g[l], modsel[l], n_ctx)
        else:
            xs, h = _resid_modulate(xs, f, modsel[l - 1], norm1_g[l], modsel[l], n_ctx)
        hm = h.reshape(M, D)

        qk = _proj(hm, w_qk, out_dtype=BF16, kernel=_proj_rope_kernel, extra=(cos_m, sin_m),
                   extra_specs=lambda tm: [pl.BlockSpec((tm, LANES), lambda j, i: (i, 0))] * 2,
                   name="proj_qk_rope").reshape(B, T, 2 * da_qk)
        rest = _proj(hm, w_rest, out_dtype=BF16, extra_specs=lambda tm: [], name="proj_rest").reshape(B, T, -1)
        gates = _proj(hm, w_ba_pad, out_dtype=F32, kernel=_proj_gates_kernel, extra=(gate_par,),
                      extra_specs=lambda tm: [pl.BlockSpec((SUBLANES, LANES), lambda j, i: (0, 0))],
                      name="proj_gates").reshape(B, T, LANES)

        vt = _proj_t(hm, w_v.T.astype(BF16), out_dtype=BF16, name="proj_v_t")

        da = _attention(qk, vt, da_lambda[l], da_subln[l], n_ctx, lam_init)
        dnq = _dn_prep(rest, dn_conv[l], n_ctx, dn_col0)
        dn = _gdn(dnq, gates, rest, z_col0, dn_norm[l], n_ctx)

        xs, h2 = _merge(da, dn, rest, ga_col0, xs, w_branch_a[l].astype(BF16), w_branch_b[l].astype(BF16),
                        w_out[l].astype(BF16), norm2_g[l], modsel[l], n_ctx)
        h2m = h2.reshape(M, D)
        pa, pb, kap = _peer_prep(h2m, peer_wq[l].astype(BF16),
                                 peer_keys[l].reshape(2 * PK_HEADS, N_KEYS, PK_HALF).astype(BF16))
        f = _peer_dense(h2m, (peer_u[l] * GELU_FOLD).astype(BF16), peer_v[l].T.astype(BF16),
                        pa, pb, kap).reshape(B, T, D)

    return _final(xs, f, modsel[L - 1], final_g, n_ctx)
```

```python
import functools
import math

import jax
import jax.numpy as jnp
import numpy as np
from jax import lax
from jax.experimental import pallas as pl
from jax.experimental.pallas import tpu as pltpu

F32 = jnp.float32
BF16 = jnp.bfloat16
HIGHEST = lax.Precision.HIGHEST

GRID_W = 64
EPS = 1e-6
ROPE_BASE = 10000.0
DA_HEADS = 8
DA_DIM = 64
DA_VDIM = 2 * DA_DIM
DN_HEADS = 8
DN_DK = 128
DN_DV = 128
DN_CONV = 5
DN_CHUNK = 64
PK_HEADS = 8
N_KEYS = 128
PK_TOPK = 16
PK_HALF = 128
GELU_FOLD = 2.0 ** -0.5

LANES = 128
SUBLANES = 8
VMEM_LIMIT_BYTES = 56 * 1024 * 1024

NT_DIMS = (((1,), (1,)), ((), ()))


def _cparams(*sem, flags=None):
    return pltpu.CompilerParams(dimension_semantics=sem, vmem_limit_bytes=VMEM_LIMIT_BYTES, flags=flags)


def _pick(n, cands):
    for c in cands:
        if n % c == 0:
            return c
    raise ValueError(f"no tile in {cands} divides {n}")


def _silu(x):
    return x * jax.nn.sigmoid(x)


def _ada_kernel(c_ref, w_ref, b_ref, o_ref):
    sc = _silu(c_ref[...])
    o_ref[0] = jnp.dot(sc, w_ref[0], preferred_element_type=F32, precision=HIGHEST) + b_ref[0]


def _ada(c_all, w_ada, b_ada):
    L, D, N = w_ada.shape
    tn = _pick(N, (1536, 1024, 512, 256, 128))
    return pl.pallas_call(
        _ada_kernel,
        grid=(L, N // tn),
        in_specs=[pl.BlockSpec((SUBLANES, D), lambda l, j: (0, 0)),
                  pl.BlockSpec((1, D, tn), lambda l, j: (l, 0, j)),
                  pl.BlockSpec((1, 1, tn), lambda l, j: (l, 0, j))],
        out_specs=pl.BlockSpec((1, SUBLANES, tn), lambda l, j: (l, 0, j)),
        out_shape=jax.ShapeDtypeStruct((L, SUBLANES, N), F32),
        compiler_params=_cparams("parallel", "parallel"),
        name="ada",
    )(c_all, w_ada, b_ada.reshape(L, 1, N))


def _rms_mod(x, g, m, shift_row, scale_row):
    ms = jnp.mean(x * x, axis=-1, keepdims=True)
    y = x * lax.rsqrt(ms + EPS) * g
    return y * (1.0 + m[scale_row:scale_row + 1]) + m[shift_row:shift_row + 1]


def _modulate_kernel(x_ref, g_ref, m_ref, h_ref):
    h_ref[...] = _rms_mod(x_ref[...], g_ref[...], m_ref[...], 0, 1).astype(BF16)


def _resid_modulate_kernel(x_ref, f_ref, mp_ref, g_ref, m_ref, xo_ref, h_ref):
    x = x_ref[...] + mp_ref[...][5:6] * f_ref[...]
    xo_ref[...] = x
    h_ref[...] = _rms_mod(x, g_ref[...], m_ref[...], 0, 1).astype(BF16)


def _seg_specs(tm, n_ctx_tiles, D):
    tok = pl.BlockSpec((None, tm, D), lambda b, t: (b, t, 0))
    gain = pl.BlockSpec((1, D), lambda b, t: (0, 0))
    mod = pl.BlockSpec((None, None, 6, D), lambda b, t: (b, jnp.where(t >= n_ctx_tiles, 1, 0), 0, 0))
    return tok, gain, mod


def _modulate(xs, g, modsel, n_ctx):
    B, T, D = xs.shape
    tm = _pick(math.gcd(n_ctx, T - n_ctx), (256, 128))
    tok, gain, mod = _seg_specs(tm, n_ctx // tm, D)
    return pl.pallas_call(
        _modulate_kernel, grid=(B, T // tm),
        in_specs=[tok, gain, mod], out_specs=tok,
        out_shape=jax.ShapeDtypeStruct((B, T, D), BF16),
        compiler_params=_cparams("parallel", "parallel"), name="modulate",
    )(xs, g.reshape(1, D), modsel)


def _resid_modulate(xs, f, modsel_prev, g, modsel, n_ctx):
    B, T, D = xs.shape
    tm = _pick(math.gcd(n_ctx, T - n_ctx), (256, 128))
    tok, gain, mod = _seg_specs(tm, n_ctx // tm, D)
    return pl.pallas_call(
        _resid_modulate_kernel, grid=(B, T // tm),
        in_specs=[tok, tok, mod, gain, mod], out_specs=[tok, tok],
        out_shape=[jax.ShapeDtypeStruct((B, T, D), F32), jax.ShapeDtypeStruct((B, T, D), BF16)],
        compiler_params=_cparams("parallel", "parallel"), name="resid_modulate",
    )(xs, f, modsel_prev, g.reshape(1, D), modsel)


def _final_kernel(x_ref, f_ref, mp_ref, g_ref, o_ref):
    x = x_ref[...] + mp_ref[...][5:6] * f_ref[...]
    ms = jnp.mean(x * x, axis=-1, keepdims=True)
    o_ref[...] = x * lax.rsqrt(ms + EPS) * g_ref[...]


def _final(xs, f, modsel_prev, g, n_ctx):
    B, T, D = xs.shape
    S = T - n_ctx
    tm = _pick(math.gcd(n_ctx, S), (256, 128))
    off = n_ctx // tm
    tok_in = pl.BlockSpec((None, tm, D), lambda b, t: (b, t + off, 0))
    return pl.pallas_call(
        _final_kernel, grid=(B, S // tm),
        in_specs=[tok_in, tok_in,
                  pl.BlockSpec((None, None, 6, D), lambda b, t: (b, 1, 0, 0)),
                  pl.BlockSpec((1, D), lambda b, t: (0, 0))],
        out_specs=pl.BlockSpec((None, tm, D), lambda b, t: (b, t, 0)),
        out_shape=jax.ShapeDtypeStruct((B, S, D), F32),
        compiler_params=_cparams("parallel", "parallel"), name="final_norm",
    )(xs, f, modsel_prev, g.reshape(1, D))


def _proj_kernel(h_ref, w_ref, o_ref):
    o_ref[...] = jnp.dot(h_ref[...], w_ref[...], preferred_element_type=F32).astype(o_ref.dtype)


def _proj_rope_kernel(h_ref, w_ref, cos_ref, sin_ref, o_ref):
    acc = jnp.dot(h_ref[...], w_ref[...], preferred_element_type=F32)
    tn = acc.shape[1]
    half = DA_DIM // 2
    lane = lax.broadcasted_iota(jnp.int32, acc.shape, 1)
    partner = jnp.where((lane & (DA_DIM - 1)) < half,
                        pltpu.roll(acc, tn - half, axis=1),
                        pltpu.roll(acc, half, axis=1))
    reps = tn // LANES
    cos = jnp.concatenate([cos_ref[...]] * reps, axis=1)
    sin = jnp.concatenate([sin_ref[...]] * reps, axis=1)
    y = acc * cos + partner * sin
    scale = jnp.where(pl.program_id(0) == 0, DA_DIM ** -0.5 * math.log2(math.e), 1.0).astype(F32)
    o_ref[...] = (y * scale).astype(o_ref.dtype)


def _proj_gates_kernel(h_ref, w_ref, p_ref, o_ref):
    acc = jnp.dot(h_ref[...], w_ref[...], preferred_element_type=F32)
    p = p_ref[...]
    lane = lax.broadcasted_iota(jnp.int32, acc.shape, 1)
    beta = jax.nn.sigmoid(acc)
    g = -jnp.exp(p[0:1]) * jax.nn.softplus(acc + p[1:2])
    o_ref[...] = jnp.where(lane < 2 * DN_HEADS, beta, jnp.where(lane < 4 * DN_HEADS, g, 0.0))


def _proj(h, w, *, out_dtype, kernel=_proj_kernel, extra=(), extra_specs=(), name="proj"):
    M, D = h.shape
    N = w.shape[1]
    tm = _pick(M, (1024, 512, 256, 128))
    tn = _pick(N, (1024, 512, 256, 128))
    return pl.pallas_call(
        kernel, grid=(N // tn, M // tm),
        in_specs=[pl.BlockSpec((tm, D), lambda j, i: (i, 0)),
                  pl.BlockSpec((D, tn), lambda j, i: (0, j))] + list(extra_specs(tm)),
        out_specs=pl.BlockSpec((tm, tn), lambda j, i: (i, j)),
        out_shape=jax.ShapeDtypeStruct((M, N), out_dtype),
        compiler_params=_cparams("parallel", "parallel"), name=name,
    )(h, w, *extra)


def _proj_t_kernel(h_ref, wt_ref, o_ref):
    o_ref[...] = lax.dot_general(wt_ref[...], h_ref[...], NT_DIMS, preferred_element_type=F32).astype(o_ref.dtype)


def _proj_t(h, wt, *, out_dtype, name):
    M, D = h.shape
    N = wt.shape[0]
    tm = _pick(M, (1024, 512, 256, 128))
    tn = _pick(N, (1024, 512, 256, 128))
    return pl.pallas_call(
        _proj_t_kernel, grid=(N // tn, M // tm),
        in_specs=[pl.BlockSpec((tm, D), lambda j, i: (i, 0)),
                  pl.BlockSpec((tn, D), lambda j, i: (j, 0))],
        out_specs=pl.BlockSpec((tn, tm), lambda j, i: (j, i)),
        out_shape=jax.ShapeDtypeStruct((N, M), out_dtype),
        compiler_params=_cparams("parallel", "parallel"), name=name,
    )(h, wt)


def _attn_kernel(lam_ref, q_ref, k_ref, vt_ref, g_ref, o_ref, *, n_ctx, tk, lam_init):
    tq = q_ref.shape[0]
    T = k_ref.shape[0]
    qi = pl.program_id(2)
    q = q_ref[...]
    lane = lax.broadcasted_iota(jnp.int32, q.shape, 1)
    zero = jnp.zeros_like(q)
    qq = jnp.concatenate([jnp.where(lane < DA_DIM, q, zero), jnp.where(lane >= DA_DIM, q, zero)], axis=0)

    def scores(r0, rows):
        return lax.dot_general(k_ref[r0:r0 + rows, :], qq, NT_DIMS, preferred_element_type=F32)

    def step(state, s, r0, rows):
        m_prev, l_prev, acc = state
        m_new = jnp.maximum(m_prev, jnp.max(s, axis=0, keepdims=True))
        alpha = jnp.exp2(m_prev - m_new)
        p = jnp.exp2(s - m_new)
        pv = jnp.dot(vt_ref[:, r0:r0 + rows], p.astype(BF16), preferred_element_type=F32)
        return m_new, alpha * l_prev + jnp.sum(p, axis=0, keepdims=True), alpha * acc + pv

    lam = lam_ref[...]
    lam_val = (jnp.exp(jnp.sum(lam[0:1] * lam[1:2], axis=-1, keepdims=True))
               - jnp.exp(jnp.sum(lam[2:3] * lam[3:4], axis=-1, keepdims=True)) + lam_init)

    def finish(state):
        _, l_fin, acc = state
        o_all = acc / l_fin
        o = o_all[:, 0:tq] - lam_val * o_all[:, tq:2 * tq]
        ms = jnp.mean(o * o, axis=0, keepdims=True)
        y = o * lax.rsqrt(ms + EPS) * (1.0 - lam_init)
        o_ref[...] = (y.T * g_ref[...]).astype(o_ref.dtype)

    init = (jnp.full((1, 2 * tq), -jnp.inf, F32), jnp.zeros((1, 2 * tq), F32), jnp.zeros((DA_VDIM, 2 * tq), F32))
    def attend(blocks):
        state = init
        s_next = scores(*blocks[0])
        for j, (r0, rows) in enumerate(blocks):
            s_cur = s_next
            if j + 1 < len(blocks):
                s_next = scores(*blocks[j + 1])
            state = step(state, s_cur, r0, rows)
        finish(state)

    @pl.when(qi * tq < n_ctx)
    def _():
        attend([(0, n_ctx)])

    @pl.when(qi * tq >= n_ctx)
    def _():
        attend([(0, n_ctx)] + [(n_ctx + j * tk, tk) for j in range((T - n_ctx) // tk)])


def _attention(qk, vt, lam, subln, n_ctx, lam_init):
    B, T, _ = qk.shape
    H = DA_HEADS
    S = T - n_ctx
    tq = _pick(math.gcd(n_ctx, S), (256, 128))
    tk = _pick(S, (2048, 1024, 512, 256, 128))
    kern = functools.partial(_attn_kernel, n_ctx=n_ctx, tk=tk, lam_init=lam_init)
    return pl.pallas_call(
        kern, grid=(B, H, T // tq),
        in_specs=[pl.BlockSpec((4, DA_DIM), lambda b, h, i: (0, 0)),
                  pl.BlockSpec((None, tq, DA_VDIM), lambda b, h, i: (b, i, h)),
                  pl.BlockSpec((None, T, DA_VDIM), lambda b, h, i: (b, 0, H + h)),
                  pl.BlockSpec((DA_VDIM, T), lambda b, h, i: (h, b)),
                  pl.BlockSpec((1, DA_VDIM), lambda b, h, i: (0, 0))],
        out_specs=pl.BlockSpec((None, tq, DA_VDIM), lambda b, h, i: (b, i, h)),
        out_shape=jax.ShapeDtypeStruct((B, T, H * DA_VDIM), BF16),
        compiler_params=_cparams("parallel", "parallel", "arbitrary"), name="diff_attention",
    )(lam, qk, qk, vt, subln.reshape(1, DA_VDIM))


def _dn_prep_kernel(x_ref, w_ref, o_ref, pad_sc, *, n_ctx, rows):
    T = x_ref.shape[0]
    halo = SUBLANES
    cb = pl.program_id(1)
    zpad = jnp.zeros((halo, LANES), F32)
    pad_sc[0:halo, :] = zpad
    pad_sc[halo + n_ctx:2 * halo + n_ctx, :] = zpad
    pad_sc[2 * halo + T:3 * halo + T, :] = zpad
    pad_sc[halo:halo + n_ctx, :] = x_ref[0:n_ctx, :].astype(F32)
    pad_sc[2 * halo + n_ctx:2 * halo + T, :] = x_ref[n_ctx:T, :].astype(F32)
    w = w_ref[...]
    is_qk = cb < 2 * DN_HEADS
    post = jnp.where(cb < DN_HEADS, DN_DK ** -0.5, 1.0).astype(F32)
    pad = DN_CONV // 2

    def body(c, carry):
        r0 = pl.multiple_of(c * rows, rows)
        base = pl.multiple_of(r0 + jnp.where(r0 >= n_ctx, 2 * halo, halo), halo)
        y = jnp.zeros((rows, LANES), F32)
        for j in range(DN_CONV):
            y = y + pad_sc[pl.ds(base + (j - pad), rows), :] * w[j:j + 1]
        y = _silu(y)
        yn = y * lax.rsqrt(jnp.sum(y * y, axis=-1, keepdims=True) + EPS) * post
        o_ref[pl.ds(r0, rows), :] = jnp.where(is_qk, yn, y).astype(o_ref.dtype)
        return carry

    lax.fori_loop(0, T // rows, body, 0)


def _dn_prep(rest, conv_w, n_ctx, col0):
    B, T, _ = rest.shape
    nblk = 3 * DN_HEADS
    rows = _pick(math.gcd(n_ctx, T - n_ctx), (256, 128))
    cw = jnp.zeros((SUBLANES, nblk * LANES), F32).at[:DN_CONV].set(conv_w)
    kern = functools.partial(_dn_prep_kernel, n_ctx=n_ctx, rows=rows)
    return pl.pallas_call(
        kern, grid=(B, nblk),
        in_specs=[pl.BlockSpec((None, T, LANES), lambda b, c: (b, 0, col0 // LANES + c)),
                  pl.BlockSpec((SUBLANES, LANES), lambda b, c: (0, c))],
        out_specs=pl.BlockSpec((None, T, LANES), lambda b, c: (b, 0, c)),
        out_shape=jax.ShapeDtypeStruct((B, T, nblk * LANES), BF16),
        scratch_shapes=[pltpu.VMEM((T + 3 * SUBLANES, LANES), F32)],
        compiler_params=_cparams("parallel", "parallel"), name="dn_prep",
    )(rest, cw)


GDN_HEADS_PER_STEP = 2
GDN_CHUNKS_PER_ITER = 4
GDN_INV_PASSES = 1


def _split_dot(a, b, passes):
    a_hi = a.astype(BF16)
    b_hi = b.astype(BF16)
    out = jnp.dot(a_hi, b_hi, preferred_element_type=F32)
    if passes >= 3:
        a_lo = (a - a_hi.astype(F32)).astype(BF16)
        b_lo = (b - b_hi.astype(F32)).astype(BF16)
        out = out + jnp.dot(a_hi, b_lo, preferred_element_type=F32) + jnp.dot(a_lo, b_hi, preferred_element_type=F32)
    return out


def _gdn_kernel(q_ref, k_ref, v_ref, gt_ref, z_ref, ng_ref, o_ref,
                w_sc, qg_sc, kt_sc, a_sc, u_sc, egl_sc, oacc_sc, s_sc, *, n_ctx):
    C = DN_CHUNK
    C2 = 2 * C
    T = q_ref.shape[0]
    nc = T // C
    ncc = n_ctx // C
    hp = pl.program_id(1)

    row = lax.broadcasted_iota(jnp.int32, (C2, C2), 0)
    col = lax.broadcasted_iota(jnp.int32, (C2, C2), 1)
    fwd_row = row < C
    rc_xor = row ^ col
    ahead = (col - row) * jnp.where(fwd_row, 1, -1)
    same_dir = rc_xor < C
    incl = same_dir & (ahead <= 0)
    strict = same_dir & (ahead < 0)
    eye = (row == col).astype(F32)
    mcs = jnp.where(incl, 1.0, 0.0).astype(BF16)

    def p1_load(c, s):
        r0 = pl.multiple_of(c * C, C)
        head = hp * GDN_HEADS_PER_STEP + s
        lo, hi = s * LANES, (s + 1) * LANES
        k = k_ref[pl.ds(r0, C), lo:hi]
        q = q_ref[pl.ds(r0, C), lo:hi]
        v = v_ref[pl.ds(r0, C), lo:hi]
        kk = jnp.concatenate([k, k], axis=0)
        qq = jnp.concatenate([q, q], axis=0)
        vf = jnp.concatenate([v, v], axis=0).astype(F32)
        x = gt_ref[pl.ds(r0, C), :]
        x2 = jnp.concatenate([x, x], axis=0)
        bsel = jnp.where(fwd_row, head, DN_HEADS + head)
        gsel = bsel + 2 * DN_HEADS
        beta = jnp.sum(jnp.where(col == bsel, x2, 0.0), axis=-1, keepdims=True)
        glog = jnp.sum(jnp.where(col == gsel, x2, 0.0), axis=-1, keepdims=True)
        return dict(c=c, s=s, kk=kk, qq=qq, vf=vf, beta=beta, glog=glog)

    def p1_prep(d):
        g_rem = jnp.broadcast_to(d["glog"], (C2, C2))
        gcum = jnp.zeros((C2, C2), F32)
        for _ in range(3):
            piece = g_rem.astype(BF16)
            gcum = gcum + jnp.dot(mcs, piece, preferred_element_type=F32)
            g_rem = g_rem - piece.astype(F32)
        decay = jnp.exp(jnp.where(incl, gcum - gcum.T, -jnp.inf))
        kkt = lax.dot_general(d["kk"], d["kk"], NT_DIMS, preferred_element_type=F32)
        qkt = lax.dot_general(d["qq"], d["kk"], NT_DIMS, preferred_element_type=F32)
        d.update(gcum=gcum, decay=decay, qkt=qkt, lmat=jnp.where(strict, d["beta"] * kkt * decay, 0.0), tinv=eye)

    def p1_finish(d):
        c, s, gcum, beta = d["c"], d["s"], d["gcum"], d["beta"]
        kf = d["kk"].astype(F32)
        eg = jnp.exp(gcum)
        rhs = jnp.concatenate([(d["vf"] * beta).astype(BF16), (kf * beta * eg).astype(BF16)], axis=1)
        uw = jnp.dot(d["tinv"].astype(BF16), rhs, preferred_element_type=F32)
        glast = jnp.where(fwd_row, gcum[C - 1:C, :], gcum[C:C + 1, :])
        ktail = kf * jnp.exp(glast - gcum)
        u_sc[s, c] = uw[:, 0:LANES]
        w_sc[s, c] = uw[:, LANES:2 * LANES].astype(BF16)
        qg_sc[s, c] = (d["qq"].astype(F32) * eg).astype(BF16)
        a_sc[s, c] = (d["qkt"] * d["decay"]).astype(BF16)
        kt_sc[s, c] = ktail.T.astype(BF16)
        egl_sc[s, c] = jnp.exp(jnp.concatenate([jnp.broadcast_to(gcum[C - 1:C, :], (4, LANES)),
                                                jnp.broadcast_to(gcum[C:C + 1, :], (4, LANES))], axis=0))

    def p1_body(i, carry):
        probs = [p1_load(i * GDN_CHUNKS_PER_ITER + j, s)
                 for j in range(GDN_CHUNKS_PER_ITER) for s in range(GDN_HEADS_PER_STEP)]
        for d in probs:
            p1_prep(d)
        for lvl in range(int(math.log2(C))):
            for d in probs:
                d["x"] = _split_dot(d["tinv"], jnp.where((rc_xor >> lvl) == 1, d["lmat"], 0.0), GDN_INV_PASSES)
            for d in probs:
                d["tinv"] = d["tinv"] - _split_dot(d["x"], d["tinv"], GDN_INV_PASSES)
        for d in probs:
            p1_finish(d)
        return carry

    lax.fori_loop(0, nc // GDN_CHUNKS_PER_ITER, p1_body, 0)

    oacc_sc[...] = jnp.zeros(oacc_sc.shape, F32)
    s_sc[...] = jnp.zeros(s_sc.shape, F32)
    lane_b = lax.broadcasted_iota(jnp.int32, (LANES, LANES), 1)
    zpad = jnp.zeros((C, LANES), BF16)

    def p2_body(i, carry):
        cf = i
        cb = jnp.where(i < ncc, ncc - 1 - i, nc - 1 + ncc - i)
        rf = pl.multiple_of(cf * C, C)
        rb = pl.multiple_of(cb * C, C)
        heads = range(GDN_HEADS_PER_STEP)
        st = [s_sc[s] for s in heads]
        lhs1 = [jnp.concatenate([
            jnp.concatenate([w_sc[s, cf, 0:C, :], zpad], axis=1),
            jnp.concatenate([zpad, w_sc[s, cb, C:C2, :]], axis=1),
            jnp.concatenate([qg_sc[s, cf, 0:C, :], zpad], axis=1),
            jnp.concatenate([zpad, qg_sc[s, cb, C:C2, :]], axis=1)], axis=0) for s in heads]
        zk = jnp.zeros((LANES, C2), BF16)
        lhs2 = [jnp.concatenate([a_sc[s, cf, 0:C, :], a_sc[s, cb, C:C2, :],
                                 jnp.where(lane_b < C, kt_sc[s, cf], zk),
                                 jnp.where(lane_b >= C, kt_sc[s, cb], zk)], axis=0) for s in heads]
        u = [jnp.concatenate([u_sc[s, cf, 0:C, :], u_sc[s, cb, C:C2, :]], axis=0) for s in heads]
        scale = [jnp.concatenate([jnp.broadcast_to(egl_sc[s, cf][0:1], (DN_DK, DN_DV)),
                                  jnp.broadcast_to(egl_sc[s, cb][4:5], (DN_DK, DN_DV))], axis=0) for s in heads]
        r1 = [jnp.dot(lhs1[s], st[s].astype(BF16), preferred_element_type=F32) for s in heads]
        vnew = [(u[s] - r1[s][0:C2]).astype(BF16) for s in heads]
        r2 = [jnp.dot(lhs2[s], vnew[s], preferred_element_type=F32) for s in heads]
        for s in heads:
            s_sc[s] = st[s] * scale[s] + r2[s][C2:C2 + 2 * DN_DK]
        for s in heads:
            oacc_sc[s, pl.ds(rf, C), :] += r1[s][C2:C2 + C] + r2[s][0:C]
            oacc_sc[s, pl.ds(rb, C), :] += r1[s][C2 + C:2 * C2] + r2[s][C:C2]
        return carry

    lax.fori_loop(0, nc, p2_body, 0)

    ng = ng_ref[...]
    rows = _pick(T, (256, 128))

    def fin_body(c, carry):
        r0 = pl.multiple_of(c * rows, rows)
        for s in range(GDN_HEADS_PER_STEP):
            o = oacc_sc[s, pl.ds(r0, rows), :]
            zf = z_ref[pl.ds(r0, rows), s * LANES:(s + 1) * LANES].astype(F32)
            ms = jnp.mean(o * o, axis=-1, keepdims=True)
            o_ref[pl.ds(r0, rows), s * LANES:(s + 1) * LANES] = (o * lax.rsqrt(ms + EPS) * ng * _silu(zf)).astype(o_ref.dtype)
        return carry

    lax.fori_loop(0, T // rows, fin_body, 0)


def _gdn(dnq, gates, rest, z_col0, norm_g, n_ctx):
    B, T, _ = dnq.shape
    H = DN_HEADS
    hps = GDN_HEADS_PER_STEP
    wblk = hps * LANES
    nc = T // DN_CHUNK
    assert nc % GDN_CHUNKS_PER_ITER == 0
    C2 = 2 * DN_CHUNK
    kern = functools.partial(_gdn_kernel, n_ctx=n_ctx)
    nb = H // hps
    once = dict(pipeline_mode=pl.Buffered(1))
    return pl.pallas_call(
        kern, grid=(B, nb),
        in_specs=[pl.BlockSpec((None, T, wblk), lambda b, h: (b, 0, h), **once),
                  pl.BlockSpec((None, T, wblk), lambda b, h: (b, 0, nb + h), **once),
                  pl.BlockSpec((None, T, wblk), lambda b, h: (b, 0, 2 * nb + h), **once),
                  pl.BlockSpec((None, T, LANES), lambda b, h: (b, 0, 0), **once),
                  pl.BlockSpec((None, T, wblk), lambda b, h: (b, 0, z_col0 // wblk + h), **once),
                  pl.BlockSpec((1, DN_DV), lambda b, h: (0, 0))],
        out_specs=pl.BlockSpec((None, T, wblk), lambda b, h: (b, 0, h)),
        out_shape=jax.ShapeDtypeStruct((B, T, H * DN_DV), BF16),
        scratch_shapes=[pltpu.VMEM((hps, nc, C2, LANES), BF16),
                        pltpu.VMEM((hps, nc, C2, LANES), BF16),
                        pltpu.VMEM((hps, nc, LANES, C2), BF16),
                        pltpu.VMEM((hps, nc, C2, C2), BF16),
                        pltpu.VMEM((hps, nc, C2, LANES), F32),
                        pltpu.VMEM((hps, nc, SUBLANES, LANES), F32),
                        pltpu.VMEM((hps, T, LANES), F32),
                        pltpu.VMEM((hps, 2 * DN_DK, DN_DV), F32)],
        compiler_params=_cparams("parallel", "arbitrary"), name="gated_deltanet",
    )(dnq, dnq, dnq, gates, rest, norm_g.reshape(1, DN_DV))


def _merge_kernel(da_ref, dn_ref, ga_ref, gb_ref, x_ref, wa_ref, wb_ref, wo_ref, g_ref, m_ref, xo_ref, h_ref):
    ya = jnp.dot(da_ref[...], wa_ref[...], preferred_element_type=F32)
    yb = jnp.dot(dn_ref[...], wb_ref[...], preferred_element_type=F32)
    y = jax.nn.sigmoid(ga_ref[...].astype(F32)) * ya + jax.nn.sigmoid(gb_ref[...].astype(F32)) * yb
    y2 = jnp.dot(y.astype(BF16), wo_ref[...], preferred_element_type=F32)
    m = m_ref[...]
    x = x_ref[...] + m[2:3] * y2
    xo_ref[...] = x
    h_ref[...] = _rms_mod(x, g_ref[...], m, 3, 4).astype(BF16)


def _merge(da, dn, rest, ga_col0, xs, w_ba, w_bb, w_o, g2, modsel, n_ctx):
    B, T, D = xs.shape
    tm = _pick(math.gcd(n_ctx, T - n_ctx), (256, 128))
    tok, gain, mod = _seg_specs(tm, n_ctx // tm, D)
    wspec = pl.BlockSpec((D, D), lambda b, t: (0, 0))
    ga_blk = ga_col0 // D
    return pl.pallas_call(
        _merge_kernel, grid=(B, T // tm),
        in_specs=[tok, tok,
                  pl.BlockSpec((None, tm, D), lambda b, t: (b, t, ga_blk)),
                  pl.BlockSpec((None, tm, D), lambda b, t: (b, t, ga_blk + 1)),
                  tok, wspec, wspec, wspec, gain, mod],
        out_specs=[tok, tok],
        out_shape=[jax.ShapeDtypeStruct((B, T, D), F32), jax.ShapeDtypeStruct((B, T, D), BF16)],
        compiler_params=_cparams("parallel", "parallel"), name="merge",
    )(da, dn, rest, rest, xs, w_ba, w_bb, w_o, g2.reshape(1, D), modsel)


def _merge_sort_pairs(n):
    pairs = []
    p = 1
    while p < n:
        k = p
        while k >= 1:
            for j in range(k % p, n - k, 2 * k):
                for i in range(min(k, n - j - k)):
                    if (i + j) // (2 * p) == (i + j + k) // (2 * p):
                        pairs.append((i + j, i + j + k))
            k //= 2
        p *= 2
    return pairs


def _desc_tops(x, n):
    ng = x.shape[0] // SUBLANES
    g = [x[v * SUBLANES:(v + 1) * SUBLANES] for v in range(ng)]
    for i, j in _merge_sort_pairs(pl.next_power_of_2(ng)):
        if j < ng:
            g[i], g[j] = jnp.maximum(g[i], g[j]), jnp.minimum(g[i], g[j])
    neg = jnp.full_like(g[0], -jnp.inf)
    tops = []
    for r in range(n):
        m = jnp.max(g[0], axis=0, keepdims=True)
        tops.append(m)
        hit = g[0] == m
        for v in range(min(ng, n - r)):
            g[v] = jnp.where(hit, g[v + 1] if v + 1 < ng else neg, g[v])
    return tops


def _peer_prep_kernel(h_ref, wq_ref, keys_ref, a_ref, b_ref, kap_ref):
    tm = h_ref.shape[0]
    q = jnp.dot(h_ref[...], wq_ref[...], preferred_element_type=F32).astype(BF16)
    for h in range(PK_HEADS):
        st = [lax.dot_general(keys_ref[2 * h + p], q[:, (2 * h + p) * PK_HALF:(2 * h + p + 1) * PK_HALF],
                              NT_DIMS, preferred_element_type=F32) for p in range(2)]
        ta = _desc_tops(st[0], PK_TOPK + 1)
        tb = _desc_tops(st[1], PK_TOPK + 1)
        tbs = jnp.concatenate(tb[:PK_TOPK], axis=0)
        tb8 = tbs[0:SUBLANES]
        row8 = lax.broadcasted_iota(jnp.int32, tb8.shape, 0)
        pieces = [ta[0] + tbs, ta[1] + tb8]
        for r in range(2, SUBLANES):
            pieces.append(jnp.where(row8 < PK_TOPK // (r + 1), ta[r] + tb8, -jnp.inf))
        pieces.append(jnp.concatenate(ta[SUBLANES:PK_TOPK], axis=0) + tb[0])
        cand = jnp.concatenate(pieces, axis=0)
        best = _desc_tops(cand, PK_TOPK + 1)
        mx = best[0]
        zsum = best[0] * 0.0
        for r in range(PK_TOPK):
            zsum = zsum + jnp.exp(best[r] - mx)
        nxt = jnp.maximum(best[PK_TOPK], jnp.maximum(ta[PK_TOPK] + tb[0], ta[0] + tb[PK_TOPK]))
        thr = 0.5 * (best[PK_TOPK - 1] + nxt)
        rz = GELU_FOLD / zsum
        a_ref[h] = jnp.exp(st[0] - ta[0])
        b_ref[h] = (jnp.exp(st[1] - tb[0]) * rz).astype(b_ref.dtype)
        kap_ref[h:h + 1, :] = jnp.exp(thr - mx) * rz


def _peer_prep(h2, wq, keys):
    M, D = h2.shape
    tm = _pick(M, (256, 128))
    nk = 2 * PK_HEADS
    return pl.pallas_call(
        _peer_prep_kernel, grid=(M // tm,),
        in_specs=[pl.BlockSpec((tm, D), lambda i: (i, 0)),
                  pl.BlockSpec((D, nk * PK_HALF), lambda i: (0, 0)),
                  pl.BlockSpec((nk, N_KEYS, PK_HALF), lambda i: (0, 0, 0))],
        out_specs=[pl.BlockSpec((PK_HEADS, N_KEYS, tm), lambda i: (0, 0, i)),
                   pl.BlockSpec((PK_HEADS, N_KEYS, tm), lambda i: (0, 0, i)),
                   pl.BlockSpec((PK_HEADS, tm), lambda i: (0, i))],
        out_shape=[jax.ShapeDtypeStruct((PK_HEADS, N_KEYS, M), F32),
                   jax.ShapeDtypeStruct((PK_HEADS, N_KEYS, M), BF16),
                   jax.ShapeDtypeStruct((PK_HEADS, M), F32)],
        compiler_params=_cparams("parallel"), name="peer_prep",
    )(h2, wq, keys)


def _peer_dense_kernel(h_ref, u_ref, vt_ref, a_ref, b_ref, kap_ref, o_ref, acc_sc, *, sub):
    e = pl.program_id(1)
    tm = h_ref.shape[0]
    eb = u_ref.shape[0]
    ni = eb // N_KEYS

    @pl.when(e == 0)
    def _():
        acc_sc[...] = jnp.zeros(acc_sc.shape, F32)

    def rows16(x):
        x16 = jnp.broadcast_to(x, (2 * SUBLANES, sub)).astype(BF16)
        return jnp.concatenate([x16] * (N_KEYS // (2 * SUBLANES)), axis=0)

    def key_scores(t):
        return lax.dot_general(u_ref[...], h_ref[t * sub:(t + 1) * sub, :], NT_DIMS,
                               preferred_element_type=F32)

    def gates(t):
        tok = slice(t * sub, (t + 1) * sub)
        kaps = [rows16(kap_ref[h:h + 1, tok]) for h in range(PK_HEADS)]
        zero = jnp.zeros((N_KEYS, sub), BF16)
        ws = []
        for il in range(ni):
            w = zero
            for h in range(PK_HEADS):
                prod = b_ref[h, :, tok] * rows16(a_ref[h, il:il + 1, tok])
                w = w + jnp.where(prod >= kaps[h], prod, zero)
            ws.append(w)
        return ws

    n_slab = tm // sub
    sc_next = key_scores(0)
    w_next = gates(0)
    for t in range(n_slab):
        sc, ws = sc_next, w_next
        if t + 1 < n_slab:
            sc_next = key_scores(t + 1)
        was = []
        for il in range(ni):
            t_ = sc[il * N_KEYS:(il + 1) * N_KEYS, :]
            was.append(ws[il] * (t_ + t_ * lax.erf(t_)).astype(BF16))
        acc_sc[:, t * sub:(t + 1) * sub] += jnp.dot(vt_ref[...], jnp.concatenate(was, axis=0),
                                                    preferred_element_type=F32)
        if t + 1 < n_slab:
            w_next = gates(t + 1)

    @pl.when(e == pl.num_programs(1) - 1)
    def _():
        o_ref[...] = acc_sc[...].T


def _peer_dense(h2, u, vt, a, b, kap):
    M, D = h2.shape
    E = u.shape[0]
    tm = _pick(M, (1024, 512, 256, 128))
    sub = _pick(tm, (512, 256, 128))
    eb = SUBLANES * N_KEYS
    kern = functools.partial(_peer_dense_kernel, sub=sub)
    return pl.pallas_call(
        kern, grid=(M // tm, E // eb),
        in_specs=[pl.BlockSpec((tm, D), lambda i, e: (i, 0)),
                  pl.BlockSpec((eb, D), lambda i, e: (e, 0)),
                  pl.BlockSpec((D, eb), lambda i, e: (0, e)),
                  pl.BlockSpec((PK_HEADS, SUBLANES, tm), lambda i, e: (0, e, i)),
                  pl.BlockSpec((PK_HEADS, N_KEYS, tm), lambda i, e: (0, 0, i)),
                  pl.BlockSpec((PK_HEADS, tm), lambda i, e: (0, i))],
        out_specs=pl.BlockSpec((tm, D), lambda i, e: (i, 0)),
        out_shape=jax.ShapeDtypeStruct((M, D), F32),
        scratch_shapes=[pltpu.VMEM((D, tm), F32)],
        compiler_params=_cparams("parallel", "arbitrary"), name="peer_dense",
    )(h2, u, vt, a, b, kap)


def _rope_tables(n_ctx, S):
    n_freq = DA_DIM // 4
    n_rows = S // GRID_W
    row = jnp.repeat(jnp.arange(n_rows, dtype=F32), GRID_W)
    col = jnp.tile(jnp.arange(GRID_W, dtype=F32), n_rows)
    inv = ROPE_BASE ** (-jnp.arange(n_freq, dtype=F32) / n_freq)
    ang = jnp.concatenate([row[:, None] * inv, col[:, None] * inv], axis=-1)
    cos = jnp.concatenate([jnp.ones((n_ctx, DA_DIM // 2), F32), jnp.cos(ang)], axis=0)
    sin = jnp.concatenate([jnp.zeros((n_ctx, DA_DIM // 2), F32), jnp.sin(ang)], axis=0)
    return jnp.concatenate([cos] * 4, axis=1), jnp.concatenate([-sin, sin, -sin, sin], axis=1)


def _deinterleave_perm():
    idx = np.arange(DA_HEADS * 2 * DA_DIM).reshape(DA_HEADS * 2, DA_DIM // 2, 2)
    return np.concatenate([idx[:, :, 0], idx[:, :, 1]], axis=1).reshape(-1)


def kernel(x, c, ctx, c_ctx, w_ada, b_ada, norm1_g, norm2_g, w_in, da_lambda, da_subln, dn_conv, dn_a_log,
           dn_dt_bias, dn_norm, w_branch_a, w_branch_b, w_out, peer_wq, peer_keys, peer_u, peer_v, final_g):
    B, S, D = x.shape
    n_ctx = ctx.shape[1]
    T = n_ctx + S
    L = w_ada.shape[0]
    M = B * T
    assert B < SUBLANES and S % GRID_W == 0 and n_ctx % DN_CHUNK == 0 and S % DN_CHUNK == 0
    da_qk = DA_HEADS * 2 * DA_DIM
    da_w = DA_HEADS * DA_VDIM
    dn_qkv = 3 * DN_HEADS * DN_DK
    dn_w = DN_HEADS * DN_DV
    nh2 = 2 * DN_HEADS

    c_all = jnp.zeros((SUBLANES, D), F32).at[:B].set(c).at[B].set(c_ctx)
    mod = _ada(c_all, w_ada, b_ada).reshape(L, SUBLANES, 6, D)
    modsel = jnp.stack([jnp.broadcast_to(mod[:, B:B + 1], (L, B, 6, D)), mod[:, :B]], axis=2)

    cos_t, sin_t = _rope_tables(n_ctx, S)
    cos_m = jnp.tile(cos_t, (B, 1))
    sin_m = jnp.tile(sin_t, (B, 1))
    perm = _deinterleave_perm()

    xs = jnp.concatenate([ctx, x], axis=1)
    f = None
    for l in range(L):
        lam_init = 0.8 - 0.6 * math.exp(-0.3 * l)
        w = w_in[l]
        o = 0
        w_q = w[:, o:o + da_qk][:, perm]; o += da_qk
        w_k = w[:, o:o + da_qk][:, perm]; o += da_qk
        w_v = w[:, o:o + da_w]; o += da_w
        w_dn = w[:, o:o + dn_qkv]; o += dn_qkv
        w_z = w[:, o:o + dn_w]; o += dn_w
        w_b = w[:, o:o + nh2]; o += nh2
        w_a = w[:, o:o + nh2]; o += nh2
        w_ga = w[:, o:o + D]; o += D
        w_gb = w[:, o:o + D]; o += D
        w_qk = jnp.concatenate([w_q, w_k], axis=1).astype(BF16)
        w_rest = jnp.concatenate([w_dn, w_z, w_ga, w_gb], axis=1).astype(BF16)
        dn_col0, z_col0, ga_col0 = 0, dn_qkv, dn_qkv + dn_w
        w_ba_pad = jnp.zeros((D, LANES), F32).at[:, :nh2].set(w_b).at[:, nh2:2 * nh2].set(w_a).astype(BF16)
        gate_par = (jnp.zeros((SUBLANES, LANES), F32)
                    .at[0, nh2:2 * nh2].set(dn_a_log[l].reshape(-1))
                    .at[1, nh2:2 * nh2].set(dn_dt_bias[l].reshape(-1)))

        if l == 0:
            h = _modulate(xs, norm1_g[l], modsel[l], n_ctx)
        else:
            xs, h = _resid_modulate(xs, f, modsel[l - 1], norm1_g[l], modsel[l], n_ctx)
        hm = h.reshape(M, D)

        qk = _proj(hm, w_qk, out_dtype=BF16, kernel=_proj_rope_kernel, extra=(cos_m, sin_m),
                   extra_specs=lambda tm: [pl.BlockSpec((tm, LANES), lambda j, i: (i, 0))] * 2,
                   name="proj_qk_rope").reshape(B, T, 2 * da_qk)
        rest = _proj(hm, w_rest, out_dtype=BF16, extra_specs=lambda tm: [], name="proj_rest").reshape(B, T, -1)
        gates = _proj(hm, w_ba_pad, out_dtype=F32, kernel=_proj_gates_kernel, extra=(gate_par,),
                      extra_specs=lambda tm: [pl.BlockSpec((SUBLANES, LANES), lambda j, i: (0, 0))],
                      name="proj_gates").reshape(B, T, LANES)

        vt = _proj_t(hm, w_v.T.astype(BF16), out_dtype=BF16, name="proj_v_t")

        da = _attention(qk, vt, da_lambda[l], da_subln[l], n_ctx, lam_init)
        dnq = _dn_prep(rest, dn_conv[l], n_ctx, dn_col0)
        dn = _gdn(dnq, gates, rest, z_col0, dn_norm[l], n_ctx)

        xs, h2 = _merge(da, dn, rest, ga_col0, xs, w_branch_a[l].astype(BF16), w_branch_b[l].astype(BF16),
                        w_out[l].astype(BF16), norm2_g[l], modsel[l], n_ctx)
        h2m = h2.reshape(M, D)
        pa, pb, kap = _peer_prep(h2m, peer_wq[l].astype(BF16),
                                 peer_keys[l].reshape(2 * PK_HEADS, N_KEYS, PK_HALF).astype(BF16))
        f = _peer_dense(h2m, (peer_u[l] * GELU_FOLD).astype(BF16), peer_v[l].T.astype(BF16),
                        pa, pb, kap).reshape(B, T, D)

    return _final(xs, f, modsel[L - 1], final_g, n_ctx)
```

```python
import functools
import math

import jax
import jax.numpy as jnp
import numpy as np
from jax import lax
from jax.experimental import pallas as pl
from jax.experimental.pallas import tpu as pltpu

F32 = jnp.float32
BF16 = jnp.bfloat16
HIGHEST = lax.Precision.HIGHEST

GRID_W = 64
EPS = 1e-6
ROPE_BASE = 10000.0
DA_HEADS = 8
DA_DIM = 64
DA_VDIM = 2 * DA_DIM
DN_HEADS = 8
DN_DK = 128
DN_DV = 128
DN_CONV = 5
DN_CHUNK = 64
PK_HEADS = 8
N_KEYS = 128
PK_TOPK = 16
PK_HALF = 128
GELU_FOLD = 2.0 ** -0.5

LANES = 128
SUBLANES = 8
VMEM_LIMIT_BYTES = 56 * 1024 * 1024

NT_DIMS = (((1,), (1,)), ((), ()))


def _cparams(*sem, flags=None):
    return pltpu.CompilerParams(dimension_semantics=sem, vmem_limit_bytes=VMEM_LIMIT_BYTES, flags=flags)


def _pick(n, cands):
    for c in cands:
        if n % c == 0:
            return c
    raise ValueError(f"no tile in {cands} divides {n}")


def _silu(x):
    return x * jax.nn.sigmoid(x)


def _ada_kernel(c_ref, w_ref, b_ref, o_ref):
    sc = _silu(c_ref[...])
    o_ref[0] = jnp.dot(sc, w_ref[0], preferred_element_type=F32, precision=HIGHEST) + b_ref[0]


def _ada(c_all, w_ada, b_ada):
    L, D, N = w_ada.shape
    tn = _pick(N, (1536, 1024, 512, 256, 128))
    return pl.pallas_call(
        _ada_kernel,
        grid=(L, N // tn),
        in_specs=[pl.BlockSpec((SUBLANES, D), lambda l, j: (0, 0)),
                  pl.BlockSpec((1, D, tn), lambda l, j: (l, 0, j)),
                  pl.BlockSpec((1, 1, tn), lambda l, j: (l, 0, j))],
        out_specs=pl.BlockSpec((1, SUBLANES, tn), lambda l, j: (l, 0, j)),
        out_shape=jax.ShapeDtypeStruct((L, SUBLANES, N), F32),
        compiler_params=_cparams("parallel", "parallel"),
        name="ada",
    )(c_all, w_ada, b_ada.reshape(L, 1, N))


def _rms_mod(x, g, m, shift_row, scale_row):
    ms = jnp.mean(x * x, axis=-1, keepdims=True)
    y = x * lax.rsqrt(ms + EPS) * g
    return y * (1.0 + m[scale_row:scale_row + 1]) + m[shift_row:shift_row + 1]


def _modulate_kernel(x_ref, g_ref, m_ref, h_ref):
    h_ref[...] = _rms_mod(x_ref[...], g_ref[...], m_ref[...], 0, 1).astype(BF16)


def _resid_modulate_kernel(x_ref, f_ref, mp_ref, g_ref, m_ref, xo_ref, h_ref):
    x = x_ref[...] + mp_ref[...][5:6] * f_ref[...]
    xo_ref[...] = x
    h_ref[...] = _rms_mod(x, g_ref[...], m_ref[...], 0, 1).astype(BF16)


def _seg_specs(tm, n_ctx_tiles, D):
    tok = pl.BlockSpec((None, tm, D), lambda b, t: (b, t, 0))
    gain = pl.BlockSpec((1, D), lambda b, t: (0, 0))
    mod = pl.BlockSpec((None, None, 6, D), lambda b, t: (b, jnp.where(t >= n_ctx_tiles, 1, 0), 0, 0))
    return tok, gain, mod


def _modulate(xs, g, modsel, n_ctx):
    B, T, D = xs.shape
    tm = _pick(math.gcd(n_ctx, T - n_ctx), (256, 128))
    tok, gain, mod = _seg_specs(tm, n_ctx // tm, D)
    return pl.pallas_call(
        _modulate_kernel, grid=(B, T // tm),
        in_specs=[tok, gain, mod], out_specs=tok,
        out_shape=jax.ShapeDtypeStruct((B, T, D), BF16),
        compiler_params=_cparams("parallel", "parallel"), name="modulate",
    )(xs, g.reshape(1, D), modsel)


def _resid_modulate(xs, f, modsel_prev, g, modsel, n_ctx):
    B, T, D = xs.shape
    tm = _pick(math.gcd(n_ctx, T - n_ctx), (256, 128))
    tok, gain, mod = _seg_specs(tm, n_ctx // tm, D)
    return pl.pallas_call(
        _resid_modulate_kernel, grid=(B, T // tm),
        in_specs=[tok, tok, mod, gain, mod], out_specs=[tok, tok],
        out_shape=[jax.ShapeDtypeStruct((B, T, D), F32), jax.ShapeDtypeStruct((B, T, D), BF16)],
        compiler_params=_cparams("parallel", "parallel"), name="resid_modulate",
    )(xs, f, modsel_prev, g.reshape(1, D), modsel)


def _final_kernel(x_ref, f_ref, mp_ref, g_ref, o_ref):
    x = x_ref[...] + mp_ref[...][5:6] * f_ref[...]
    ms = jnp.mean(x * x, axis=-1, keepdims=True)
    o_ref[...] = x * lax.rsqrt(ms + EPS) * g_ref[...]


def _final(xs, f, modsel_prev, g, n_ctx):
    B, T, D = xs.shape
    S = T - n_ctx
    tm = _pick(math.gcd(n_ctx, S), (256, 128))
    off = n_ctx // tm
    tok_in = pl.BlockSpec((None, tm, D), lambda b, t: (b, t + off, 0))
    return pl.pallas_call(
        _final_kernel, grid=(B, S // tm),
        in_specs=[tok_in, tok_in,
                  pl.BlockSpec((None, None, 6, D), lambda b, t: (b, 1, 0, 0)),
                  pl.BlockSpec((1, D), lambda b, t: (0, 0))],
        out_specs=pl.BlockSpec((None, tm, D), lambda b, t: (b, t, 0)),
        out_shape=jax.ShapeDtypeStruct((B, S, D), F32),
        compiler_params=_cparams("parallel", "parallel"), name="final_norm",
    )(xs, f, modsel_prev, g.reshape(1, D))


def _proj_kernel(h_ref, w_ref, o_ref):
    o_ref[...] = jnp.dot(h_ref[...], w_ref[...], preferred_element_type=F32).astype(o_ref.dtype)


def _proj_rope_kernel(h_ref, w_ref, cos_ref, sin_ref, o_ref):
    acc = jnp.dot(h_ref[...], w_ref[...], preferred_element_type=F32)
    tn = acc.shape[1]
    half = DA_DIM // 2
    lane = lax.broadcasted_iota(jnp.int32, acc.shape, 1)
    partner = jnp.where((lane & (DA_DIM - 1)) < half,
                        pltpu.roll(acc, tn - half, axis=1),
                        pltpu.roll(acc, half, axis=1))
    reps = tn // LANES
    cos = jnp.concatenate([cos_ref[...]] * reps, axis=1)
    sin = jnp.concatenate([sin_ref[...]] * reps, axis=1)
    y = acc * cos + partner * sin
    scale = jnp.where(pl.program_id(0) == 0, DA_DIM ** -0.5 * math.log2(math.e), 1.0).astype(F32)
    o_ref[...] = (y * scale).astype(o_ref.dtype)


def _proj_gates_kernel(h_ref, w_ref, p_ref, o_ref):
    acc = jnp.dot(h_ref[...], w_ref[...], preferred_element_type=F32)
    p = p_ref[...]
    lane = lax.broadcasted_iota(jnp.int32, acc.shape, 1)
    beta = jax.nn.sigmoid(acc)
    g = -jnp.exp(p[0:1]) * jax.nn.softplus(acc + p[1:2])
    o_ref[...] = jnp.where(lane < 2 * DN_HEADS, beta, jnp.where(lane < 4 * DN_HEADS, g, 0.0))


def _proj(h, w, *, out_dtype, kernel=_proj_kernel, extra=(), extra_specs=(), name="proj"):
    M, D = h.shape
    N = w.shape[1]
    tm = _pick(M, (1024, 512, 256, 128))
    tn = _pick(N, (1024, 512, 256, 128))
    return pl.pallas_call(
        kernel, grid=(N // tn, M // tm),
        in_specs=[pl.BlockSpec((tm, D), lambda j, i: (i, 0)),
                  pl.BlockSpec((D, tn), lambda j, i: (0, j))] + list(extra_specs(tm)),
        out_specs=pl.BlockSpec((tm, tn), lambda j, i: (i, j)),
        out_shape=jax.ShapeDtypeStruct((M, N), out_dtype),
        compiler_params=_cparams("parallel", "parallel"), name=name,
    )(h, w, *extra)


def _proj_t_kernel(h_ref, wt_ref, o_ref):
    o_ref[...] = lax.dot_general(wt_ref[...], h_ref[...], NT_DIMS, preferred_element_type=F32).astype(o_ref.dtype)


def _proj_t(h, wt, *, out_dtype, name):
    M, D = h.shape
    N = wt.shape[0]
    tm = _pick(M, (1024, 512, 256, 128))
    tn = _pick(N, (1024, 512, 256, 128))
    return pl.pallas_call(
        _proj_t_kernel, grid=(N // tn, M // tm),
        in_specs=[pl.BlockSpec((tm, D), lambda j, i: (i, 0)),
                  pl.BlockSpec((tn, D), lambda j, i: (j, 0))],
        out_specs=pl.BlockSpec((tn, tm), lambda j, i: (j, i)),
        out_shape=jax.ShapeDtypeStruct((N, M), out_dtype),
        compiler_params=_cparams("parallel", "parallel"), name=name,
    )(h, wt)


ATTN_HEADS_PER_STEP = 2


def _attn_kernel(lam_ref, q_ref, k_ref, vt_ref, g_ref, o_ref, *, n_ctx, tk, lam_init):
    tq = q_ref.shape[0]
    T = k_ref.shape[0]
    qi = pl.program_id(2)
    heads = range(q_ref.shape[1] // DA_VDIM)
    cols = [slice(hh * DA_VDIM, (hh + 1) * DA_VDIM) for hh in heads]
    qqs = []
    for hh in heads:
        q = q_ref[:, cols[hh]]
        lane = lax.broadcasted_iota(jnp.int32, q.shape, 1)
        zero = jnp.zeros_like(q)
        qqs.append(jnp.concatenate([jnp.where(lane < DA_DIM, q, zero), jnp.where(lane >= DA_DIM, q, zero)], axis=0))

    def scores(hh, r0, rows):
        return lax.dot_general(k_ref[r0:r0 + rows, cols[hh]], qqs[hh], NT_DIMS,
                               preferred_element_type=F32)

    def step(hh, state, s, r0, rows):
        m_prev, l_prev, acc = state
        m_new = jnp.maximum(m_prev, jnp.max(s, axis=0, keepdims=True))
        alpha = jnp.exp2(m_prev - m_new)
        p = jnp.exp2(s - m_new)
        pv = jnp.dot(vt_ref[cols[hh], r0:r0 + rows], p.astype(BF16), preferred_element_type=F32)
        return m_new, alpha * l_prev + jnp.sum(p, axis=0, keepdims=True), alpha * acc + pv

    lam = lam_ref[...]
    lam_val = (jnp.exp(jnp.sum(lam[0:1] * lam[1:2], axis=-1, keepdims=True))
               - jnp.exp(jnp.sum(lam[2:3] * lam[3:4], axis=-1, keepdims=True)) + lam_init)

    def finish(hh, state):
        _, l_fin, acc = state
        o_all = acc / l_fin
        o = o_all[:, 0:tq] - lam_val * o_all[:, tq:2 * tq]
        ms = jnp.mean(o * o, axis=0, keepdims=True)
        y = o * lax.rsqrt(ms + EPS) * (1.0 - lam_init)
        o_ref[:, cols[hh]] = (y.T * g_ref[...]).astype(o_ref.dtype)

    init = (jnp.full((1, 2 * tq), -jnp.inf, F32), jnp.zeros((1, 2 * tq), F32), jnp.zeros((DA_VDIM, 2 * tq), F32))

    def attend(blocks):
        state = [init for _ in heads]
        s_next = [scores(hh, *blocks[0]) for hh in heads]
        for j, (r0, rows) in enumerate(blocks):
            for hh in heads:
                s_cur = s_next[hh]
                if j + 1 < len(blocks):
                    s_next[hh] = scores(hh, *blocks[j + 1])
                state[hh] = step(hh, state[hh], s_cur, r0, rows)
        for hh in heads:
            finish(hh, state[hh])

    @pl.when(qi * tq < n_ctx)
    def _():
        attend([(0, n_ctx)])

    @pl.when(qi * tq >= n_ctx)
    def _():
        attend([(0, n_ctx)] + [(n_ctx + j * tk, tk) for j in range((T - n_ctx) // tk)])


def _attention(qk, vt, lam, subln, n_ctx, lam_init):
    B, T, _ = qk.shape
    H = DA_HEADS
    S = T - n_ctx
    tq = _pick(math.gcd(n_ctx, S), (256, 128))
    tk = _pick(S, (1024, 512, 256, 128))
    hps = ATTN_HEADS_PER_STEP
    nb = H // hps
    wblk = hps * DA_VDIM
    kern = functools.partial(_attn_kernel, n_ctx=n_ctx, tk=tk, lam_init=lam_init)
    return pl.pallas_call(
        kern, grid=(B, nb, T // tq),
        in_specs=[pl.BlockSpec((4, DA_DIM), lambda b, h, i: (0, 0)),
                  pl.BlockSpec((None, tq, wblk), lambda b, h, i: (b, i, h)),
                  pl.BlockSpec((None, T, wblk), lambda b, h, i: (b, 0, nb + h)),
                  pl.BlockSpec((wblk, T), lambda b, h, i: (h, b)),
                  pl.BlockSpec((1, DA_VDIM), lambda b, h, i: (0, 0))],
        out_specs=pl.BlockSpec((None, tq, wblk), lambda b, h, i: (b, i, h)),
        out_shape=jax.ShapeDtypeStruct((B, T, H * DA_VDIM), BF16),
        compiler_params=_cparams("parallel", "parallel", "arbitrary"), name="diff_attention",
    )(lam, qk, qk, vt, subln.reshape(1, DA_VDIM))


def _dn_prep_kernel(x_ref, w_ref, o_ref, pad_sc, *, n_ctx, rows):
    T = x_ref.shape[0]
    halo = SUBLANES
    cb = pl.program_id(1)
    zpad = jnp.zeros((halo, LANES), F32)
    pad_sc[0:halo, :] = zpad
    pad_sc[halo + n_ctx:2 * halo + n_ctx, :] = zpad
    pad_sc[2 * halo + T:3 * halo + T, :] = zpad
    pad_sc[halo:halo + n_ctx, :] = x_ref[0:n_ctx, :].astype(F32)
    pad_sc[2 * halo + n_ctx:2 * halo + T, :] = x_ref[n_ctx:T, :].astype(F32)
    w = w_ref[...]
    is_qk = cb < 2 * DN_HEADS
    post = jnp.where(cb < DN_HEADS, DN_DK ** -0.5, 1.0).astype(F32)
    pad = DN_CONV // 2

    def body(c, carry):
        r0 = pl.multiple_of(c * rows, rows)
        base = pl.multiple_of(r0 + jnp.where(r0 >= n_ctx, 2 * halo, halo), halo)
        y = jnp.zeros((rows, LANES), F32)
        for j in range(DN_CONV):
            y = y + pad_sc[pl.ds(base + (j - pad), rows), :] * w[j:j + 1]
        y = _silu(y)
        yn = y * lax.rsqrt(jnp.sum(y * y, axis=-1, keepdims=True) + EPS) * post
        o_ref[pl.ds(r0, rows), :] = jnp.where(is_qk, yn, y).astype(o_ref.dtype)
        return carry

    lax.fori_loop(0, T // rows, body, 0)


def _dn_prep(rest, conv_w, n_ctx, col0):
    B, T, _ = rest.shape
    nblk = 3 * DN_HEADS
    rows = _pick(math.gcd(n_ctx, T - n_ctx), (256, 128))
    cw = jnp.zeros((SUBLANES, nblk * LANES), F32).at[:DN_CONV].set(conv_w)
    kern = functools.partial(_dn_prep_kernel, n_ctx=n_ctx, rows=rows)
    return pl.pallas_call(
        kern, grid=(B, nblk),
        in_specs=[pl.BlockSpec((None, T, LANES), lambda b, c: (b, 0, col0 // LANES + c)),
                  pl.BlockSpec((SUBLANES, LANES), lambda b, c: (0, c))],
        out_specs=pl.BlockSpec((None, T, LANES), lambda b, c: (b, 0, c)),
        out_shape=jax.ShapeDtypeStruct((B, T, nblk * LANES), BF16),
        scratch_shapes=[pltpu.VMEM((T + 3 * SUBLANES, LANES), F32)],
        compiler_params=_cparams("parallel", "parallel"), name="dn_prep",
    )(rest, cw)


GDN_HEADS_PER_STEP = 2
GDN_CHUNKS_PER_ITER = 4
GDN_INV_PASSES = 1


def _split_dot(a, b, passes):
    a_hi = a.astype(BF16)
    b_hi = b.astype(BF16)
    out = jnp.dot(a_hi, b_hi, preferred_element_type=F32)
    if passes >= 3:
        a_lo = (a - a_hi.astype(F32)).astype(BF16)
        b_lo = (b - b_hi.astype(F32)).astype(BF16)
        out = out + jnp.dot(a_hi, b_lo, preferred_element_type=F32) + jnp.dot(a_lo, b_hi, preferred_element_type=F32)
    return out


def _gdn_kernel(q_ref, k_ref, v_ref, gt_ref, z_ref, ng_ref, o_ref,
                w_sc, qg_sc, kt_sc, a_sc, u_sc, egl_sc, oacc_sc, s_sc, *, n_ctx):
    C = DN_CHUNK
    C2 = 2 * C
    T = q_ref.shape[0]
    nc = T // C
    ncc = n_ctx // C
    hp = pl.program_id(1)

    row = lax.broadcasted_iota(jnp.int32, (C2, C2), 0)
    col = lax.broadcasted_iota(jnp.int32, (C2, C2), 1)
    fwd_row = row < C
    rc_xor = row ^ col
    ahead = (col - row) * jnp.where(fwd_row, 1, -1)
    same_dir = rc_xor < C
    incl = same_dir & (ahead <= 0)
    strict = same_dir & (ahead < 0)
    eye = (row == col).astype(F32)
    mcs = jnp.where(incl, 1.0, 0.0).astype(BF16)

    def p1_load(c, s):
        r0 = pl.multiple_of(c * C, C)
        head = hp * GDN_HEADS_PER_STEP + s
        lo, hi = s * LANES, (s + 1) * LANES
        k = k_ref[pl.ds(r0, C), lo:hi]
        q = q_ref[pl.ds(r0, C), lo:hi]
        v = v_ref[pl.ds(r0, C), lo:hi]
        kk = jnp.concatenate([k, k], axis=0)
        qq = jnp.concatenate([q, q], axis=0)
        vf = jnp.concatenate([v, v], axis=0).astype(F32)
        x = gt_ref[pl.ds(r0, C), :]
        x2 = jnp.concatenate([x, x], axis=0)
        bsel = jnp.where(fwd_row, head, DN_HEADS + head)
        gsel = bsel + 2 * DN_HEADS
        beta = jnp.sum(jnp.where(col == bsel, x2, 0.0), axis=-1, keepdims=True)
        glog = jnp.sum(jnp.where(col == gsel, x2, 0.0), axis=-1, keepdims=True)
        return dict(c=c, s=s, kk=kk, qq=qq, vf=vf, beta=beta, glog=glog)

    def p1_prep(d):
        g_rem = jnp.broadcast_to(d["glog"], (C2, C2))
        gcum = jnp.zeros((C2, C2), F32)
        for _ in range(3):
            piece = g_rem.astype(BF16)
            gcum = gcum + jnp.dot(mcs, piece, preferred_element_type=F32)
            g_rem = g_rem - piece.astype(F32)
        decay = jnp.exp(jnp.where(incl, gcum - gcum.T, -jnp.inf))
        kkt = lax.dot_general(d["kk"], d["kk"], NT_DIMS, preferred_element_type=F32)
        qkt = lax.dot_general(d["qq"], d["kk"], NT_DIMS, preferred_element_type=F32)
        d.update(gcum=gcum, decay=decay, qkt=qkt, lmat=jnp.where(strict, d["beta"] * kkt * decay, 0.0), tinv=eye)

    def p1_finish(d):
        c, s, gcum, beta = d["c"], d["s"], d["gcum"], d["beta"]
        kf = d["kk"].astype(F32)
        eg = jnp.exp(gcum)
        rhs = jnp.concatenate([(d["vf"] * beta).astype(BF16), (kf * beta * eg).astype(BF16)], axis=1)
        uw = jnp.dot(d["tinv"].astype(BF16), rhs, preferred_element_type=F32)
        glast = jnp.where(fwd_row, gcum[C - 1:C, :], gcum[C:C + 1, :])
        ktail = kf * jnp.exp(glast - gcum)
        u_sc[s, c] = uw[:, 0:LANES]
        w_sc[s, c] = uw[:, LANES:2 * LANES].astype(BF16)
        qg_sc[s, c] = (d["qq"].astype(F32) * eg).astype(BF16)
        a_sc[s, c] = (d["qkt"] * d["decay"]).astype(BF16)
        kt_sc[s, c] = ktail.T.astype(BF16)
        egl_sc[s, c] = jnp.exp(jnp.concatenate([jnp.broadcast_to(gcum[C - 1:C, :], (4, LANES)),
                                                jnp.broadcast_to(gcum[C:C + 1, :], (4, LANES))], axis=0))

    def p1_body(i, carry):
        probs = [p1_load(i * GDN_CHUNKS_PER_ITER + j, s)
                 for j in range(GDN_CHUNKS_PER_ITER) for s in range(GDN_HEADS_PER_STEP)]
        for d in probs:
            p1_prep(d)
        for lvl in range(int(math.log2(C))):
            for d in probs:
                d["x"] = _split_dot(d["tinv"], jnp.where((rc_xor >> lvl) == 1, d["lmat"], 0.0), GDN_INV_PASSES)
            for d in probs:
                d["tinv"] = d["tinv"] - _split_dot(d["x"], d["tinv"], GDN_INV_PASSES)
        for d in probs:
            p1_finish(d)
        return carry

    lax.fori_loop(0, nc // GDN_CHUNKS_PER_ITER, p1_body, 0)

    oacc_sc[...] = jnp.zeros(oacc_sc.shape, F32)
    s_sc[...] = jnp.zeros(s_sc.shape, F32)
    lane_b = lax.broadcasted_iota(jnp.int32, (LANES, LANES), 1)
    zpad = jnp.zeros((C, LANES), BF16)

    def p2_body(i, carry):
        cf = i
        cb = jnp.where(i < ncc, ncc - 1 - i, nc - 1 + ncc - i)
        rf = pl.multiple_of(cf * C, C)
        rb = pl.multiple_of(cb * C, C)
        heads = range(GDN_HEADS_PER_STEP)
        st = [s_sc[s] for s in heads]
        lhs1 = [jnp.concatenate([
            jnp.concatenate([w_sc[s, cf, 0:C, :], zpad], axis=1),
            jnp.concatenate([zpad, w_sc[s, cb, C:C2, :]], axis=1),
            jnp.concatenate([qg_sc[s, cf, 0:C, :], zpad], axis=1),
            jnp.concatenate([zpad, qg_sc[s, cb, C:C2, :]], axis=1)], axis=0) for s in heads]
        zk = jnp.zeros((LANES, C2), BF16)
        lhs2 = [jnp.concatenate([a_sc[s, cf, 0:C, :], a_sc[s, cb, C:C2, :],
                                 jnp.where(lane_b < C, kt_sc[s, cf], zk),
                                 jnp.where(lane_b >= C, kt_sc[s, cb], zk)], axis=0) for s in heads]
        u = [jnp.concatenate([u_sc[s, cf, 0:C, :], u_sc[s, cb, C:C2, :]], axis=0) for s in heads]
        scale = [jnp.concatenate([jnp.broadcast_to(egl_sc[s, cf][0:1], (DN_DK, DN_DV)),
                                  jnp.broadcast_to(egl_sc[s, cb][4:5], (DN_DK, DN_DV))], axis=0) for s in heads]
        r1 = [jnp.dot(lhs1[s], st[s].astype(BF16), preferred_element_type=F32) for s in heads]
        vnew = [(u[s] - r1[s][0:C2]).astype(BF16) for s in heads]
        r2 = [jnp.dot(lhs2[s], vnew[s], preferred_element_type=F32) for s in heads]
        for s in heads:
            s_sc[s] = st[s] * scale[s] + r2[s][C2:C2 + 2 * DN_DK]
        for s in heads:
            oacc_sc[s, pl.ds(rf, C), :] += r1[s][C2:C2 + C] + r2[s][0:C]
            oacc_sc[s, pl.ds(rb, C), :] += r1[s][C2 + C:2 * C2] + r2[s][C:C2]
        return carry

    lax.fori_loop(0, nc, p2_body, 0)

    ng = ng_ref[...]
    rows = _pick(T, (256, 128))

    def fin_body(c, carry):
        r0 = pl.multiple_of(c * rows, rows)
        for s in range(GDN_HEADS_PER_STEP):
            o = oacc_sc[s, pl.ds(r0, rows), :]
            zf = z_ref[pl.ds(r0, rows), s * LANES:(s + 1) * LANES].astype(F32)
            ms = jnp.mean(o * o, axis=-1, keepdims=True)
            o_ref[pl.ds(r0, rows), s * LANES:(s + 1) * LANES] = (o * lax.rsqrt(ms + EPS) * ng * _silu(zf)).astype(o_ref.dtype)
        return carry

    lax.fori_loop(0, T // rows, fin_body, 0)


def _gdn(dnq, gates, rest, z_col0, norm_g, n_ctx):
    B, T, _ = dnq.shape
    H = DN_HEADS
    hps = GDN_HEADS_PER_STEP
    wblk = hps * LANES
    nc = T // DN_CHUNK
    assert nc % GDN_CHUNKS_PER_ITER == 0
    C2 = 2 * DN_CHUNK
    kern = functools.partial(_gdn_kernel, n_ctx=n_ctx)
    nb = H // hps
    once = dict(pipeline_mode=pl.Buffered(1))
    return pl.pallas_call(
        kern, grid=(B, nb),
        in_specs=[pl.BlockSpec((None, T, wblk), lambda b, h: (b, 0, h), **once),
                  pl.BlockSpec((None, T, wblk), lambda b, h: (b, 0, nb + h), **once),
                  pl.BlockSpec((None, T, wblk), lambda b, h: (b, 0, 2 * nb + h), **once),
                  pl.BlockSpec((None, T, LANES), lambda b, h: (b, 0, 0), **once),
                  pl.BlockSpec((None, T, wblk), lambda b, h: (b, 0, z_col0 // wblk + h), **once),
                  pl.BlockSpec((1, DN_DV), lambda b, h: (0, 0))],
        out_specs=pl.BlockSpec((None, T, wblk), lambda b, h: (b, 0, h)),
        out_shape=jax.ShapeDtypeStruct((B, T, H * DN_DV), BF16),
        scratch_shapes=[pltpu.VMEM((hps, nc, C2, LANES), BF16),
                        pltpu.VMEM((hps, nc, C2, LANES), BF16),
                        pltpu.VMEM((hps, nc, LANES, C2), BF16),
                        pltpu.VMEM((hps, nc, C2, C2), BF16),
                        pltpu.VMEM((hps, nc, C2, LANES), F32),
                        pltpu.VMEM((hps, nc, SUBLANES, LANES), F32),
                        pltpu.VMEM((hps, T, LANES), F32),
                        pltpu.VMEM((hps, 2 * DN_DK, DN_DV), F32)],
        compiler_params=_cparams("parallel", "arbitrary"), name="gated_deltanet",
    )(dnq, dnq, dnq, gates, rest, norm_g.reshape(1, DN_DV))


def _merge_kernel(da_ref, dn_ref, ga_ref, gb_ref, x_ref, wa_ref, wb_ref, wo_ref, g_ref, m_ref, xo_ref, h_ref):
    ya = jnp.dot(da_ref[...], wa_ref[...], preferred_element_type=F32)
    yb = jnp.dot(dn_ref[...], wb_ref[...], preferred_element_type=F32)
    y = jax.nn.sigmoid(ga_ref[...].astype(F32)) * ya + jax.nn.sigmoid(gb_ref[...].astype(F32)) * yb
    y2 = jnp.dot(y.astype(BF16), wo_ref[...], preferred_element_type=F32)
    m = m_ref[...]
    x = x_ref[...] + m[2:3] * y2
    xo_ref[...] = x
    h_ref[...] = _rms_mod(x, g_ref[...], m, 3, 4).astype(BF16)


def _merge(da, dn, rest, ga_col0, xs, w_ba, w_bb, w_o, g2, modsel, n_ctx):
    B, T, D = xs.shape
    tm = _pick(math.gcd(n_ctx, T - n_ctx), (256, 128))
    tok, gain, mod = _seg_specs(tm, n_ctx // tm, D)
    wspec = pl.BlockSpec((D, D), lambda b, t: (0, 0))
    ga_blk = ga_col0 // D
    return pl.pallas_call(
        _merge_kernel, grid=(B, T // tm),
        in_specs=[tok, tok,
                  pl.BlockSpec((None, tm, D), lambda b, t: (b, t, ga_blk)),
                  pl.BlockSpec((None, tm, D), lambda b, t: (b, t, ga_blk + 1)),
                  tok, wspec, wspec, wspec, gain, mod],
        out_specs=[tok, tok],
        out_shape=[jax.ShapeDtypeStruct((B, T, D), F32), jax.ShapeDtypeStruct((B, T, D), BF16)],
        compiler_params=_cparams("parallel", "parallel"), name="merge",
    )(da, dn, rest, rest, xs, w_ba, w_bb, w_o, g2.reshape(1, D), modsel)


def _merge_sort_pairs(n):
    pairs = []
    p = 1
    while p < n:
        k = p
        while k >= 1:
            for j in range(k % p, n - k, 2 * k):
                for i in range(min(k, n - j - k)):
                    if (i + j) // (2 * p) == (i + j + k) // (2 * p):
                        pairs.append((i + j, i + j + k))
            k //= 2
        p *= 2
    return pairs


def _desc_tops(x, n):
    ng = x.shape[0] // SUBLANES
    g = [x[v * SUBLANES:(v + 1) * SUBLANES] for v in range(ng)]
    for i, j in _merge_sort_pairs(pl.next_power_of_2(ng)):
        if j < ng:
            g[i], g[j] = jnp.maximum(g[i], g[j]), jnp.minimum(g[i], g[j])
    neg = jnp.full_like(g[0], -jnp.inf)
    tops = []
    for r in range(n):
        m = jnp.max(g[0], axis=0, keepdims=True)
        tops.append(m)
        hit = g[0] == m
        for v in range(min(ng, n - r)):
            g[v] = jnp.where(hit, g[v + 1] if v + 1 < ng else neg, g[v])
    return tops


def _peer_prep_kernel(h_ref, wq_ref, keys_ref, a_ref, b_ref, kap_ref):
    tm = h_ref.shape[0]
    q = jnp.dot(h_ref[...], wq_ref[...], preferred_element_type=F32).astype(BF16)
    for h in range(PK_HEADS):
        st = [lax.dot_general(keys_ref[2 * h + p], q[:, (2 * h + p) * PK_HALF:(2 * h + p + 1) * PK_HALF],
                              NT_DIMS, preferred_element_type=F32) for p in range(2)]
        ta = _desc_tops(st[0], PK_TOPK + 1)
        tb = _desc_tops(st[1], PK_TOPK + 1)
        tbs = jnp.concatenate(tb[:PK_TOPK], axis=0)
        tb8 = tbs[0:SUBLANES]
        row8 = lax.broadcasted_iota(jnp.int32, tb8.shape, 0)
        pieces = [ta[0] + tbs, ta[1] + tb8]
        for r in range(2, SUBLANES):
            pieces.append(jnp.where(row8 < PK_TOPK // (r + 1), ta[r] + tb8, -jnp.inf))
        pieces.append(jnp.concatenate(ta[SUBLANES:PK_TOPK], axis=0) + tb[0])
        cand = jnp.concatenate(pieces, axis=0)
        best = _desc_tops(cand, PK_TOPK + 1)
        mx = best[0]
        zsum = best[0] * 0.0
        for r in range(PK_TOPK):
            zsum = zsum + jnp.exp(best[r] - mx)
        nxt = jnp.maximum(best[PK_TOPK], jnp.maximum(ta[PK_TOPK] + tb[0], ta[0] + tb[PK_TOPK]))
        thr = 0.5 * (best[PK_TOPK - 1] + nxt)
        rz = GELU_FOLD / zsum
        a_ref[h] = jnp.exp(st[0] - ta[0])
        b_ref[h] = (jnp.exp(st[1] - tb[0]) * rz).astype(b_ref.dtype)
        kap_ref[h:h + 1, :] = jnp.exp(thr - mx) * rz


def _peer_prep(h2, wq, keys):
    M, D = h2.shape
    tm = _pick(M, (256, 128))
    nk = 2 * PK_HEADS
    return pl.pallas_call(
        _peer_prep_kernel, grid=(M // tm,),
        in_specs=[pl.BlockSpec((tm, D), lambda i: (i, 0)),
                  pl.BlockSpec((D, nk * PK_HALF), lambda i: (0, 0)),
                  pl.BlockSpec((nk, N_KEYS, PK_HALF), lambda i: (0, 0, 0))],
        out_specs=[pl.BlockSpec((PK_HEADS, N_KEYS, tm), lambda i: (0, 0, i)),
                   pl.BlockSpec((PK_HEADS, N_KEYS, tm), lambda i: (0, 0, i)),
                   pl.BlockSpec((PK_HEADS, tm), lambda i: (0, i))],
        out_shape=[jax.ShapeDtypeStruct((PK_HEADS, N_KEYS, M), F32),
                   jax.ShapeDtypeStruct((PK_HEADS, N_KEYS, M), BF16),
                   jax.ShapeDtypeStruct((PK_HEADS, M), F32)],
        compiler_params=_cparams("parallel"), name="peer_prep",
    )(h2, wq, keys)


def _peer_dense_kernel(h_ref, u_ref, vt_ref, a_ref, b_ref, kap_ref, o_ref, acc_sc, *, sub):
    e = pl.program_id(1)
    tm = h_ref.shape[0]
    eb = u_ref.shape[0]
    ni = eb // N_KEYS

    @pl.when(e == 0)
    def _():
        acc_sc[...] = jnp.zeros(acc_sc.shape, F32)

    def rows16(x):
        x16 = jnp.broadcast_to(x, (2 * SUBLANES, sub)).astype(BF16)
        return jnp.concatenate([x16] * (N_KEYS // (2 * SUBLANES)), axis=0)

    def key_scores(t):
        return lax.dot_general(u_ref[...], h_ref[t * sub:(t + 1) * sub, :], NT_DIMS,
                               preferred_element_type=F32)

    def gates(t):
        tok = slice(t * sub, (t + 1) * sub)
        kaps = [rows16(kap_ref[h:h + 1, tok]) for h in range(PK_HEADS)]
        zero = jnp.zeros((N_KEYS, sub), BF16)
        ws = []
        for il in range(ni):
            w = zero
            for h in range(PK_HEADS):
                prod = b_ref[h, :, tok] * rows16(a_ref[h, il:il + 1, tok])
                w = w + jnp.where(prod >= kaps[h], prod, zero)
            ws.append(w)
        return ws

    n_slab = tm // sub
    sc_next = key_scores(0)
    w_next = gates(0)
    for t in range(n_slab):
        sc, ws = sc_next, w_next
        if t + 1 < n_slab:
            sc_next = key_scores(t + 1)
        was = []
        for il in range(ni):
            t_ = sc[il * N_KEYS:(il + 1) * N_KEYS, :]
            was.append(ws[il] * (t_ + t_ * lax.erf(t_)).astype(BF16))
        acc_sc[:, t * sub:(t + 1) * sub] += jnp.dot(vt_ref[...], jnp.concatenate(was, axis=0),
                                                    preferred_element_type=F32)
        if t + 1 < n_slab:
            w_next = gates(t + 1)

    @pl.when(e == pl.num_programs(1) - 1)
    def _():
        o_ref[...] = acc_sc[...].T


def _peer_dense(h2, u, vt, a, b, kap):
    M, D = h2.shape
    E = u.shape[0]
    tm = _pick(M, (1024, 512, 256, 128))
    sub = _pick(tm, (512, 256, 128))
    eb = SUBLANES * N_KEYS
    kern = functools.partial(_peer_dense_kernel, sub=sub)
    return pl.pallas_call(
        kern, grid=(M // tm, E // eb),
        in_specs=[pl.BlockSpec((tm, D), lambda i, e: (i, 0)),
                  pl.BlockSpec((eb, D), lambda i, e: (e, 0)),
                  pl.BlockSpec((D, eb), lambda i, e: (0, e)),
                  pl.BlockSpec((PK_HEADS, SUBLANES, tm), lambda i, e: (0, e, i)),
                  pl.BlockSpec((PK_HEADS, N_KEYS, tm), lambda i, e: (0, 0, i)),
                  pl.BlockSpec((PK_HEADS, tm), lambda i, e: (0, i))],
        out_specs=pl.BlockSpec((tm, D), lambda i, e: (i, 0)),
        out_shape=jax.ShapeDtypeStruct((M, D), F32),
        scratch_shapes=[pltpu.VMEM((D, tm), F32)],
        compiler_params=_cparams("parallel", "arbitrary"), name="peer_dense",
    )(h2, u, vt, a, b, kap)


def _rope_tables(n_ctx, S):
    n_freq = DA_DIM // 4
    n_rows = S // GRID_W
    row = jnp.repeat(jnp.arange(n_rows, dtype=F32), GRID_W)
    col = jnp.tile(jnp.arange(GRID_W, dtype=F32), n_rows)
    inv = ROPE_BASE ** (-jnp.arange(n_freq, dtype=F32) / n_freq)
    ang = jnp.concatenate([row[:, None] * inv, col[:, None] * inv], axis=-1)
    cos = jnp.concatenate([jnp.ones((n_ctx, DA_DIM // 2), F32), jnp.cos(ang)], axis=0)
    sin = jnp.concatenate([jnp.zeros((n_ctx, DA_DIM // 2), F32), jnp.sin(ang)], axis=0)
    return jnp.concatenate([cos] * 4, axis=1), jnp.concatenate([-sin, sin, -sin, sin], axis=1)


def _deinterleave_perm():
    idx = np.arange(DA_HEADS * 2 * DA_DIM).reshape(DA_HEADS * 2, DA_DIM // 2, 2)
    return np.concatenate([idx[:, :, 0], idx[:, :, 1]], axis=1).reshape(-1)


def kernel(x, c, ctx, c_ctx, w_ada, b_ada, norm1_g, norm2_g, w_in, da_lambda, da_subln, dn_conv, dn_a_log,
           dn_dt_bias, dn_norm, w_branch_a, w_branch_b, w_out, peer_wq, peer_keys, peer_u, peer_v, final_g):
    B, S, D = x.shape
    n_ctx = ctx.shape[1]
    T = n_ctx + S
    L = w_ada.shape[0]
    M = B * T
    assert B < SUBLANES and S % GRID_W == 0 and n_ctx % DN_CHUNK == 0 and S % DN_CHUNK == 0
    da_qk = DA_HEADS * 2 * DA_DIM
    da_w = DA_HEADS * DA_VDIM
    dn_qkv = 3 * DN_HEADS * DN_DK
    dn_w = DN_HEADS * DN_DV
    nh2 = 2 * DN_HEADS

    c_all = jnp.zeros((SUBLANES, D), F32).at[:B].set(c).at[B].set(c_ctx)
    mod = _ada(c_all, w_ada, b_ada).reshape(L, SUBLANES, 6, D)
    modsel = jnp.stack([jnp.broadcast_to(mod[:, B:B + 1], (L, B, 6, D)), mod[:, :B]], axis=2)

    cos_t, sin_t = _rope_tables(n_ctx, S)
    cos_m = jnp.tile(cos_t, (B, 1))
    sin_m = jnp.tile(sin_t, (B, 1))
    perm = _deinterleave_perm()

    xs = jnp.concatenate([ctx, x], axis=1)
    f = None
    for l in range(L):
        lam_init = 0.8 - 0.6 * math.exp(-0.3 * l)
        w = w_in[l]
        o = 0
        w_q = w[:, o:o + da_qk][:, perm]; o += da_qk
        w_k = w[:, o:o + da_qk][:, perm]; o += da_qk
        w_v = w[:, o:o + da_w]; o += da_w
        w_dn = w[:, o:o + dn_qkv]; o += dn_qkv
        w_z = w[:, o:o + dn_w]; o += dn_w
        w_b = w[:, o:o + nh2]; o += nh2
        w_a = w[:, o:o + nh2]; o += nh2
        w_ga = w[:, o:o + D]; o += D
        w_gb = w[:, o:o + D]; o += D
        w_qk = jnp.concatenate([w_q, w_k], axis=1).astype(BF16)
        w_rest = jnp.concatenate([w_dn, w_z, w_ga, w_gb], axis=1).astype(BF16)
        dn_col0, z_col0, ga_col0 = 0, dn_qkv, dn_qkv + dn_w
        w_ba_pad = jnp.zeros((D, LANES), F32).at[:, :nh2].set(w_b).at[:, nh2:2 * nh2].set(w_a).astype(BF16)
        gate_par = (jnp.zeros((SUBLANES, LANES), F32)
                    .at[0, nh2:2 * nh2].set(dn_a_log[l].reshape(-1))
                    .at[1, nh2:2 * nh2].set(dn_dt_bias[l].reshape(-1)))

        if l == 0:
            h = _modulate(xs, norm1_g[l], modsel[l], n_ctx)
        else:
            xs, h = _resid_modulate(xs, f, modsel[l - 1], norm1_g[l], modsel[l], n_ctx)
        hm = h.reshape(M, D)

        qk = _proj(hm, w_qk, out_dtype=BF16, kernel=_proj_rope_kernel, extra=(cos_m, sin_m),
                   extra_specs=lambda tm: [pl.BlockSpec((tm, LANES), lambda j, i: (i, 0))] * 2,
                   name="proj_qk_rope").reshape(B, T, 2 * da_qk)
        rest = _proj(hm, w_rest, out_dtype=BF16, extra_specs=lambda tm: [], name="proj_rest").reshape(B, T, -1)
        gates = _proj(hm, w_ba_pad, out_dtype=F32, kernel=_proj_gates_kernel, extra=(gate_par,),
                      extra_specs=lambda tm: [pl.BlockSpec((SUBLANES, LANES), lambda j, i: (0, 0))],
                      name="proj_gates").reshape(B, T, LANES)

        vt = _proj_t(hm, w_v.T.astype(BF16), out_dtype=BF16, name="proj_v_t")

        da = _attention(qk, vt, da_lambda[l], da_subln[l], n_ctx, lam_init)
        dnq = _dn_prep(rest, dn_conv[l], n_ctx, dn_col0)
        dn = _gdn(dnq, gates, rest, z_col0, dn_norm[l], n_ctx)

        xs, h2 = _merge(da, dn, rest, ga_col0, xs, w_branch_a[l].astype(BF16), w_branch_b[l].astype(BF16),
                        w_out[l].astype(BF16), norm2_g[l], modsel[l], n_ctx)
        h2m = h2.reshape(M, D)
        pa, pb, kap = _peer_prep(h2m, peer_wq[l].astype(BF16),
                                 peer_keys[l].reshape(2 * PK_HEADS, N_KEYS, PK_HALF).astype(BF16))
        f = _peer_dense(h2m, (peer_u[l] * GELU_FOLD).astype(BF16), peer_v[l].T.astype(BF16),
                        pa, pb, kap).reshape(B, T, D)

    return _final(xs, f, modsel[L - 1], final_g, n_ctx)
```

```python
import functools
import math

import jax
import jax.numpy as jnp
import numpy as np
from jax import lax
from jax.experimental import pallas as pl
from jax.experimental.pallas import tpu as pltpu

F32 = jnp.float32
BF16 = jnp.bfloat16
HIGHEST = lax.Precision.HIGHEST

GRID_W = 64
EPS = 1e-6
ROPE_BASE = 10000.0
DA_HEADS = 8
DA_DIM = 64
DA_VDIM = 2 * DA_DIM
DN_HEADS = 8
DN_DK = 128
DN_DV = 128
DN_CONV = 5
DN_CHUNK = 64
PK_HEADS = 8
N_KEYS = 128
PK_TOPK = 16
PK_HALF = 128
GELU_FOLD = 2.0 ** -0.5

LANES = 128
SUBLANES = 8
VMEM_LIMIT_BYTES = 56 * 1024 * 1024

NT_DIMS = (((1,), (1,)), ((), ()))


def _cparams(*sem, flags=None):
    return pltpu.CompilerParams(dimension_semantics=sem, vmem_limit_bytes=VMEM_LIMIT_BYTES, flags=flags)


def _pick(n, cands):
    for c in cands:
        if n % c == 0:
            return c
    raise ValueError(f"no tile in {cands} divides {n}")


def _silu(x):
    return x * jax.nn.sigmoid(x)


def _ada_kernel(c_ref, w_ref, b_ref, o_ref):
    sc = _silu(c_ref[...])
    o_ref[0] = jnp.dot(sc, w_ref[0], preferred_element_type=F32, precision=HIGHEST) + b_ref[0]


def _ada(c_all, w_ada, b_ada):
    L, D, N = w_ada.shape
    tn = _pick(N, (1536, 1024, 512, 256, 128))
    return pl.pallas_call(
        _ada_kernel,
        grid=(L, N // tn),
        in_specs=[pl.BlockSpec((SUBLANES, D), lambda l, j: (0, 0)),
                  pl.BlockSpec((1, D, tn), lambda l, j: (l, 0, j)),
                  pl.BlockSpec((1, 1, tn), lambda l, j: (l, 0, j))],
        out_specs=pl.BlockSpec((1, SUBLANES, tn), lambda l, j: (l, 0, j)),
        out_shape=jax.ShapeDtypeStruct((L, SUBLANES, N), F32),
        compiler_params=_cparams("parallel", "parallel"),
        name="ada",
    )(c_all, w_ada, b_ada.reshape(L, 1, N))


def _rms_mod(x, g, m, shift_row, scale_row):
    ms = jnp.mean(x * x, axis=-1, keepdims=True)
    y = x * lax.rsqrt(ms + EPS) * g
    return y * (1.0 + m[scale_row:scale_row + 1]) + m[shift_row:shift_row + 1]


def _modulate_kernel(x_ref, g_ref, m_ref, h_ref):
    h_ref[...] = _rms_mod(x_ref[...], g_ref[...], m_ref[...], 0, 1).astype(BF16)


def _resid_modulate_kernel(x_ref, f_ref, mp_ref, g_ref, m_ref, xo_ref, h_ref):
    x = x_ref[...] + mp_ref[...][5:6] * f_ref[...]
    xo_ref[...] = x
    h_ref[...] = _rms_mod(x, g_ref[...], m_ref[...], 0, 1).astype(BF16)


def _seg_specs(tm, n_ctx_tiles, D):
    tok = pl.BlockSpec((None, tm, D), lambda b, t: (b, t, 0))
    gain = pl.BlockSpec((1, D), lambda b, t: (0, 0))
    mod = pl.BlockSpec((None, None, 6, D), lambda b, t: (b, jnp.where(t >= n_ctx_tiles, 1, 0), 0, 0))
    return tok, gain, mod


def _modulate(xs, g, modsel, n_ctx):
    B, T, D = xs.shape
    tm = _pick(math.gcd(n_ctx, T - n_ctx), (256, 128))
    tok, gain, mod = _seg_specs(tm, n_ctx // tm, D)
    return pl.pallas_call(
        _modulate_kernel, grid=(B, T // tm),
        in_specs=[tok, gain, mod], out_specs=tok,
        out_shape=jax.ShapeDtypeStruct((B, T, D), BF16),
        compiler_params=_cparams("parallel", "parallel"), name="modulate",
    )(xs, g.reshape(1, D), modsel)


def _resid_modulate(xs, f, modsel_prev, g, modsel, n_ctx):
    B, T, D = xs.shape
    tm = _pick(math.gcd(n_ctx, T - n_ctx), (256, 128))
    tok, gain, mod = _seg_specs(tm, n_ctx // tm, D)
    return pl.pallas_call(
        _resid_modulate_kernel, grid=(B, T // tm),
        in_specs=[tok, tok, mod, gain, mod], out_specs=[tok, tok],
        out_shape=[jax.ShapeDtypeStruct((B, T, D), F32), jax.ShapeDtypeStruct((B, T, D), BF16)],
        compiler_params=_cparams("parallel", "parallel"), name="resid_modulate",
    )(xs, f, modsel_prev, g.reshape(1, D), modsel)


def _final_kernel(x_ref, f_ref, mp_ref, g_ref, o_ref):
    x = x_ref[...] + mp_ref[...][5:6] * f_ref[...]
    ms = jnp.mean(x * x, axis=-1, keepdims=True)
    o_ref[...] = x * lax.rsqrt(ms + EPS) * g_ref[...]


def _final(xs, f, modsel_prev, g, n_ctx):
    B, T, D = xs.shape
    S = T - n_ctx
    tm = _pick(math.gcd(n_ctx, S), (256, 128))
    off = n_ctx // tm
    tok_in = pl.BlockSpec((None, tm, D), lambda b, t: (b, t + off, 0))
    return pl.pallas_call(
        _final_kernel, grid=(B, S // tm),
        in_specs=[tok_in, tok_in,
                  pl.BlockSpec((None, None, 6, D), lambda b, t: (b, 1, 0, 0)),
                  pl.BlockSpec((1, D), lambda b, t: (0, 0))],
        out_specs=pl.BlockSpec((None, tm, D), lambda b, t: (b, t, 0)),
        out_shape=jax.ShapeDtypeStruct((B, S, D), F32),
        compiler_params=_cparams("parallel", "parallel"), name="final_norm",
    )(xs, f, modsel_prev, g.reshape(1, D))


def _proj_kernel(h_ref, w_ref, o_ref):
    o_ref[...] = jnp.dot(h_ref[...], w_ref[...], preferred_element_type=F32).astype(o_ref.dtype)


def _proj_rope_kernel(h_ref, w_ref, cos_ref, sin_ref, o_ref):
    acc = jnp.dot(h_ref[...], w_ref[...], preferred_element_type=F32)
    tn = acc.shape[1]
    half = DA_DIM // 2
    lane = lax.broadcasted_iota(jnp.int32, acc.shape, 1)
    partner = jnp.where((lane & (DA_DIM - 1)) < half,
                        pltpu.roll(acc, tn - half, axis=1),
                        pltpu.roll(acc, half, axis=1))
    reps = tn // LANES
    cos = jnp.concatenate([cos_ref[...]] * reps, axis=1)
    sin = jnp.concatenate([sin_ref[...]] * reps, axis=1)
    y = acc * cos + partner * sin
    scale = jnp.where(pl.program_id(0) == 0, DA_DIM ** -0.5 * math.log2(math.e), 1.0).astype(F32)
    o_ref[...] = (y * scale).astype(o_ref.dtype)


def _proj_gates_kernel(h_ref, w_ref, p_ref, o_ref):
    acc = jnp.dot(h_ref[...], w_ref[...], preferred_element_type=F32)
    p = p_ref[...]
    lane = lax.broadcasted_iota(jnp.int32, acc.shape, 1)
    beta = jax.nn.sigmoid(acc)
    g = -jnp.exp(p[0:1]) * jax.nn.softplus(acc + p[1:2])
    o_ref[...] = jnp.where(lane < 2 * DN_HEADS, beta, jnp.where(lane < 4 * DN_HEADS, g, 0.0))


def _proj(h, w, *, out_dtype, kernel=_proj_kernel, extra=(), extra_specs=(), name="proj"):
    M, D = h.shape
    N = w.shape[1]
    tm = _pick(M, (1024, 512, 256, 128))
    tn = _pick(N, (1024, 512, 256, 128))
    return pl.pallas_call(
        kernel, grid=(N // tn, M // tm),
        in_specs=[pl.BlockSpec((tm, D), lambda j, i: (i, 0)),
                  pl.BlockSpec((D, tn), lambda j, i: (0, j))] + list(extra_specs(tm)),
        out_specs=pl.BlockSpec((tm, tn), lambda j, i: (i, j)),
        out_shape=jax.ShapeDtypeStruct((M, N), out_dtype),
        compiler_params=_cparams("parallel", "parallel"), name=name,
    )(h, w, *extra)


def _proj_t_kernel(h_ref, wt_ref, o_ref):
    o_ref[...] = lax.dot_general(wt_ref[...], h_ref[...], NT_DIMS, preferred_element_type=F32).astype(o_ref.dtype)


def _proj_t(h, wt, *, out_dtype, name):
    M, D = h.shape
    N = wt.shape[0]
    tm = _pick(M, (1024, 512, 256, 128))
    tn = _pick(N, (1024, 512, 256, 128))
    return pl.pallas_call(
        _proj_t_kernel, grid=(N // tn, M // tm),
        in_specs=[pl.BlockSpec((tm, D), lambda j, i: (i, 0)),
                  pl.BlockSpec((tn, D), lambda j, i: (j, 0))],
        out_specs=pl.BlockSpec((tn, tm), lambda j, i: (j, i)),
        out_shape=jax.ShapeDtypeStruct((N, M), out_dtype),
        compiler_params=_cparams("parallel", "parallel"), name=name,
    )(h, wt)


ATTN_HEADS_PER_STEP = 2


def _attn_kernel(lam_ref, q_ref, k_ref, vt_ref, g_ref, o_ref, *, n_ctx, tk, lam_init):
    tq = q_ref.shape[0]
    T = k_ref.shape[0]
    qi = pl.program_id(2)
    heads = range(q_ref.shape[1] // DA_VDIM)
    cols = [slice(hh * DA_VDIM, (hh + 1) * DA_VDIM) for hh in heads]
    qqs = []
    for hh in heads:
        q = q_ref[:, cols[hh]]
        lane = lax.broadcasted_iota(jnp.int32, q.shape, 1)
        zero = jnp.zeros_like(q)
        qqs.append(jnp.concatenate([jnp.where(lane < DA_DIM, q, zero), jnp.where(lane >= DA_DIM, q, zero)], axis=0))

    def scores(hh, r0, rows):
        return lax.dot_general(k_ref[r0:r0 + rows, cols[hh]], qqs[hh], NT_DIMS,
                               preferred_element_type=F32)

    def step(hh, state, s, r0, rows):
        m_prev, l_prev, acc = state
        m_new = jnp.maximum(m_prev, jnp.max(s, axis=0, keepdims=True))
        alpha = jnp.exp2(m_prev - m_new)
        p = jnp.exp2(s - m_new)
        pv = jnp.dot(vt_ref[cols[hh], r0:r0 + rows], p.astype(BF16), preferred_element_type=F32)
        return m_new, alpha * l_prev + jnp.sum(p, axis=0, keepdims=True), alpha * acc + pv

    lam = lam_ref[...]
    lam_val = (jnp.exp(jnp.sum(lam[0:1] * lam[1:2], axis=-1, keepdims=True))
               - jnp.exp(jnp.sum(lam[2:3] * lam[3:4], axis=-1, keepdims=True)) + lam_init)

    def finish(hh, state):
        _, l_fin, acc = state
        o_all = acc / l_fin
        o = o_all[:, 0:tq] - lam_val * o_all[:, tq:2 * tq]
        ms = jnp.mean(o * o, axis=0, keepdims=True)
        y = o * lax.rsqrt(ms + EPS) * (1.0 - lam_init)
        o_ref[:, cols[hh]] = (y.T * g_ref[...]).astype(o_ref.dtype)

    init = (jnp.full((1, 2 * tq), -jnp.inf, F32), jnp.zeros((1, 2 * tq), F32), jnp.zeros((DA_VDIM, 2 * tq), F32))

    def attend(blocks):
        state = [init for _ in heads]
        s_next = [scores(hh, *blocks[0]) for hh in heads]
        for j, (r0, rows) in enumerate(blocks):
            for hh in heads:
                s_cur = s_next[hh]
                if j + 1 < len(blocks):
                    s_next[hh] = scores(hh, *blocks[j + 1])
                state[hh] = step(hh, state[hh], s_cur, r0, rows)
        for hh in heads:
            finish(hh, state[hh])

    @pl.when(qi * tq < n_ctx)
    def _():
        attend([(0, n_ctx)])

    @pl.when(qi * tq >= n_ctx)
    def _():
        attend([(0, n_ctx)] + [(n_ctx + j * tk, tk) for j in range((T - n_ctx) // tk)])


def _attention(qk, vt, lam, subln, n_ctx, lam_init):
    B, T, _ = qk.shape
    H = DA_HEADS
    S = T - n_ctx
    tq = _pick(math.gcd(n_ctx, S), (256, 128))
    tk = _pick(S, (1024, 512, 256, 128))
    hps = ATTN_HEADS_PER_STEP
    nb = H // hps
    wblk = hps * DA_VDIM
    kern = functools.partial(_attn_kernel, n_ctx=n_ctx, tk=tk, lam_init=lam_init)
    return pl.pallas_call(
        kern, grid=(B, nb, T // tq),
        in_specs=[pl.BlockSpec((4, DA_DIM), lambda b, h, i: (0, 0)),
                  pl.BlockSpec((None, tq, wblk), lambda b, h, i: (b, i, h)),
                  pl.BlockSpec((None, T, wblk), lambda b, h, i: (b, 0, nb + h)),
                  pl.BlockSpec((wblk, T), lambda b, h, i: (h, b)),
                  pl.BlockSpec((1, DA_VDIM), lambda b, h, i: (0, 0))],
        out_specs=pl.BlockSpec((None, tq, wblk), lambda b, h, i: (b, i, h)),
        out_shape=jax.ShapeDtypeStruct((B, T, H * DA_VDIM), BF16),
        compiler_params=_cparams("parallel", "parallel", "arbitrary"), name="diff_attention",
    )(lam, qk, qk, vt, subln.reshape(1, DA_VDIM))


def _dn_prep_kernel(x_ref, w_ref, o_ref, pad_sc, *, n_ctx, rows):
    T = x_ref.shape[0]
    halo = SUBLANES
    cb = pl.program_id(1)
    zpad = jnp.zeros((halo, LANES), F32)
    pad_sc[0:halo, :] = zpad
    pad_sc[halo + n_ctx:2 * halo + n_ctx, :] = zpad
    pad_sc[2 * halo + T:3 * halo + T, :] = zpad
    pad_sc[halo:halo + n_ctx, :] = x_ref[0:n_ctx, :].astype(F32)
    pad_sc[2 * halo + n_ctx:2 * halo + T, :] = x_ref[n_ctx:T, :].astype(F32)
    w = w_ref[...]
    is_qk = cb < 2 * DN_HEADS
    post = jnp.where(cb < DN_HEADS, DN_DK ** -0.5, 1.0).astype(F32)
    pad = DN_CONV // 2

    for c in range(T // rows):
        r0 = c * rows
        base = r0 + (2 * halo if r0 >= n_ctx else halo)
        y = jnp.zeros((rows, LANES), F32)
        for j in range(DN_CONV):
            y = y + pad_sc[base + (j - pad):base + (j - pad) + rows, :] * w[j:j + 1]
        y = _silu(y)
        yn = y * lax.rsqrt(jnp.sum(y * y, axis=-1, keepdims=True) + EPS) * post
        o_ref[r0:r0 + rows, :] = jnp.where(is_qk, yn, y).astype(o_ref.dtype)


def _dn_prep(rest, conv_w, n_ctx, col0):
    B, T, _ = rest.shape
    nblk = 3 * DN_HEADS
    rows = _pick(math.gcd(n_ctx, T - n_ctx), (256, 128))
    cw = jnp.zeros((SUBLANES, nblk * LANES), F32).at[:DN_CONV].set(conv_w)
    kern = functools.partial(_dn_prep_kernel, n_ctx=n_ctx, rows=rows)
    return pl.pallas_call(
        kern, grid=(B, nblk),
        in_specs=[pl.BlockSpec((None, T, LANES), lambda b, c: (b, 0, col0 // LANES + c)),
                  pl.BlockSpec((SUBLANES, LANES), lambda b, c: (0, c))],
        out_specs=pl.BlockSpec((None, T, LANES), lambda b, c: (b, 0, c)),
        out_shape=jax.ShapeDtypeStruct((B, T, nblk * LANES), BF16),
        scratch_shapes=[pltpu.VMEM((T + 3 * SUBLANES, LANES), F32)],
        compiler_params=_cparams("parallel", "parallel"), name="dn_prep",
    )(rest, cw)


GDN_HEADS_PER_STEP = 2
GDN_CHUNKS_PER_ITER = 4
GDN_INV_PASSES = 1


def _split_dot(a, b, passes):
    a_hi = a.astype(BF16)
    b_hi = b.astype(BF16)
    out = jnp.dot(a_hi, b_hi, preferred_element_type=F32)
    if passes >= 3:
        a_lo = (a - a_hi.astype(F32)).astype(BF16)
        b_lo = (b - b_hi.astype(F32)).astype(BF16)
        out = out + jnp.dot(a_hi, b_lo, preferred_element_type=F32) + jnp.dot(a_lo, b_hi, preferred_element_type=F32)
    return out


def _gdn_kernel(q_ref, k_ref, v_ref, gt_ref, z_ref, ng_ref, o_ref,
                w_sc, qg_sc, kt_sc, a_sc, u_sc, egl_sc, oacc_sc, s_sc, *, n_ctx):
    C = DN_CHUNK
    C2 = 2 * C
    T = q_ref.shape[0]
    nc = T // C
    ncc = n_ctx // C
    hp = pl.program_id(1)

    row = lax.broadcasted_iota(jnp.int32, (C2, C2), 0)
    col = lax.broadcasted_iota(jnp.int32, (C2, C2), 1)
    fwd_row = row < C
    rc_xor = row ^ col
    ahead = (col - row) * jnp.where(fwd_row, 1, -1)
    same_dir = rc_xor < C
    incl = same_dir & (ahead <= 0)
    strict = same_dir & (ahead < 0)
    eye = (row == col).astype(F32)
    mcs = jnp.where(incl, 1.0, 0.0).astype(BF16)

    def p1_load(c, s):
        r0 = pl.multiple_of(c * C, C)
        head = hp * GDN_HEADS_PER_STEP + s
        lo, hi = s * LANES, (s + 1) * LANES
        k = k_ref[pl.ds(r0, C), lo:hi]
        q = q_ref[pl.ds(r0, C), lo:hi]
        v = v_ref[pl.ds(r0, C), lo:hi]
        kk = jnp.concatenate([k, k], axis=0)
        qq = jnp.concatenate([q, q], axis=0)
        vf = jnp.concatenate([v, v], axis=0).astype(F32)
        x = gt_ref[pl.ds(r0, C), :]
        x2 = jnp.concatenate([x, x], axis=0)
        bsel = jnp.where(fwd_row, head, DN_HEADS + head)
        gsel = bsel + 2 * DN_HEADS
        beta = jnp.sum(jnp.where(col == bsel, x2, 0.0), axis=-1, keepdims=True)
        glog = jnp.sum(jnp.where(col == gsel, x2, 0.0), axis=-1, keepdims=True)
        return dict(c=c, s=s, kk=kk, qq=qq, vf=vf, beta=beta, glog=glog)

    def p1_prep(d):
        g_rem = jnp.broadcast_to(d["glog"], (C2, C2))
        gcum = jnp.zeros((C2, C2), F32)
        for _ in range(3):
            piece = g_rem.astype(BF16)
            gcum = gcum + jnp.dot(mcs, piece, preferred_element_type=F32)
            g_rem = g_rem - piece.astype(F32)
        decay = jnp.exp(jnp.where(incl, gcum - gcum.T, -jnp.inf))
        kkt = lax.dot_general(d["kk"], d["kk"], NT_DIMS, preferred_element_type=F32)
        qkt = lax.dot_general(d["qq"], d["kk"], NT_DIMS, preferred_element_type=F32)
        d.update(gcum=gcum, decay=decay, qkt=qkt, lmat=jnp.where(strict, d["beta"] * kkt * decay, 0.0), tinv=eye)

    def p1_finish(d):
        c, s, gcum, beta = d["c"], d["s"], d["gcum"], d["beta"]
        kf = d["kk"].astype(F32)
        eg = jnp.exp(gcum)
        rhs = jnp.concatenate([(d["vf"] * beta).astype(BF16), (kf * beta * eg).astype(BF16)], axis=1)
        uw = jnp.dot(d["tinv"].astype(BF16), rhs, preferred_element_type=F32)
        glast = jnp.where(fwd_row, gcum[C - 1:C, :], gcum[C:C + 1, :])
        ktail = kf * jnp.exp(glast - gcum)
        u_sc[s, c] = uw[:, 0:LANES]
        w_sc[s, c] = uw[:, LANES:2 * LANES].astype(BF16)
        qg_sc[s, c] = (d["qq"].astype(F32) * eg).astype(BF16)
        a_sc[s, c] = (d["qkt"] * d["decay"]).astype(BF16)
        kt_sc[s, c] = ktail.T.astype(BF16)
        egl_sc[s, c] = jnp.exp(jnp.concatenate([jnp.broadcast_to(gcum[C - 1:C, :], (4, LANES)),
                                                jnp.broadcast_to(gcum[C:C + 1, :], (4, LANES))], axis=0))

    def p1_body(i, carry):
        probs = [p1_load(i * GDN_CHUNKS_PER_ITER + j, s)
                 for j in range(GDN_CHUNKS_PER_ITER) for s in range(GDN_HEADS_PER_STEP)]
        for d in probs:
            p1_prep(d)
        for lvl in range(int(math.log2(C))):
            for d in probs:
                d["x"] = _split_dot(d["tinv"], jnp.where((rc_xor >> lvl) == 1, d["lmat"], 0.0), GDN_INV_PASSES)
            for d in probs:
                d["tinv"] = d["tinv"] - _split_dot(d["x"], d["tinv"], GDN_INV_PASSES)
        for d in probs:
            p1_finish(d)
        return carry

    lax.fori_loop(0, nc // GDN_CHUNKS_PER_ITER, p1_body, 0)

    oacc_sc[...] = jnp.zeros(oacc_sc.shape, F32)
    s_sc[...] = jnp.zeros(s_sc.shape, F32)
    lane_b = lax.broadcasted_iota(jnp.int32, (LANES, LANES), 1)
    zpad = jnp.zeros((C, LANES), BF16)

    def p2_body(i, carry):
        cf = i
        cb = jnp.where(i < ncc, ncc - 1 - i, nc - 1 + ncc - i)
        rf = pl.multiple_of(cf * C, C)
        rb = pl.multiple_of(cb * C, C)
        heads = range(GDN_HEADS_PER_STEP)
        st = [s_sc[s] for s in heads]
        lhs1 = [jnp.concatenate([
            jnp.concatenate([w_sc[s, cf, 0:C, :], zpad], axis=1),
            jnp.concatenate([zpad, w_sc[s, cb, C:C2, :]], axis=1),
            jnp.concatenate([qg_sc[s, cf, 0:C, :], zpad], axis=1),
            jnp.concatenate([zpad, qg_sc[s, cb, C:C2, :]], axis=1)], axis=0) for s in heads]
        zk = jnp.zeros((LANES, C2), BF16)
        lhs2 = [jnp.concatenate([a_sc[s, cf, 0:C, :], a_sc[s, cb, C:C2, :],
                                 jnp.where(lane_b < C, kt_sc[s, cf], zk),
                                 jnp.where(lane_b >= C, kt_sc[s, cb], zk)], axis=0) for s in heads]
        u = [jnp.concatenate([u_sc[s, cf, 0:C, :], u_sc[s, cb, C:C2, :]], axis=0) for s in heads]
        scale = [jnp.concatenate([jnp.broadcast_to(egl_sc[s, cf][0:1], (DN_DK, DN_DV)),
                                  jnp.broadcast_to(egl_sc[s, cb][4:5], (DN_DK, DN_DV))], axis=0) for s in heads]
        r1 = [jnp.dot(lhs1[s], st[s].astype(BF16), preferred_element_type=F32) for s in heads]
        vnew = [(u[s] - r1[s][0:C2]).astype(BF16) for s in heads]
        r2 = [jnp.dot(lhs2[s], vnew[s], preferred_element_type=F32) for s in heads]
        for s in heads:
            s_sc[s] = st[s] * scale[s] + r2[s][C2:C2 + 2 * DN_DK]
        for s in heads:
            oacc_sc[s, pl.ds(rf, C), :] += r1[s][C2:C2 + C] + r2[s][0:C]
            oacc_sc[s, pl.ds(rb, C), :] += r1[s][C2 + C:2 * C2] + r2[s][C:C2]
        return carry

    lax.fori_loop(0, nc, p2_body, 0)

    ng = ng_ref[...]
    rows = _pick(T, (256, 128))

    def fin_body(c, carry):
        r0 = pl.multiple_of(c * rows, rows)
        for s in range(GDN_HEADS_PER_STEP):
            o = oacc_sc[s, pl.ds(r0, rows), :]
            zf = z_ref[pl.ds(r0, rows), s * LANES:(s + 1) * LANES].astype(F32)
            ms = jnp.mean(o * o, axis=-1, keepdims=True)
            o_ref[pl.ds(r0, rows), s * LANES:(s + 1) * LANES] = (o * lax.rsqrt(ms + EPS) * ng * _silu(zf)).astype(o_ref.dtype)
        return carry

    lax.fori_loop(0, T // rows, fin_body, 0)


def _gdn(dnq, gates, rest, z_col0, norm_g, n_ctx):
    B, T, _ = dnq.shape
    H = DN_HEADS
    hps = GDN_HEADS_PER_STEP
    wblk = hps * LANES
    nc = T // DN_CHUNK
    assert nc % GDN_CHUNKS_PER_ITER == 0
    C2 = 2 * DN_CHUNK
    kern = functools.partial(_gdn_kernel, n_ctx=n_ctx)
    nb = H // hps
    once = dict(pipeline_mode=pl.Buffered(1))
    return pl.pallas_call(
        kern, grid=(B, nb),
        in_specs=[pl.BlockSpec((None, T, wblk), lambda b, h: (b, 0, h), **once),
                  pl.BlockSpec((None, T, wblk), lambda b, h: (b, 0, nb + h), **once),
                  pl.BlockSpec((None, T, wblk), lambda b, h: (b, 0, 2 * nb + h), **once),
                  pl.BlockSpec((None, T, LANES), lambda b, h: (b, 0, 0), **once),
                  pl.BlockSpec((None, T, wblk), lambda b, h: (b, 0, z_col0 // wblk + h), **once),
                  pl.BlockSpec((1, DN_DV), lambda b, h: (0, 0))],
        out_specs=pl.BlockSpec((None, T, wblk), lambda b, h: (b, 0, h)),
        out_shape=jax.ShapeDtypeStruct((B, T, H * DN_DV), BF16),
        scratch_shapes=[pltpu.VMEM((hps, nc, C2, LANES), BF16),
                        pltpu.VMEM((hps, nc, C2, LANES), BF16),
                        pltpu.VMEM((hps, nc, LANES, C2), BF16),
                        pltpu.VMEM((hps, nc, C2, C2), BF16),
                        pltpu.VMEM((hps, nc, C2, LANES), F32),
                        pltpu.VMEM((hps, nc, SUBLANES, LANES), F32),
                        pltpu.VMEM((hps, T, LANES), F32),
                        pltpu.VMEM((hps, 2 * DN_DK, DN_DV), F32)],
        compiler_params=_cparams("parallel", "arbitrary"), name="gated_deltanet",
    )(dnq, dnq, dnq, gates, rest, norm_g.reshape(1, DN_DV))


def _merge_kernel(da_ref, dn_ref, ga_ref, gb_ref, x_ref, wa_ref, wb_ref, wo_ref, g_ref, m_ref, xo_ref, h_ref):
    ya = jnp.dot(da_ref[...], wa_ref[...], preferred_element_type=F32)
    yb = jnp.dot(dn_ref[...], wb_ref[...], preferred_element_type=F32)
    y = jax.nn.sigmoid(ga_ref[...].astype(F32)) * ya + jax.nn.sigmoid(gb_ref[...].astype(F32)) * yb
    y2 = jnp.dot(y.astype(BF16), wo_ref[...], preferred_element_type=F32)
    m = m_ref[...]
    x = x_ref[...] + m[2:3] * y2
    xo_ref[...] = x
    h_ref[...] = _rms_mod(x, g_ref[...], m, 3, 4).astype(BF16)


def _merge(da, dn, rest, ga_col0, xs, w_ba, w_bb, w_o, g2, modsel, n_ctx):
    B, T, D = xs.shape
    tm = _pick(math.gcd(n_ctx, T - n_ctx), (256, 128))
    tok, gain, mod = _seg_specs(tm, n_ctx // tm, D)
    wspec = pl.BlockSpec((D, D), lambda b, t: (0, 0))
    ga_blk = ga_col0 // D
    return pl.pallas_call(
        _merge_kernel, grid=(B, T // tm),
        in_specs=[tok, tok,
                  pl.BlockSpec((None, tm, D), lambda b, t: (b, t, ga_blk)),
                  pl.BlockSpec((None, tm, D), lambda b, t: (b, t, ga_blk + 1)),
                  tok, wspec, wspec, wspec, gain, mod],
        out_specs=[tok, tok],
        out_shape=[jax.ShapeDtypeStruct((B, T, D), F32), jax.ShapeDtypeStruct((B, T, D), BF16)],
        compiler_params=_cparams("parallel", "parallel"), name="merge",
    )(da, dn, rest, rest, xs, w_ba, w_bb, w_o, g2.reshape(1, D), modsel)


def _merge_sort_pairs(n):
    pairs = []
    p = 1
    while p < n:
        k = p
        while k >= 1:
            for j in range(k % p, n - k, 2 * k):
                for i in range(min(k, n - j - k)):
                    if (i + j) // (2 * p) == (i + j + k) // (2 * p):
                        pairs.append((i + j, i + j + k))
            k //= 2
        p *= 2
    return pairs


def _desc_tops(x, n):
    ng = x.shape[0] // SUBLANES
    g = [x[v * SUBLANES:(v + 1) * SUBLANES] for v in range(ng)]
    for i, j in _merge_sort_pairs(pl.next_power_of_2(ng)):
        if j < ng:
            g[i], g[j] = jnp.maximum(g[i], g[j]), jnp.minimum(g[i], g[j])
    neg = jnp.full_like(g[0], -jnp.inf)
    tops = []
    for r in range(n):
        m = jnp.max(g[0], axis=0, keepdims=True)
        tops.append(m)
        hit = g[0] == m
        for v in range(min(ng, n - r)):
            g[v] = jnp.where(hit, g[v + 1] if v + 1 < ng else neg, g[v])
    return tops


def _peer_prep_kernel(h_ref, wq_ref, keys_ref, a_ref, b_ref, kap_ref):
    tm = h_ref.shape[0]
    q = jnp.dot(h_ref[...], wq_ref[...], preferred_element_type=F32).astype(BF16)
    for h in range(PK_HEADS):
        st = [lax.dot_general(keys_ref[2 * h + p], q[:, (2 * h + p) * PK_HALF:(2 * h + p + 1) * PK_HALF],
                              NT_DIMS, preferred_element_type=F32) for p in range(2)]
        ta = _desc_tops(st[0], PK_TOPK + 1)
        tb = _desc_tops(st[1], PK_TOPK + 1)
        tbs = jnp.concatenate(tb[:PK_TOPK], axis=0)
        tb8 = tbs[0:SUBLANES]
        row8 = lax.broadcasted_iota(jnp.int32, tb8.shape, 0)
        pieces = [ta[0] + tbs, ta[1] + tb8]
        for r in range(2, SUBLANES):
            pieces.append(jnp.where(row8 < PK_TOPK // (r + 1), ta[r] + tb8, -jnp.inf))
        pieces.append(jnp.concatenate(ta[SUBLANES:PK_TOPK], axis=0) + tb[0])
        cand = jnp.concatenate(pieces, axis=0)
        best = _desc_tops(cand, PK_TOPK + 1)
        mx = best[0]
        zsum = best[0] * 0.0
        for r in range(PK_TOPK):
            zsum = zsum + jnp.exp(best[r] - mx)
        nxt = jnp.maximum(best[PK_TOPK], jnp.maximum(ta[PK_TOPK] + tb[0], ta[0] + tb[PK_TOPK]))
        thr = 0.5 * (best[PK_TOPK - 1] + nxt)
        rz = GELU_FOLD / zsum
        a_ref[h] = jnp.exp(st[0] - ta[0])
        b_ref[h] = (jnp.exp(st[1] - tb[0]) * rz).astype(b_ref.dtype)
        kap_ref[h:h + 1, :] = jnp.exp(thr - mx) * rz


def _peer_prep(h2, wq, keys):
    M, D = h2.shape
    tm = _pick(M, (256, 128))
    nk = 2 * PK_HEADS
    return pl.pallas_call(
        _peer_prep_kernel, grid=(M // tm,),
        in_specs=[pl.BlockSpec((tm, D), lambda i: (i, 0)),
                  pl.BlockSpec((D, nk * PK_HALF), lambda i: (0, 0)),
                  pl.BlockSpec((nk, N_KEYS, PK_HALF), lambda i: (0, 0, 0))],
        out_specs=[pl.BlockSpec((PK_HEADS, N_KEYS, tm), lambda i: (0, 0, i)),
                   pl.BlockSpec((PK_HEADS, N_KEYS, tm), lambda i: (0, 0, i)),
                   pl.BlockSpec((PK_HEADS, tm), lambda i: (0, i))],
        out_shape=[jax.ShapeDtypeStruct((PK_HEADS, N_KEYS, M), F32),
                   jax.ShapeDtypeStruct((PK_HEADS, N_KEYS, M), BF16),
                   jax.ShapeDtypeStruct((PK_HEADS, M), F32)],
        compiler_params=_cparams("parallel"), name="peer_prep",
    )(h2, wq, keys)


PEER_KEY_ROWS_PER_STEP = 8


def _peer_dense_kernel(h_ref, u_ref, vt_ref, a_ref, b_ref, kap_ref, o_ref, acc_sc, *, sub):
    e = pl.program_id(1)
    tm = h_ref.shape[0]
    eb = u_ref.shape[0]
    ni = eb // N_KEYS

    @pl.when(e == 0)
    def _():
        acc_sc[...] = jnp.zeros(acc_sc.shape, F32)

    def rows16(x):
        x16 = jnp.broadcast_to(x, (2 * SUBLANES, sub)).astype(BF16)
        return jnp.concatenate([x16] * (N_KEYS // (2 * SUBLANES)), axis=0)

    def key_scores(t):
        return lax.dot_general(u_ref[...], h_ref[t * sub:(t + 1) * sub, :], NT_DIMS,
                               preferred_element_type=F32)

    def gates(t):
        tok = slice(t * sub, (t + 1) * sub)
        kaps = [rows16(kap_ref[h:h + 1, tok]) for h in range(PK_HEADS)]
        zero = jnp.zeros((N_KEYS, sub), BF16)
        ws = []
        for il in range(ni):
            w = zero
            for h in range(PK_HEADS):
                prod = b_ref[h, :, tok] * rows16(a_ref[h, il:il + 1, tok])
                w = w + jnp.where(prod >= kaps[h], prod, zero)
            ws.append(w)
        return ws

    n_slab = tm // sub
    sc_next = key_scores(0)
    w_next = gates(0)
    for t in range(n_slab):
        sc, ws = sc_next, w_next
        if t + 1 < n_slab:
            sc_next = key_scores(t + 1)
        was = []
        for il in range(ni):
            t_ = sc[il * N_KEYS:(il + 1) * N_KEYS, :]
            was.append(ws[il] * (t_ + t_ * lax.erf(t_)).astype(BF16))
        acc_sc[:, t * sub:(t + 1) * sub] += jnp.dot(vt_ref[...], jnp.concatenate(was, axis=0),
                                                    preferred_element_type=F32)
        if t + 1 < n_slab:
            w_next = gates(t + 1)

    @pl.when(e == pl.num_programs(1) - 1)
    def _():
        o_ref[...] = acc_sc[...].T


def _peer_dense(h2, u, vt, a, b, kap):
    M, D = h2.shape
    E = u.shape[0]
    tm = _pick(M, (1024, 512, 256, 128))
    sub = _pick(tm, (512, 256, 128))
    ni = PEER_KEY_ROWS_PER_STEP
    eb = ni * N_KEYS
    kern = functools.partial(_peer_dense_kernel, sub=sub)
    return pl.pallas_call(
        kern, grid=(M // tm, E // eb),
        in_specs=[pl.BlockSpec((tm, D), lambda i, e: (i, 0)),
                  pl.BlockSpec((eb, D), lambda i, e: (e, 0)),
                  pl.BlockSpec((D, eb), lambda i, e: (0, e)),
                  pl.BlockSpec((PK_HEADS, ni, tm), lambda i, e: (0, e, i)),
                  pl.BlockSpec((PK_HEADS, N_KEYS, tm), lambda i, e: (0, 0, i)),
                  pl.BlockSpec((PK_HEADS, tm), lambda i, e: (0, i))],
        out_specs=pl.BlockSpec((tm, D), lambda i, e: (i, 0)),
        out_shape=jax.ShapeDtypeStruct((M, D), F32),
        scratch_shapes=[pltpu.VMEM((D, tm), F32)],
        compiler_params=_cparams("parallel", "arbitrary"), name="peer_dense",
    )(h2, u, vt, a, b, kap)


def _rope_tables(n_ctx, S):
    n_freq = DA_DIM // 4
    n_rows = S // GRID_W
    row = jnp.repeat(jnp.arange(n_rows, dtype=F32), GRID_W)
    col = jnp.tile(jnp.arange(GRID_W, dtype=F32), n_rows)
    inv = ROPE_BASE ** (-jnp.arange(n_freq, dtype=F32) / n_freq)
    ang = jnp.concatenate([row[:, None] * inv, col[:, None] * inv], axis=-1)
    cos = jnp.concatenate([jnp.ones((n_ctx, DA_DIM // 2), F32), jnp.cos(ang)], axis=0)
    sin = jnp.concatenate([jnp.zeros((n_ctx, DA_DIM // 2), F32), jnp.sin(ang)], axis=0)
    return jnp.concatenate([cos] * 4, axis=1), jnp.concatenate([-sin, sin, -sin, sin], axis=1)


def _deinterleave_perm():
    idx = np.arange(DA_HEADS * 2 * DA_DIM).reshape(DA_HEADS * 2, DA_DIM // 2, 2)
    return np.concatenate([idx[:, :, 0], idx[:, :, 1]], axis=1).reshape(-1)


def kernel(x, c, ctx, c_ctx, w_ada, b_ada, norm1_g, norm2_g, w_in, da_lambda, da_subln, dn_conv, dn_a_log,
           dn_dt_bias, dn_norm, w_branch_a, w_branch_b, w_out, peer_wq, peer_keys, peer_u, peer_v, final_g):
    B, S, D = x.shape
    n_ctx = ctx.shape[1]
    T = n_ctx + S
    L = w_ada.shape[0]
    M = B * T
    assert B < SUBLANES and S % GRID_W == 0 and n_ctx % DN_CHUNK == 0 and S % DN_CHUNK == 0
    da_qk = DA_HEADS * 2 * DA_DIM
    da_w = DA_HEADS * DA_VDIM
    dn_qkv = 3 * DN_HEADS * DN_DK
    dn_w = DN_HEADS * DN_DV
    nh2 = 2 * DN_HEADS

    c_all = jnp.zeros((SUBLANES, D), F32).at[:B].set(c).at[B].set(c_ctx)
    mod = _ada(c_all, w_ada, b_ada).reshape(L, SUBLANES, 6, D)
    modsel = jnp.stack([jnp.broadcast_to(mod[:, B:B + 1], (L, B, 6, D)), mod[:, :B]], axis=2)

    cos_t, sin_t = _rope_tables(n_ctx, S)
    cos_m = jnp.tile(cos_t, (B, 1))
    sin_m = jnp.tile(sin_t, (B, 1))
    perm = _deinterleave_perm()

    xs = jnp.concatenate([ctx, x], axis=1)
    f = None
    for l in range(L):
        lam_init = 0.8 - 0.6 * math.exp(-0.3 * l)
        w = w_in[l]
        o = 0
        w_q = w[:, o:o + da_qk][:, perm]; o += da_qk
        w_k = w[:, o:o + da_qk][:, perm]; o += da_qk
        w_v = w[:, o:o + da_w]; o += da_w
        w_dn = w[:, o:o + dn_qkv]; o += dn_qkv
        w_z = w[:, o:o + dn_w]; o += dn_w
        w_b = w[:, o:o + nh2]; o += nh2
        w_a = w[:, o:o + nh2]; o += nh2
        w_ga = w[:, o:o + D]; o += D
        w_gb = w[:, o:o + D]; o += D
        w_qk = jnp.concatenate([w_q, w_k], axis=1).astype(BF16)
        w_rest = jnp.concatenate([w_dn, w_z, w_ga, w_gb], axis=1).astype(BF16)
        dn_col0, z_col0, ga_col0 = 0, dn_qkv, dn_qkv + dn_w
        w_ba_pad = jnp.zeros((D, LANES), F32).at[:, :nh2].set(w_b).at[:, nh2:2 * nh2].set(w_a).astype(BF16)
        gate_par = (jnp.zeros((SUBLANES, LANES), F32)
                    .at[0, nh2:2 * nh2].set(dn_a_log[l].reshape(-1))
                    .at[1, nh2:2 * nh2].set(dn_dt_bias[l].reshape(-1)))

        if l == 0:
            h = _modulate(xs, norm1_g[l], modsel[l], n_ctx)
        else:
            xs, h = _resid_modulate(xs, f, modsel[l - 1], norm1_g[l], modsel[l], n_ctx)
        hm = h.reshape(M, D)

        qk = _proj(hm, w_qk, out_dtype=BF16, kernel=_proj_rope_kernel, extra=(cos_m, sin_m),
                   extra_specs=lambda tm: [pl.BlockSpec((tm, LANES), lambda j, i: (i, 0))] * 2,
                   name="proj_qk_rope").reshape(B, T, 2 * da_qk)
        rest = _proj(hm, w_rest, out_dtype=BF16, extra_specs=lambda tm: [], name="proj_rest").reshape(B, T, -1)
        gates = _proj(hm, w_ba_pad, out_dtype=F32, kernel=_proj_gates_kernel, extra=(gate_par,),
                      extra_specs=lambda tm: [pl.BlockSpec((SUBLANES, LANES), lambda j, i: (0, 0))],
                      name="proj_gates").reshape(B, T, LANES)

        vt = _proj_t(hm, w_v.T.astype(BF16), out_dtype=BF16, name="proj_v_t")

        da = _attention(qk, vt, da_lambda[l], da_subln[l], n_ctx, lam_init)
        dnq = _dn_prep(rest, dn_conv[l], n_ctx, dn_col0)
        dn = _gdn(dnq, gates, rest, z_col0, dn_norm[l], n_ctx)

        xs, h2 = _merge(da, dn, rest, ga_col0, xs, w_branch_a[l].astype(BF16), w_branch_b[l].astype(BF16),
                        w_out[l].astype(BF16), norm2_g[l], modsel[l], n_ctx)
        h2m = h2.reshape(M, D)
        pa, pb, kap = _peer_prep(h2m, peer_wq[l].astype(BF16),
                                 peer_keys[l].reshape(2 * PK_HEADS, N_KEYS, PK_HALF).astype(BF16))
        f = _peer_dense(h2m, (peer_u[l] * GELU_FOLD).astype(BF16), peer_v[l].T.astype(BF16),
                        pa, pb, kap).reshape(B, T, D)

    return _final(xs, f, modsel[L - 1], final_g, n_ctx)
```

```python
import functools
import math

import jax
import jax.numpy as jnp
import numpy as np
from jax import lax
from jax.experimental import pallas as pl
from jax.experimental.pallas import tpu as pltpu

F32 = jnp.float32
BF16 = jnp.bfloat16
HIGHEST = lax.Precision.HIGHEST

GRID_W = 64
EPS = 1e-6
ROPE_BASE = 10000.0
DA_HEADS = 8
DA_DIM = 64
DA_VDIM = 2 * DA_DIM
DN_HEADS = 8
DN_DK = 128
DN_DV = 128
DN_CONV = 5
DN_CHUNK = 64
PK_HEADS = 8
N_KEYS = 128
PK_TOPK = 16
PK_HALF = 128
GELU_FOLD = 2.0 ** -0.5

LANES = 128
SUBLANES = 8
VMEM_LIMIT_BYTES = 56 * 1024 * 1024

NT_DIMS = (((1,), (1,)), ((), ()))


def _cparams(*sem, flags=None):
    return pltpu.CompilerParams(dimension_semantics=sem, vmem_limit_bytes=VMEM_LIMIT_BYTES, flags=flags)


def _pick(n, cands):
    for c in cands:
        if n % c == 0:
            return c
    raise ValueError(f"no tile in {cands} divides {n}")


def _silu(x):
    return x * jax.nn.sigmoid(x)


def _ada_kernel(c_ref, w_ref, b_ref, o_ref):
    sc = _silu(c_ref[...])
    o_ref[0] = jnp.dot(sc, w_ref[0], preferred_element_type=F32, precision=HIGHEST) + b_ref[0]


def _ada(c_all, w_ada, b_ada):
    L, D, N = w_ada.shape
    tn = _pick(N, (1536, 1024, 512, 256, 128))
    return pl.pallas_call(
        _ada_kernel,
        grid=(L, N // tn),
        in_specs=[pl.BlockSpec((SUBLANES, D), lambda l, j: (0, 0)),
                  pl.BlockSpec((1, D, tn), lambda l, j: (l, 0, j)),
                  pl.BlockSpec((1, 1, tn), lambda l, j: (l, 0, j))],
        out_specs=pl.BlockSpec((1, SUBLANES, tn), lambda l, j: (l, 0, j)),
        out_shape=jax.ShapeDtypeStruct((L, SUBLANES, N), F32),
        compiler_params=_cparams("parallel", "parallel"),
        name="ada",
    )(c_all, w_ada, b_ada.reshape(L, 1, N))


def _rms_mod(x, g, m, shift_row, scale_row):
    ms = jnp.mean(x * x, axis=-1, keepdims=True)
    y = x * lax.rsqrt(ms + EPS) * g
    return y * (1.0 + m[scale_row:scale_row + 1]) + m[shift_row:shift_row + 1]


def _modulate_kernel(x_ref, g_ref, m_ref, h_ref):
    h_ref[...] = _rms_mod(x_ref[...], g_ref[...], m_ref[...], 0, 1).astype(BF16)


def _resid_modulate_kernel(x_ref, f_ref, mp_ref, g_ref, m_ref, xo_ref, h_ref):
    x = x_ref[...] + mp_ref[...][5:6] * f_ref[...]
    xo_ref[...] = x
    h_ref[...] = _rms_mod(x, g_ref[...], m_ref[...], 0, 1).astype(BF16)


def _seg_specs(tm, n_ctx_tiles, D):
    tok = pl.BlockSpec((None, tm, D), lambda b, t: (b, t, 0))
    gain = pl.BlockSpec((1, D), lambda b, t: (0, 0))
    mod = pl.BlockSpec((None, None, 6, D), lambda b, t: (b, jnp.where(t >= n_ctx_tiles, 1, 0), 0, 0))
    return tok, gain, mod


def _modulate(xs, g, modsel, n_ctx):
    B, T, D = xs.shape
    tm = _pick(math.gcd(n_ctx, T - n_ctx), (256, 128))
    tok, gain, mod = _seg_specs(tm, n_ctx // tm, D)
    return pl.pallas_call(
        _modulate_kernel, grid=(B, T // tm),
        in_specs=[tok, gain, mod], out_specs=tok,
        out_shape=jax.ShapeDtypeStruct((B, T, D), BF16),
        compiler_params=_cparams("parallel", "parallel"), name="modulate",
    )(xs, g.reshape(1, D), modsel)


def _resid_modulate(xs, f, modsel_prev, g, modsel, n_ctx):
    B, T, D = xs.shape
    tm = _pick(math.gcd(n_ctx, T - n_ctx), (256, 128))
    tok, gain, mod = _seg_specs(tm, n_ctx // tm, D)
    return pl.pallas_call(
        _resid_modulate_kernel, grid=(B, T // tm),
        in_specs=[tok, tok, mod, gain, mod], out_specs=[tok, tok],
        out_shape=[jax.ShapeDtypeStruct((B, T, D), F32), jax.ShapeDtypeStruct((B, T, D), BF16)],
        compiler_params=_cparams("parallel", "parallel"), name="resid_modulate",
    )(xs, f, modsel_prev, g.reshape(1, D), modsel)


def _final_kernel(x_ref, f_ref, mp_ref, g_ref, o_ref):
    x = x_ref[...] + mp_ref[...][5:6] * f_ref[...]
    ms = jnp.mean(x * x, axis=-1, keepdims=True)
    o_ref[...] = x * lax.rsqrt(ms + EPS) * g_ref[...]


def _final(xs, f, modsel_prev, g, n_ctx):
    B, T, D = xs.shape
    S = T - n_ctx
    tm = _pick(math.gcd(n_ctx, S), (256, 128))
    off = n_ctx // tm
    tok_in = pl.BlockSpec((None, tm, D), lambda b, t: (b, t + off, 0))
    return pl.pallas_call(
        _final_kernel, grid=(B, S // tm),
        in_specs=[tok_in, tok_in,
                  pl.BlockSpec((None, None, 6, D), lambda b, t: (b, 1, 0, 0)),
                  pl.BlockSpec((1, D), lambda b, t: (0, 0))],
        out_specs=pl.BlockSpec((None, tm, D), lambda b, t: (b, t, 0)),
        out_shape=jax.ShapeDtypeStruct((B, S, D), F32),
        compiler_params=_cparams("parallel", "parallel"), name="final_norm",
    )(xs, f, modsel_prev, g.reshape(1, D))


def _proj_kernel(h_ref, w_ref, o_ref):
    o_ref[...] = jnp.dot(h_ref[...], w_ref[...], preferred_element_type=F32).astype(o_ref.dtype)


def _proj_rope_kernel(h_ref, w_ref, cos_ref, sin_ref, o_ref):
    acc = jnp.dot(h_ref[...], w_ref[...], preferred_element_type=F32)
    tn = acc.shape[1]
    half = DA_DIM // 2
    lane = lax.broadcasted_iota(jnp.int32, acc.shape, 1)
    partner = jnp.where((lane & (DA_DIM - 1)) < half,
                        pltpu.roll(acc, tn - half, axis=1),
                        pltpu.roll(acc, half, axis=1))
    reps = tn // LANES
    cos = jnp.concatenate([cos_ref[...]] * reps, axis=1)
    sin = jnp.concatenate([sin_ref[...]] * reps, axis=1)
    y = acc * cos + partner * sin
    scale = jnp.where(pl.program_id(0) == 0, DA_DIM ** -0.5 * math.log2(math.e), 1.0).astype(F32)
    o_ref[...] = (y * scale).astype(o_ref.dtype)


def _proj_gates_kernel(h_ref, w_ref, p_ref, o_ref):
    acc = jnp.dot(h_ref[...], w_ref[...], preferred_element_type=F32)
    p = p_ref[...]
    lane = lax.broadcasted_iota(jnp.int32, acc.shape, 1)
    beta = jax.nn.sigmoid(acc)
    g = -jnp.exp(p[0:1]) * jax.nn.softplus(acc + p[1:2])
    o_ref[...] = jnp.where(lane < 2 * DN_HEADS, beta, jnp.where(lane < 4 * DN_HEADS, g, 0.0))


def _proj(h, w, *, out_dtype, kernel=_proj_kernel, extra=(), extra_specs=(), name="proj"):
    M, D = h.shape
    N = w.shape[1]
    tm = _pick(M, (1024, 512, 256, 128))
    tn = _pick(N, (1024, 512, 256, 128))
    return pl.pallas_call(
        kernel, grid=(N // tn, M // tm),
        in_specs=[pl.BlockSpec((tm, D), lambda j, i: (i, 0)),
                  pl.BlockSpec((D, tn), lambda j, i: (0, j))] + list(extra_specs(tm)),
        out_specs=pl.BlockSpec((tm, tn), lambda j, i: (i, j)),
        out_shape=jax.ShapeDtypeStruct((M, N), out_dtype),
        compiler_params=_cparams("parallel", "parallel"), name=name,
    )(h, w, *extra)


def _proj_t_kernel(h_ref, wt_ref, o_ref):
    o_ref[...] = lax.dot_general(wt_ref[...], h_ref[...], NT_DIMS, preferred_element_type=F32).astype(o_ref.dtype)


def _proj_t(h, wt, *, out_dtype, name):
    M, D = h.shape
    N = wt.shape[0]
    tm = _pick(M, (1024, 512, 256, 128))
    tn = _pick(N, (1024, 512, 256, 128))
    return pl.pallas_call(
        _proj_t_kernel, grid=(N // tn, M // tm),
        in_specs=[pl.BlockSpec((tm, D), lambda j, i: (i, 0)),
                  pl.BlockSpec((tn, D), lambda j, i: (j, 0))],
        out_specs=pl.BlockSpec((tn, tm), lambda j, i: (j, i)),
        out_shape=jax.ShapeDtypeStruct((N, M), out_dtype),
        compiler_params=_cparams("parallel", "parallel"), name=name,
    )(h, wt)


ATTN_HEADS_PER_STEP = 2


def _attn_kernel(lam_ref, q_ref, k_ref, vt_ref, g_ref, o_ref, *, n_ctx, tk, lam_init):
    tq = q_ref.shape[0]
    T = k_ref.shape[0]
    qi = pl.program_id(2)
    heads = range(q_ref.shape[1] // DA_VDIM)
    cols = [slice(hh * DA_VDIM, (hh + 1) * DA_VDIM) for hh in heads]
    qqs = []
    for hh in heads:
        q = q_ref[:, cols[hh]]
        lane = lax.broadcasted_iota(jnp.int32, q.shape, 1)
        zero = jnp.zeros_like(q)
        qqs.append(jnp.concatenate([jnp.where(lane < DA_DIM, q, zero), jnp.where(lane >= DA_DIM, q, zero)], axis=0))

    def scores(hh, r0, rows):
        return lax.dot_general(k_ref[r0:r0 + rows, cols[hh]], qqs[hh], NT_DIMS,
                               preferred_element_type=F32)

    def step(hh, state, s, r0, rows):
        m_prev, l_prev, acc = state
        m_new = jnp.maximum(m_prev, jnp.max(s, axis=0, keepdims=True))
        alpha = jnp.exp2(m_prev - m_new)
        p = jnp.exp2(s - m_new)
        pv = jnp.dot(vt_ref[cols[hh], r0:r0 + rows], p.astype(BF16), preferred_element_type=F32)
        return m_new, alpha * l_prev + jnp.sum(p, axis=0, keepdims=True), alpha * acc + pv

    lam = lam_ref[...]
    lam_val = (jnp.exp(jnp.sum(lam[0:1] * lam[1:2], axis=-1, keepdims=True))
               - jnp.exp(jnp.sum(lam[2:3] * lam[3:4], axis=-1, keepdims=True)) + lam_init)

    def finish(hh, state):
        _, l_fin, acc = state
        o_all = acc / l_fin
        o = o_all[:, 0:tq] - lam_val * o_all[:, tq:2 * tq]
        ms = jnp.mean(o * o, axis=0, keepdims=True)
        y = o * lax.rsqrt(ms + EPS) * (1.0 - lam_init)
        o_ref[:, cols[hh]] = (y.T * g_ref[...]).astype(o_ref.dtype)

    init = (jnp.full((1, 2 * tq), -jnp.inf, F32), jnp.zeros((1, 2 * tq), F32), jnp.zeros((DA_VDIM, 2 * tq), F32))

    def attend(blocks):
        state = [init for _ in heads]
        s_next = [scores(hh, *blocks[0]) for hh in heads]
        for j, (r0, rows) in enumerate(blocks):
            for hh in heads:
                s_cur = s_next[hh]
                if j + 1 < len(blocks):
                    s_next[hh] = scores(hh, *blocks[j + 1])
                state[hh] = step(hh, state[hh], s_cur, r0, rows)
        for hh in heads:
            finish(hh, state[hh])

    @pl.when(qi * tq < n_ctx)
    def _():
        attend([(0, n_ctx)])

    @pl.when(qi * tq >= n_ctx)
    def _():
        attend([(0, n_ctx)] + [(n_ctx + j * tk, tk) for j in range((T - n_ctx) // tk)])


def _attention(qk, vt, lam, subln, n_ctx, lam_init):
    B, T, _ = qk.shape
    H = DA_HEADS
    S = T - n_ctx
    tq = _pick(math.gcd(n_ctx, S), (256, 128))
    tk = _pick(S, (1024, 512, 256, 128))
    hps = ATTN_HEADS_PER_STEP
    nb = H // hps
    wblk = hps * DA_VDIM
    kern = functools.partial(_attn_kernel, n_ctx=n_ctx, tk=tk, lam_init=lam_init)
    return pl.pallas_call(
        kern, grid=(B, nb, T // tq),
        in_specs=[pl.BlockSpec((4, DA_DIM), lambda b, h, i: (0, 0)),
                  pl.BlockSpec((None, tq, wblk), lambda b, h, i: (b, i, h)),
                  pl.BlockSpec((None, T, wblk), lambda b, h, i: (b, 0, nb + h)),
                  pl.BlockSpec((wblk, T), lambda b, h, i: (h, b)),
                  pl.BlockSpec((1, DA_VDIM), lambda b, h, i: (0, 0))],
        out_specs=pl.BlockSpec((None, tq, wblk), lambda b, h, i: (b, i, h)),
        out_shape=jax.ShapeDtypeStruct((B, T, H * DA_VDIM), BF16),
        compiler_params=_cparams("parallel", "parallel", "arbitrary"), name="diff_attention",
    )(lam, qk, qk, vt, subln.reshape(1, DA_VDIM))


def _dn_prep_kernel(x_ref, w_ref, o_ref, pad_sc, *, n_ctx, rows):
    T = x_ref.shape[0]
    halo = SUBLANES
    cb = pl.program_id(1)
    zpad = jnp.zeros((halo, LANES), F32)
    pad_sc[0:halo, :] = zpad
    pad_sc[halo + n_ctx:2 * halo + n_ctx, :] = zpad
    pad_sc[2 * halo + T:3 * halo + T, :] = zpad
    pad_sc[halo:halo + n_ctx, :] = x_ref[0:n_ctx, :].astype(F32)
    pad_sc[2 * halo + n_ctx:2 * halo + T, :] = x_ref[n_ctx:T, :].astype(F32)
    w = w_ref[...]
    is_qk = cb < 2 * DN_HEADS
    post = jnp.where(cb < DN_HEADS, DN_DK ** -0.5, 1.0).astype(F32)
    pad = DN_CONV // 2

    for c in range(T // rows):
        r0 = c * rows
        base = r0 + (2 * halo if r0 >= n_ctx else halo)
        y = jnp.zeros((rows, LANES), F32)
        for j in range(DN_CONV):
            y = y + pad_sc[base + (j - pad):base + (j - pad) + rows, :] * w[j:j + 1]
        y = _silu(y)
        yn = y * lax.rsqrt(jnp.sum(y * y, axis=-1, keepdims=True) + EPS) * post
        o_ref[r0:r0 + rows, :] = jnp.where(is_qk, yn, y).astype(o_ref.dtype)


def _dn_prep(rest, conv_w, n_ctx, col0):
    B, T, _ = rest.shape
    nblk = 3 * DN_HEADS
    rows = _pick(math.gcd(n_ctx, T - n_ctx), (256, 128))
    cw = jnp.zeros((SUBLANES, nblk * LANES), F32).at[:DN_CONV].set(conv_w)
    kern = functools.partial(_dn_prep_kernel, n_ctx=n_ctx, rows=rows)
    return pl.pallas_call(
        kern, grid=(B, nblk),
        in_specs=[pl.BlockSpec((None, T, LANES), lambda b, c: (b, 0, col0 // LANES + c)),
                  pl.BlockSpec((SUBLANES, LANES), lambda b, c: (0, c))],
        out_specs=pl.BlockSpec((None, T, LANES), lambda b, c: (b, 0, c)),
        out_shape=jax.ShapeDtypeStruct((B, T, nblk * LANES), BF16),
        scratch_shapes=[pltpu.VMEM((T + 3 * SUBLANES, LANES), F32)],
        compiler_params=_cparams("parallel", "parallel"), name="dn_prep",
    )(rest, cw)


GDN_HEADS_PER_STEP = 2
GDN_CHUNKS_PER_ITER = 8
GDN_INV_PASSES = 1


def _split_dot(a, b, passes):
    a_hi = a.astype(BF16)
    b_hi = b.astype(BF16)
    out = jnp.dot(a_hi, b_hi, preferred_element_type=F32)
    if passes >= 3:
        a_lo = (a - a_hi.astype(F32)).astype(BF16)
        b_lo = (b - b_hi.astype(F32)).astype(BF16)
        out = out + jnp.dot(a_hi, b_lo, preferred_element_type=F32) + jnp.dot(a_lo, b_hi, preferred_element_type=F32)
    return out


def _gdn_kernel(q_ref, k_ref, v_ref, gt_ref, z_ref, ng_ref, o_ref,
                w_sc, qg_sc, kt_sc, a_sc, u_sc, egl_sc, oacc_sc, s_sc, *, n_ctx):
    C = DN_CHUNK
    C2 = 2 * C
    T = q_ref.shape[0]
    nc = T // C
    ncc = n_ctx // C
    hp = pl.program_id(1)

    row = lax.broadcasted_iota(jnp.int32, (C2, C2), 0)
    col = lax.broadcasted_iota(jnp.int32, (C2, C2), 1)
    fwd_row = row < C
    rc_xor = row ^ col
    ahead = (col - row) * jnp.where(fwd_row, 1, -1)
    same_dir = rc_xor < C
    incl = same_dir & (ahead <= 0)
    strict = same_dir & (ahead < 0)
    eye = (row == col).astype(F32)
    mcs = jnp.where(incl, 1.0, 0.0).astype(BF16)

    def p1_load(c, s):
        r0 = pl.multiple_of(c * C, C)
        head = hp * GDN_HEADS_PER_STEP + s
        lo, hi = s * LANES, (s + 1) * LANES
        k = k_ref[pl.ds(r0, C), lo:hi]
        q = q_ref[pl.ds(r0, C), lo:hi]
        v = v_ref[pl.ds(r0, C), lo:hi]
        kk = jnp.concatenate([k, k], axis=0)
        qq = jnp.concatenate([q, q], axis=0)
        vf = jnp.concatenate([v, v], axis=0).astype(F32)
        x = gt_ref[pl.ds(r0, C), :]
        x2 = jnp.concatenate([x, x], axis=0)
        bsel = jnp.where(fwd_row, head, DN_HEADS + head)
        gsel = bsel + 2 * DN_HEADS
        beta = jnp.sum(jnp.where(col == bsel, x2, 0.0), axis=-1, keepdims=True)
        glog = jnp.sum(jnp.where(col == gsel, x2, 0.0), axis=-1, keepdims=True)
        return dict(c=c, s=s, kk=kk, qq=qq, vf=vf, beta=beta, glog=glog)

    def p1_prep(d):
        g_rem = jnp.broadcast_to(d["glog"], (C2, C2))
        gcum = jnp.zeros((C2, C2), F32)
        for _ in range(3):
            piece = g_rem.astype(BF16)
            gcum = gcum + jnp.dot(mcs, piece, preferred_element_type=F32)
            g_rem = g_rem - piece.astype(F32)
        decay = jnp.exp(jnp.where(incl, gcum - gcum.T, -jnp.inf))
        kkt = lax.dot_general(d["kk"], d["kk"], NT_DIMS, preferred_element_type=F32)
        qkt = lax.dot_general(d["qq"], d["kk"], NT_DIMS, preferred_element_type=F32)
        d.update(gcum=gcum, decay=decay, qkt=qkt, lmat=jnp.where(strict, d["beta"] * kkt * decay, 0.0), tinv=eye)

    def p1_finish(d):
        c, s, gcum, beta = d["c"], d["s"], d["gcum"], d["beta"]
        kf = d["kk"].astype(F32)
        eg = jnp.exp(gcum)
        rhs = jnp.concatenate([(d["vf"] * beta).astype(BF16), (kf * beta * eg).astype(BF16)], axis=1)
        uw = jnp.dot(d["tinv"].astype(BF16), rhs, preferred_element_type=F32)
        glast = jnp.where(fwd_row, gcum[C - 1:C, :], gcum[C:C + 1, :])
        ktail = kf * jnp.exp(glast - gcum)
        u_sc[s, c] = uw[:, 0:LANES]
        w_sc[s, c] = uw[:, LANES:2 * LANES].astype(BF16)
        qg_sc[s, c] = (d["qq"].astype(F32) * eg).astype(BF16)
        a_sc[s, c] = (d["qkt"] * d["decay"]).astype(BF16)
        kt_sc[s, c] = ktail.T.astype(BF16)
        egl_sc[s, c] = jnp.exp(jnp.concatenate([jnp.broadcast_to(gcum[C - 1:C, :], (4, LANES)),
                                                jnp.broadcast_to(gcum[C:C + 1, :], (4, LANES))], axis=0))

    def p1_group(c0, cpi):
        probs = [p1_load(c0 + j, s) for j in range(cpi) for s in range(GDN_HEADS_PER_STEP)]
        for d in probs:
            p1_prep(d)
        for lvl in range(int(math.log2(C))):
            for d in probs:
                d["x"] = _split_dot(d["tinv"], jnp.where((rc_xor >> lvl) == 1, d["lmat"], 0.0), GDN_INV_PASSES)
            for d in probs:
                d["tinv"] = d["tinv"] - _split_dot(d["x"], d["tinv"], GDN_INV_PASSES)
        for d in probs:
            p1_finish(d)

    for c_lo, n in ((0, ncc), (ncc, nc - ncc)):
        cpi = _pick(n, tuple(c for c in (8, 4, 2, 1) if c <= GDN_CHUNKS_PER_ITER))

        def p1_body(i, carry, c_lo=c_lo, cpi=cpi):
            p1_group(c_lo + i * cpi, cpi)
            return carry

        lax.fori_loop(0, n // cpi, p1_body, 0)

    oacc_sc[...] = jnp.zeros(oacc_sc.shape, F32)
    s_sc[...] = jnp.zeros(s_sc.shape, F32)
    lane_b = lax.broadcasted_iota(jnp.int32, (LANES, LANES), 1)
    zpad = jnp.zeros((C, LANES), BF16)

    def p2_body(i, carry):
        cf = i
        cb = jnp.where(i < ncc, ncc - 1 - i, nc - 1 + ncc - i)
        rf = pl.multiple_of(cf * C, C)
        rb = pl.multiple_of(cb * C, C)
        heads = range(GDN_HEADS_PER_STEP)
        st = [s_sc[s] for s in heads]
        lhs1 = [jnp.concatenate([
            jnp.concatenate([w_sc[s, cf, 0:C, :], zpad], axis=1),
            jnp.concatenate([zpad, w_sc[s, cb, C:C2, :]], axis=1),
            jnp.concatenate([qg_sc[s, cf, 0:C, :], zpad], axis=1),
            jnp.concatenate([zpad, qg_sc[s, cb, C:C2, :]], axis=1)], axis=0) for s in heads]
        zk = jnp.zeros((LANES, C2), BF16)
        lhs2 = [jnp.concatenate([a_sc[s, cf, 0:C, :], a_sc[s, cb, C:C2, :],
                                 jnp.where(lane_b < C, kt_sc[s, cf], zk),
                                 jnp.where(lane_b >= C, kt_sc[s, cb], zk)], axis=0) for s in heads]
        u = [jnp.concatenate([u_sc[s, cf, 0:C, :], u_sc[s, cb, C:C2, :]], axis=0) for s in heads]
        scale = [jnp.concatenate([jnp.broadcast_to(egl_sc[s, cf][0:1], (DN_DK, DN_DV)),
                                  jnp.broadcast_to(egl_sc[s, cb][4:5], (DN_DK, DN_DV))], axis=0) for s in heads]
        r1 = [jnp.dot(lhs1[s], st[s].astype(BF16), preferred_element_type=F32) for s in heads]
        vnew = [(u[s] - r1[s][0:C2]).astype(BF16) for s in heads]
        r2 = [jnp.dot(lhs2[s], vnew[s], preferred_element_type=F32) for s in heads]
        for s in heads:
            s_sc[s] = st[s] * scale[s] + r2[s][C2:C2 + 2 * DN_DK]
        for s in heads:
            oacc_sc[s, pl.ds(rf, C), :] += r1[s][C2:C2 + C] + r2[s][0:C]
            oacc_sc[s, pl.ds(rb, C), :] += r1[s][C2 + C:2 * C2] + r2[s][C:C2]
        return carry

    lax.fori_loop(0, nc, p2_body, 0)

    ng = ng_ref[...]
    rows = _pick(T, (256, 128))

    def fin_body(c, carry):
        r0 = pl.multiple_of(c * rows, rows)
        for s in range(GDN_HEADS_PER_STEP):
            o = oacc_sc[s, pl.ds(r0, rows), :]
            zf = z_ref[pl.ds(r0, rows), s * LANES:(s + 1) * LANES].astype(F32)
            ms = jnp.mean(o * o, axis=-1, keepdims=True)
            o_ref[pl.ds(r0, rows), s * LANES:(s + 1) * LANES] = (o * lax.rsqrt(ms + EPS) * ng * _silu(zf)).astype(o_ref.dtype)
        return carry

    lax.fori_loop(0, T // rows, fin_body, 0)


def _gdn(dnq, gates, rest, z_col0, norm_g, n_ctx):
    B, T, _ = dnq.shape
    H = DN_HEADS
    hps = GDN_HEADS_PER_STEP
    wblk = hps * LANES
    nc = T // DN_CHUNK
    C2 = 2 * DN_CHUNK
    kern = functools.partial(_gdn_kernel, n_ctx=n_ctx)
    nb = H // hps
    once = dict(pipeline_mode=pl.Buffered(1))
    return pl.pallas_call(
        kern, grid=(B, nb),
        in_specs=[pl.BlockSpec((None, T, wblk), lambda b, h: (b, 0, h), **once),
                  pl.BlockSpec((None, T, wblk), lambda b, h: (b, 0, nb + h), **once),
                  pl.BlockSpec((None, T, wblk), lambda b, h: (b, 0, 2 * nb + h), **once),
                  pl.BlockSpec((None, T, LANES), lambda b, h: (b, 0, 0), **once),
                  pl.BlockSpec((None, T, wblk), lambda b, h: (b, 0, z_col0 // wblk + h), **once),
                  pl.BlockSpec((1, DN_DV), lambda b, h: (0, 0))],
        out_specs=pl.BlockSpec((None, T, wblk), lambda b, h: (b, 0, h)),
        out_shape=jax.ShapeDtypeStruct((B, T, H * DN_DV), BF16),
        scratch_shapes=[pltpu.VMEM((hps, nc, C2, LANES), BF16),
                        pltpu.VMEM((hps, nc, C2, LANES), BF16),
                        pltpu.VMEM((hps, nc, LANES, C2), BF16),
                        pltpu.VMEM((hps, nc, C2, C2), BF16),
                        pltpu.VMEM((hps, nc, C2, LANES), F32),
                        pltpu.VMEM((hps, nc, SUBLANES, LANES), F32),
                        pltpu.VMEM((hps, T, LANES), F32),
                        pltpu.VMEM((hps, 2 * DN_DK, DN_DV), F32)],
        compiler_params=_cparams("parallel", "arbitrary"), name="gated_deltanet",
    )(dnq, dnq, dnq, gates, rest, norm_g.reshape(1, DN_DV))


def _merge_kernel(da_ref, dn_ref, ga_ref, gb_ref, x_ref, wa_ref, wb_ref, wo_ref, g_ref, m_ref, xo_ref, h_ref):
    ya = jnp.dot(da_ref[...], wa_ref[...], preferred_element_type=F32)
    yb = jnp.dot(dn_ref[...], wb_ref[...], preferred_element_type=F32)
    y = jax.nn.sigmoid(ga_ref[...].astype(F32)) * ya + jax.nn.sigmoid(gb_ref[...].astype(F32)) * yb
    y2 = jnp.dot(y.astype(BF16), wo_ref[...], preferred_element_type=F32)
    m = m_ref[...]
    x = x_ref[...] + m[2:3] * y2
    xo_ref[...] = x
    h_ref[...] = _rms_mod(x, g_ref[...], m, 3, 4).astype(BF16)


def _merge(da, dn, rest, ga_col0, xs, w_ba, w_bb, w_o, g2, modsel, n_ctx):
    B, T, D = xs.shape
    tm = _pick(math.gcd(n_ctx, T - n_ctx), (256, 128))
    tok, gain, mod = _seg_specs(tm, n_ctx // tm, D)
    wspec = pl.BlockSpec((D, D), lambda b, t: (0, 0))
    ga_blk = ga_col0 // D
    return pl.pallas_call(
        _merge_kernel, grid=(B, T // tm),
        in_specs=[tok, tok,
                  pl.BlockSpec((None, tm, D), lambda b, t: (b, t, ga_blk)),
                  pl.BlockSpec((None, tm, D), lambda b, t: (b, t, ga_blk + 1)),
                  tok, wspec, wspec, wspec, gain, mod],
        out_specs=[tok, tok],
        out_shape=[jax.ShapeDtypeStruct((B, T, D), F32), jax.ShapeDtypeStruct((B, T, D), BF16)],
        compiler_params=_cparams("parallel", "parallel"), name="merge",
    )(da, dn, rest, rest, xs, w_ba, w_bb, w_o, g2.reshape(1, D), modsel)


def _merge_sort_pairs(n):
    pairs = []
    p = 1
    while p < n:
        k = p
        while k >= 1:
            for j in range(k % p, n - k, 2 * k):
                for i in range(min(k, n - j - k)):
                    if (i + j) // (2 * p) == (i + j + k) // (2 * p):
                        pairs.append((i + j, i + j + k))
            k //= 2
        p *= 2
    return pairs


def _desc_tops(x, n):
    ng = x.shape[0] // SUBLANES
    g = [x[v * SUBLANES:(v + 1) * SUBLANES] for v in range(ng)]
    for i, j in _merge_sort_pairs(pl.next_power_of_2(ng)):
        if j < ng:
            g[i], g[j] = jnp.maximum(g[i], g[j]), jnp.minimum(g[i], g[j])
    neg = jnp.full_like(g[0], -jnp.inf)
    tops = []
    for r in range(n):
        m = jnp.max(g[0], axis=0, keepdims=True)
        tops.append(m)
        hit = g[0] == m
        for v in range(min(ng, n - r)):
            g[v] = jnp.where(hit, g[v + 1] if v + 1 < ng else neg, g[v])
    return tops


def _peer_prep_kernel(h_ref, wq_ref, keys_ref, a_ref, b_ref, kap_ref):
    tm = h_ref.shape[0]
    q = jnp.dot(h_ref[...], wq_ref[...], preferred_element_type=F32).astype(BF16)
    for h in range(PK_HEADS):
        st = [lax.dot_general(keys_ref[2 * h + p], q[:, (2 * h + p) * PK_HALF:(2 * h + p + 1) * PK_HALF],
                              NT_DIMS, preferred_element_type=F32) for p in range(2)]
        ta = _desc_tops(st[0], PK_TOPK + 1)
        tb = _desc_tops(st[1], PK_TOPK + 1)
        tbs = jnp.concatenate(tb[:PK_TOPK], axis=0)
        tb8 = tbs[0:SUBLANES]
        row8 = lax.broadcasted_iota(jnp.int32, tb8.shape, 0)
        pieces = [ta[0] + tbs, ta[1] + tb8]
        for r in range(2, SUBLANES):
            pieces.append(jnp.where(row8 < PK_TOPK // (r + 1), ta[r] + tb8, -jnp.inf))
        pieces.append(jnp.concatenate(ta[SUBLANES:PK_TOPK], axis=0) + tb[0])
        cand = jnp.concatenate(pieces, axis=0)
        best = _desc_tops(cand, PK_TOPK + 1)
        mx = best[0]
        zsum = best[0] * 0.0
        for r in range(PK_TOPK):
            zsum = zsum + jnp.exp(best[r] - mx)
        nxt = jnp.maximum(best[PK_TOPK], jnp.maximum(ta[PK_TOPK] + tb[0], ta[0] + tb[PK_TOPK]))
        thr = 0.5 * (best[PK_TOPK - 1] + nxt)
        rz = GELU_FOLD / zsum
        a_ref[h] = jnp.exp(st[0] - ta[0])
        b_ref[h] = (jnp.exp(st[1] - tb[0]) * rz).astype(b_ref.dtype)
        kap_ref[h:h + 1, :] = jnp.exp(thr - mx) * rz


def _peer_prep(h2, wq, keys):
    M, D = h2.shape
    tm = _pick(M, (256, 128))
    nk = 2 * PK_HEADS
    return pl.pallas_call(
        _peer_prep_kernel, grid=(M // tm,),
        in_specs=[pl.BlockSpec((tm, D), lambda i: (i, 0)),
                  pl.BlockSpec((D, nk * PK_HALF), lambda i: (0, 0)),
                  pl.BlockSpec((nk, N_KEYS, PK_HALF), lambda i: (0, 0, 0))],
        out_specs=[pl.BlockSpec((PK_HEADS, N_KEYS, tm), lambda i: (0, 0, i)),
                   pl.BlockSpec((PK_HEADS, N_KEYS, tm), lambda i: (0, 0, i)),
                   pl.BlockSpec((PK_HEADS, tm), lambda i: (0, i))],
        out_shape=[jax.ShapeDtypeStruct((PK_HEADS, N_KEYS, M), F32),
                   jax.ShapeDtypeStruct((PK_HEADS, N_KEYS, M), BF16),
                   jax.ShapeDtypeStruct((PK_HEADS, M), F32)],
        compiler_params=_cparams("parallel"), name="peer_prep",
    )(h2, wq, keys)


PEER_KEY_ROWS_PER_STEP = 8


def _peer_dense_kernel(h_ref, u_ref, vt_ref, a_ref, b_ref, kap_ref, o_ref, acc_sc, *, sub):
    e = pl.program_id(1)
    tm = h_ref.shape[0]
    eb = u_ref.shape[0]
    ni = eb // N_KEYS

    @pl.when(e == 0)
    def _():
        acc_sc[...] = jnp.zeros(acc_sc.shape, F32)

    def rows16(x):
        x16 = jnp.broadcast_to(x, (2 * SUBLANES, sub)).astype(BF16)
        return jnp.concatenate([x16] * (N_KEYS // (2 * SUBLANES)), axis=0)

    def key_scores(t):
        return lax.dot_general(u_ref[...], h_ref[t * sub:(t + 1) * sub, :], NT_DIMS,
                               preferred_element_type=F32)

    def gates(t):
        tok = slice(t * sub, (t + 1) * sub)
        kaps = [rows16(kap_ref[h:h + 1, tok]) for h in range(PK_HEADS)]
        zero = jnp.zeros((N_KEYS, sub), BF16)
        ws = []
        for il in range(ni):
            w = zero
            for h in range(PK_HEADS):
                prod = b_ref[h, :, tok] * rows16(a_ref[h, il:il + 1, tok])
                w = w + jnp.where(prod >= kaps[h], prod, zero)
            ws.append(w)
        return ws

    n_slab = tm // sub
    sc_next = key_scores(0)
    w_next = gates(0)
    for t in range(n_slab):
        sc, ws = sc_next, w_next
        if t + 1 < n_slab:
            sc_next = key_scores(t + 1)
        was = []
        for il in range(ni):
            t_ = sc[il * N_KEYS:(il + 1) * N_KEYS, :]
            was.append(ws[il] * (t_ + t_ * lax.erf(t_)).astype(BF16))
        acc_sc[:, t * sub:(t + 1) * sub] += jnp.dot(vt_ref[...], jnp.concatenate(was, axis=0),
                                                    preferred_element_type=F32)
        if t + 1 < n_slab:
            w_next = gates(t + 1)

    @pl.when(e == pl.num_programs(1) - 1)
    def _():
        o_ref[...] = acc_sc[...].T


def _peer_dense(h2, u, vt, a, b, kap):
    M, D = h2.shape
    E = u.shape[0]
    tm = _pick(M, (1024, 512, 256, 128))
    sub = _pick(tm, (512, 256, 128))
    ni = PEER_KEY_ROWS_PER_STEP
    eb = ni * N_KEYS
    kern = functools.partial(_peer_dense_kernel, sub=sub)
    return pl.pallas_call(
        kern, grid=(M // tm, E // eb),
        in_specs=[pl.BlockSpec((tm, D), lambda i, e: (i, 0)),
                  pl.BlockSpec((eb, D), lambda i, e: (e, 0)),
                  pl.BlockSpec((D, eb), lambda i, e: (0, e)),
                  pl.BlockSpec((PK_HEADS, ni, tm), lambda i, e: (0, e, i)),
                  pl.BlockSpec((PK_HEADS, N_KEYS, tm), lambda i, e: (0, 0, i)),
                  pl.BlockSpec((PK_HEADS, tm), lambda i, e: (0, i))],
        out_specs=pl.BlockSpec((tm, D), lambda i, e: (i, 0)),
        out_shape=jax.ShapeDtypeStruct((M, D), F32),
        scratch_shapes=[pltpu.VMEM((D, tm), F32)],
        compiler_params=_cparams("parallel", "arbitrary"), name="peer_dense",
    )(h2, u, vt, a, b, kap)


def _rope_tables(n_ctx, S):
    n_freq = DA_DIM // 4
    n_rows = S // GRID_W
    row = jnp.repeat(jnp.arange(n_rows, dtype=F32), GRID_W)
    col = jnp.tile(jnp.arange(GRID_W, dtype=F32), n_rows)
    inv = ROPE_BASE ** (-jnp.arange(n_freq, dtype=F32) / n_freq)
    ang = jnp.concatenate([row[:, None] * inv, col[:, None] * inv], axis=-1)
    cos = jnp.concatenate([jnp.ones((n_ctx, DA_DIM // 2), F32), jnp.cos(ang)], axis=0)
    sin = jnp.concatenate([jnp.zeros((n_ctx, DA_DIM // 2), F32), jnp.sin(ang)], axis=0)
    return jnp.concatenate([cos] * 4, axis=1), jnp.concatenate([-sin, sin, -sin, sin], axis=1)


def _deinterleave_perm():
    idx = np.arange(DA_HEADS * 2 * DA_DIM).reshape(DA_HEADS * 2, DA_DIM // 2, 2)
    return np.concatenate([idx[:, :, 0], idx[:, :, 1]], axis=1).reshape(-1)


def kernel(x, c, ctx, c_ctx, w_ada, b_ada, norm1_g, norm2_g, w_in, da_lambda, da_subln, dn_conv, dn_a_log,
           dn_dt_bias, dn_norm, w_branch_a, w_branch_b, w_out, peer_wq, peer_keys, peer_u, peer_v, final_g):
    B, S, D = x.shape
    n_ctx = ctx.shape[1]
    T = n_ctx + S
    L = w_ada.shape[0]
    M = B * T
    assert B < SUBLANES and S % GRID_W == 0 and n_ctx % DN_CHUNK == 0 and S % DN_CHUNK == 0
    da_qk = DA_HEADS * 2 * DA_DIM
    da_w = DA_HEADS * DA_VDIM
    dn_qkv = 3 * DN_HEADS * DN_DK
    dn_w = DN_HEADS * DN_DV
    nh2 = 2 * DN_HEADS

    c_all = jnp.zeros((SUBLANES, D), F32).at[:B].set(c).at[B].set(c_ctx)
    mod = _ada(c_all, w_ada, b_ada).reshape(L, SUBLANES, 6, D)
    modsel = jnp.stack([jnp.broadcast_to(mod[:, B:B + 1], (L, B, 6, D)), mod[:, :B]], axis=2)

    cos_t, sin_t = _rope_tables(n_ctx, S)
    cos_m = jnp.tile(cos_t, (B, 1))
    sin_m = jnp.tile(sin_t, (B, 1))
    perm = _deinterleave_perm()

    xs = jnp.concatenate([ctx, x], axis=1)
    f = None
    for l in range(L):
        lam_init = 0.8 - 0.6 * math.exp(-0.3 * l)
        w = w_in[l]
        o = 0
        w_q = w[:, o:o + da_qk][:, perm]; o += da_qk
        w_k = w[:, o:o + da_qk][:, perm]; o += da_qk
        w_v = w[:, o:o + da_w]; o += da_w
        w_dn = w[:, o:o + dn_qkv]; o += dn_qkv
        w_z = w[:, o:o + dn_w]; o += dn_w
        w_b = w[:, o:o + nh2]; o += nh2
        w_a = w[:, o:o + nh2]; o += nh2
        w_ga = w[:, o:o + D]; o += D
        w_gb = w[:, o:o + D]; o += D
        w_qk = jnp.concatenate([w_q, w_k], axis=1).astype(BF16)
        w_rest = jnp.concatenate([w_dn, w_z, w_ga, w_gb], axis=1).astype(BF16)
        dn_col0, z_col0, ga_col0 = 0, dn_qkv, dn_qkv + dn_w
        w_ba_pad = jnp.zeros((D, LANES), F32).at[:, :nh2].set(w_b).at[:, nh2:2 * nh2].set(w_a).astype(BF16)
        gate_par = (jnp.zeros((SUBLANES, LANES), F32)
                    .at[0, nh2:2 * nh2].set(dn_a_log[l].reshape(-1))
                    .at[1, nh2:2 * nh2].set(dn_dt_bias[l].reshape(-1)))

        if l == 0:
            h = _modulate(xs, norm1_g[l], modsel[l], n_ctx)
        else:
            xs, h = _resid_modulate(xs, f, modsel[l - 1], norm1_g[l], modsel[l], n_ctx)
        hm = h.reshape(M, D)

        qk = _proj(hm, w_qk, out_dtype=BF16, kernel=_proj_rope_kernel, extra=(cos_m, sin_m),
                   extra_specs=lambda tm: [pl.BlockSpec((tm, LANES), lambda j, i: (i, 0))] * 2,
                   name="proj_qk_rope").reshape(B, T, 2 * da_qk)
        rest = _proj(hm, w_rest, out_dtype=BF16, extra_specs=lambda tm: [], name="proj_rest").reshape(B, T, -1)
        gates = _proj(hm, w_ba_pad, out_dtype=F32, kernel=_proj_gates_kernel, extra=(gate_par,),
                      extra_specs=lambda tm: [pl.BlockSpec((SUBLANES, LANES), lambda j, i: (0, 0))],
                      name="proj_gates").reshape(B, T, LANES)

        vt = _proj_t(hm, w_v.T.astype(BF16), out_dtype=BF16, name="proj_v_t")

        da = _attention(qk, vt, da_lambda[l], da_subln[l], n_ctx, lam_init)
        dnq = _dn_prep(rest, dn_conv[l], n_ctx, dn_col0)
        dn = _gdn(dnq, gates, rest, z_col0, dn_norm[l], n_ctx)

        xs, h2 = _merge(da, dn, rest, ga_col0, xs, w_branch_a[l].astype(BF16), w_branch_b[l].astype(BF16),
                        w_out[l].astype(BF16), norm2_g[l], modsel[l], n_ctx)
        h2m = h2.reshape(M, D)
        pa, pb, kap = _peer_prep(h2m, peer_wq[l].astype(BF16),
                                 peer_keys[l].reshape(2 * PK_HEADS, N_KEYS, PK_HALF).astype(BF16))
        f = _peer_dense(h2m, (peer_u[l] * GELU_FOLD).astype(BF16), peer_v[l].T.astype(BF16),
                        pa, pb, kap).reshape(B, T, D)

    return _final(xs, f, modsel[L - 1], final_g, n_ctx)
```

```python
import functools
import math

import jax
import jax.numpy as jnp
import numpy as np
from jax import lax
from jax.experimental import pallas as pl
from jax.experimental.pallas import tpu as pltpu

F32 = jnp.float32
BF16 = jnp.bfloat16
HIGHEST = lax.Precision.HIGHEST

GRID_W = 64
EPS = 1e-6
ROPE_BASE = 10000.0
DA_HEADS = 8
DA_DIM = 64
DA_VDIM = 2 * DA_DIM
DN_HEADS = 8
DN_DK = 128
DN_DV = 128
DN_CONV = 5
DN_CHUNK = 64
PK_HEADS = 8
N_KEYS = 128
PK_TOPK = 16
PK_HALF = 128
GELU_FOLD = 2.0 ** -0.5

LANES = 128
SUBLANES = 8
VMEM_LIMIT_BYTES = 56 * 1024 * 1024

NT_DIMS = (((1,), (1,)), ((), ()))


def _cparams(*sem, flags=None):
    return pltpu.CompilerParams(dimension_semantics=sem, vmem_limit_bytes=VMEM_LIMIT_BYTES, flags=flags)


def _pick(n, cands):
    for c in cands:
        if n % c == 0:
            return c
    raise ValueError(f"no tile in {cands} divides {n}")


def _silu(x):
    return x * jax.nn.sigmoid(x)


def _ada_kernel(c_ref, w_ref, b_ref, o_ref):
    sc = _silu(c_ref[...])
    o_ref[0] = jnp.dot(sc, w_ref[0], preferred_element_type=F32, precision=HIGHEST) + b_ref[0]


def _ada(c_all, w_ada, b_ada):
    L, D, N = w_ada.shape
    tn = _pick(N, (1536, 1024, 512, 256, 128))
    return pl.pallas_call(
        _ada_kernel,
        grid=(L, N // tn),
        in_specs=[pl.BlockSpec((SUBLANES, D), lambda l, j: (0, 0)),
                  pl.BlockSpec((1, D, tn), lambda l, j: (l, 0, j)),
                  pl.BlockSpec((1, 1, tn), lambda l, j: (l, 0, j))],
        out_specs=pl.BlockSpec((1, SUBLANES, tn), lambda l, j: (l, 0, j)),
        out_shape=jax.ShapeDtypeStruct((L, SUBLANES, N), F32),
        compiler_params=_cparams("parallel", "parallel"),
        name="ada",
    )(c_all, w_ada, b_ada.reshape(L, 1, N))


def _rms_mod(x, g, m, shift_row, scale_row):
    ms = jnp.mean(x * x, axis=-1, keepdims=True)
    y = x * lax.rsqrt(ms + EPS) * g
    return y * (1.0 + m[scale_row:scale_row + 1]) + m[shift_row:shift_row + 1]


def _modulate_kernel(x_ref, g_ref, m_ref, h_ref):
    h_ref[...] = _rms_mod(x_ref[...], g_ref[...], m_ref[...], 0, 1).astype(BF16)


def _resid_modulate_kernel(x_ref, f_ref, mp_ref, g_ref, m_ref, xo_ref, h_ref):
    x = x_ref[...] + mp_ref[...][5:6] * f_ref[...]
    xo_ref[...] = x
    h_ref[...] = _rms_mod(x, g_ref[...], m_ref[...], 0, 1).astype(BF16)


def _seg_specs(tm, n_ctx_tiles, D):
    tok = pl.BlockSpec((None, tm, D), lambda b, t: (b, t, 0))
    gain = pl.BlockSpec((1, D), lambda b, t: (0, 0))
    mod = pl.BlockSpec((None, None, 6, D), lambda b, t: (b, jnp.where(t >= n_ctx_tiles, 1, 0), 0, 0))
    return tok, gain, mod


def _modulate(xs, g, modsel, n_ctx):
    B, T, D = xs.shape
    tm = _pick(math.gcd(n_ctx, T - n_ctx), (256, 128))
    tok, gain, mod = _seg_specs(tm, n_ctx // tm, D)
    return pl.pallas_call(
        _modulate_kernel, grid=(B, T // tm),
        in_specs=[tok, gain, mod], out_specs=tok,
        out_shape=jax.ShapeDtypeStruct((B, T, D), BF16),
        compiler_params=_cparams("parallel", "parallel"), name="modulate",
    )(xs, g.reshape(1, D), modsel)


def _resid_modulate(xs, f, modsel_prev, g, modsel, n_ctx):
    B, T, D = xs.shape
    tm = _pick(math.gcd(n_ctx, T - n_ctx), (256, 128))
    tok, gain, mod = _seg_specs(tm, n_ctx // tm, D)
    return pl.pallas_call(
        _resid_modulate_kernel, grid=(B, T // tm),
        in_specs=[tok, tok, mod, gain, mod], out_specs=[tok, tok],
        out_shape=[jax.ShapeDtypeStruct((B, T, D), F32), jax.ShapeDtypeStruct((B, T, D), BF16)],
        compiler_params=_cparams("parallel", "parallel"), name="resid_modulate",
    )(xs, f, modsel_prev, g.reshape(1, D), modsel)


def _final_kernel(x_ref, f_ref, mp_ref, g_ref, o_ref):
    x = x_ref[...] + mp_ref[...][5:6] * f_ref[...]
    ms = jnp.mean(x * x, axis=-1, keepdims=True)
    o_ref[...] = x * lax.rsqrt(ms + EPS) * g_ref[...]


def _final(xs, f, modsel_prev, g, n_ctx):
    B, T, D = xs.shape
    S = T - n_ctx
    tm = _pick(math.gcd(n_ctx, S), (256, 128))
    off = n_ctx // tm
    tok_in = pl.BlockSpec((None, tm, D), lambda b, t: (b, t + off, 0))
    return pl.pallas_call(
        _final_kernel, grid=(B, S // tm),
        in_specs=[tok_in, tok_in,
                  pl.BlockSpec((None, None, 6, D), lambda b, t: (b, 1, 0, 0)),
                  pl.BlockSpec((1, D), lambda b, t: (0, 0))],
        out_specs=pl.BlockSpec((None, tm, D), lambda b, t: (b, t, 0)),
        out_shape=jax.ShapeDtypeStruct((B, S, D), F32),
        compiler_params=_cparams("parallel", "parallel"), name="final_norm",
    )(xs, f, modsel_prev, g.reshape(1, D))


def _proj_kernel(h_ref, w_ref, o_ref):
    o_ref[...] = jnp.dot(h_ref[...], w_ref[...], preferred_element_type=F32).astype(o_ref.dtype)


def _proj_rope_kernel(h_ref, w_ref, cos_ref, sin_ref, o_ref):
    acc = jnp.dot(h_ref[...], w_ref[...], preferred_element_type=F32)
    tn = acc.shape[1]
    half = DA_DIM // 2
    lane = lax.broadcasted_iota(jnp.int32, acc.shape, 1)
    partner = jnp.where((lane & (DA_DIM - 1)) < half,
                        pltpu.roll(acc, tn - half, axis=1),
                        pltpu.roll(acc, half, axis=1))
    reps = tn // LANES
    cos = jnp.concatenate([cos_ref[...]] * reps, axis=1)
    sin = jnp.concatenate([sin_ref[...]] * reps, axis=1)
    y = acc * cos + partner * sin
    scale = jnp.where(pl.program_id(0) == 0, DA_DIM ** -0.5 * math.log2(math.e), 1.0).astype(F32)
    o_ref[...] = (y * scale).astype(o_ref.dtype)


def _proj_gates_kernel(h_ref, w_ref, p_ref, o_ref):
    acc = jnp.dot(h_ref[...], w_ref[...], preferred_element_type=F32)
    p = p_ref[...]
    lane = lax.broadcasted_iota(jnp.int32, acc.shape, 1)
    beta = jax.nn.sigmoid(acc)
    g = -jnp.exp(p[0:1]) * jax.nn.softplus(acc + p[1:2])
    o_ref[...] = jnp.where(lane < 2 * DN_HEADS, beta, jnp.where(lane < 4 * DN_HEADS, g, 0.0))


def _proj(h, w, *, out_dtype, kernel=_proj_kernel, extra=(), extra_specs=(), name="proj"):
    M, D = h.shape
    N = w.shape[1]
    tm = _pick(M, (1024, 512, 256, 128))
    tn = _pick(N, (1024, 512, 256, 128))
    return pl.pallas_call(
        kernel, grid=(N // tn, M // tm),
        in_specs=[pl.BlockSpec((tm, D), lambda j, i: (i, 0)),
                  pl.BlockSpec((D, tn), lambda j, i: (0, j))] + list(extra_specs(tm)),
        out_specs=pl.BlockSpec((tm, tn), lambda j, i: (i, j)),
        out_shape=jax.ShapeDtypeStruct((M, N), out_dtype),
        compiler_params=_cparams("parallel", "parallel"), name=name,
    )(h, w, *extra)


def _proj_t_kernel(h_ref, wt_ref, o_ref):
    o_ref[...] = lax.dot_general(wt_ref[...], h_ref[...], NT_DIMS, preferred_element_type=F32).astype(o_ref.dtype)


def _proj_t(h, wt, *, out_dtype, name):
    M, D = h.shape
    N = wt.shape[0]
    tm = _pick(M, (1024, 512, 256, 128))
    tn = _pick(N, (1024, 512, 256, 128))
    return pl.pallas_call(
        _proj_t_kernel, grid=(N // tn, M // tm),
        in_specs=[pl.BlockSpec((tm, D), lambda j, i: (i, 0)),
                  pl.BlockSpec((tn, D), lambda j, i: (j, 0))],
        out_specs=pl.BlockSpec((tn, tm), lambda j, i: (j, i)),
        out_shape=jax.ShapeDtypeStruct((N, M), out_dtype),
        compiler_params=_cparams("parallel", "parallel"), name=name,
    )(h, wt)


ATTN_HEADS_PER_STEP = 4


def _attn_kernel(lam_ref, q_ref, k_ref, vt_ref, g_ref, o_ref, *, n_ctx, tk, lam_init):
    tq = q_ref.shape[0]
    T = k_ref.shape[0]
    qi = pl.program_id(2)
    heads = range(q_ref.shape[1] // DA_VDIM)
    cols = [slice(hh * DA_VDIM, (hh + 1) * DA_VDIM) for hh in heads]
    qqs = []
    for hh in heads:
        q = q_ref[:, cols[hh]]
        lane = lax.broadcasted_iota(jnp.int32, q.shape, 1)
        zero = jnp.zeros_like(q)
        qqs.append(jnp.concatenate([jnp.where(lane < DA_DIM, q, zero), jnp.where(lane >= DA_DIM, q, zero)], axis=0))

    def scores(hh, r0, rows):
        return lax.dot_general(k_ref[r0:r0 + rows, cols[hh]], qqs[hh], NT_DIMS,
                               preferred_element_type=F32)

    def step(hh, state, s, r0, rows):
        m_prev, l_prev, acc = state
        m_new = jnp.maximum(m_prev, jnp.max(s, axis=0, keepdims=True))
        alpha = jnp.exp2(m_prev - m_new)
        p = jnp.exp2(s - m_new)
        pv = jnp.dot(vt_ref[cols[hh], r0:r0 + rows], p.astype(BF16), preferred_element_type=F32)
        return m_new, alpha * l_prev + jnp.sum(p, axis=0, keepdims=True), alpha * acc + pv

    lam = lam_ref[...]
    lam_val = (jnp.exp(jnp.sum(lam[0:1] * lam[1:2], axis=-1, keepdims=True))
               - jnp.exp(jnp.sum(lam[2:3] * lam[3:4], axis=-1, keepdims=True)) + lam_init)

    def finish(hh, state):
        _, l_fin, acc = state
        o_all = acc / l_fin
        o = o_all[:, 0:tq] - lam_val * o_all[:, tq:2 * tq]
        ms = jnp.mean(o * o, axis=0, keepdims=True)
        y = o * lax.rsqrt(ms + EPS) * (1.0 - lam_init)
        o_ref[:, cols[hh]] = (y.T * g_ref[...]).astype(o_ref.dtype)

    init = (jnp.full((1, 2 * tq), -jnp.inf, F32), jnp.zeros((1, 2 * tq), F32), jnp.zeros((DA_VDIM, 2 * tq), F32))

    def attend(blocks):
        state = [init for _ in heads]
        s_next = [scores(hh, *blocks[0]) for hh in heads]
        for j, (r0, rows) in enumerate(blocks):
            for hh in heads:
                s_cur = s_next[hh]
                if j + 1 < len(blocks):
                    s_next[hh] = scores(hh, *blocks[j + 1])
                state[hh] = step(hh, state[hh], s_cur, r0, rows)
        for hh in heads:
            finish(hh, state[hh])

    @pl.when(qi * tq < n_ctx)
    def _():
        attend([(0, n_ctx)])

    @pl.when(qi * tq >= n_ctx)
    def _():
        attend([(0, n_ctx)] + [(n_ctx + j * tk, tk) for j in range((T - n_ctx) // tk)])


def _attention(qk, vt, lam, subln, n_ctx, lam_init):
    B, T, _ = qk.shape
    H = DA_HEADS
    S = T - n_ctx
    tq = _pick(math.gcd(n_ctx, S), (256, 128))
    tk = _pick(S, (1024, 512, 256, 128))
    hps = ATTN_HEADS_PER_STEP
    nb = H // hps
    wblk = hps * DA_VDIM
    kern = functools.partial(_attn_kernel, n_ctx=n_ctx, tk=tk, lam_init=lam_init)
    return pl.pallas_call(
        kern, grid=(B, nb, T // tq),
        in_specs=[pl.BlockSpec((4, DA_DIM), lambda b, h, i: (0, 0)),
                  pl.BlockSpec((None, tq, wblk), lambda b, h, i: (b, i, h)),
                  pl.BlockSpec((None, T, wblk), lambda b, h, i: (b, 0, nb + h)),
                  pl.BlockSpec((wblk, T), lambda b, h, i: (h, b)),
                  pl.BlockSpec((1, DA_VDIM), lambda b, h, i: (0, 0))],
        out_specs=pl.BlockSpec((None, tq, wblk), lambda b, h, i: (b, i, h)),
        out_shape=jax.ShapeDtypeStruct((B, T, H * DA_VDIM), BF16),
        compiler_params=_cparams("parallel", "parallel", "arbitrary"), name="diff_attention",
    )(lam, qk, qk, vt, subln.reshape(1, DA_VDIM))


def _dn_prep_kernel(x_ref, w_ref, o_ref, pad_sc, *, n_ctx, rows):
    T = x_ref.shape[0]
    halo = SUBLANES
    cb = pl.program_id(1)
    zpad = jnp.zeros((halo, LANES), F32)
    pad_sc[0:halo, :] = zpad
    pad_sc[halo + n_ctx:2 * halo + n_ctx, :] = zpad
    pad_sc[2 * halo + T:3 * halo + T, :] = zpad
    pad_sc[halo:halo + n_ctx, :] = x_ref[0:n_ctx, :].astype(F32)
    pad_sc[2 * halo + n_ctx:2 * halo + T, :] = x_ref[n_ctx:T, :].astype(F32)
    w = w_ref[...]
    is_qk = cb < 2 * DN_HEADS
    post = jnp.where(cb < DN_HEADS, DN_DK ** -0.5, 1.0).astype(F32)
    pad = DN_CONV // 2

    for c in range(T // rows):
        r0 = c * rows
        base = r0 + (2 * halo if r0 >= n_ctx else halo)
        y = jnp.zeros((rows, LANES), F32)
        for j in range(DN_CONV):
            y = y + pad_sc[base + (j - pad):base + (j - pad) + rows, :] * w[j:j + 1]
        y = _silu(y)
        yn = y * lax.rsqrt(jnp.sum(y * y, axis=-1, keepdims=True) + EPS) * post
        o_ref[r0:r0 + rows, :] = jnp.where(is_qk, yn, y).astype(o_ref.dtype)


def _dn_prep(rest, conv_w, n_ctx, col0):
    B, T, _ = rest.shape
    nblk = 3 * DN_HEADS
    rows = _pick(math.gcd(n_ctx, T - n_ctx), (256, 128))
    cw = jnp.zeros((SUBLANES, nblk * LANES), F32).at[:DN_CONV].set(conv_w)
    kern = functools.partial(_dn_prep_kernel, n_ctx=n_ctx, rows=rows)
    return pl.pallas_call(
        kern, grid=(B, nblk),
        in_specs=[pl.BlockSpec((None, T, LANES), lambda b, c: (b, 0, col0 // LANES + c)),
                  pl.BlockSpec((SUBLANES, LANES), lambda b, c: (0, c))],
        out_specs=pl.BlockSpec((None, T, LANES), lambda b, c: (b, 0, c)),
        out_shape=jax.ShapeDtypeStruct((B, T, nblk * LANES), BF16),
        scratch_shapes=[pltpu.VMEM((T + 3 * SUBLANES, LANES), F32)],
        compiler_params=_cparams("parallel", "parallel"), name="dn_prep",
    )(rest, cw)


GDN_HEADS_PER_STEP = 2
GDN_CHUNKS_PER_ITER = 8
GDN_INV_PASSES = 1


def _split_dot(a, b, passes):
    a_hi = a.astype(BF16)
    b_hi = b.astype(BF16)
    out = jnp.dot(a_hi, b_hi, preferred_element_type=F32)
    if passes >= 3:
        a_lo = (a - a_hi.astype(F32)).astype(BF16)
        b_lo = (b - b_hi.astype(F32)).astype(BF16)
        out = out + jnp.dot(a_hi, b_lo, preferred_element_type=F32) + jnp.dot(a_lo, b_hi, preferred_element_type=F32)
    return out


def _gdn_kernel(q_ref, k_ref, v_ref, gt_ref, z_ref, ng_ref, o_ref,
                w_sc, qg_sc, kt_sc, a_sc, u_sc, egl_sc, oacc_sc, s_sc, *, n_ctx):
    C = DN_CHUNK
    C2 = 2 * C
    T = q_ref.shape[0]
    nc = T // C
    ncc = n_ctx // C
    hp = pl.program_id(1)

    row = lax.broadcasted_iota(jnp.int32, (C2, C2), 0)
    col = lax.broadcasted_iota(jnp.int32, (C2, C2), 1)
    fwd_row = row < C
    rc_xor = row ^ col
    ahead = (col - row) * jnp.where(fwd_row, 1, -1)
    same_dir = rc_xor < C
    incl = same_dir & (ahead <= 0)
    strict = same_dir & (ahead < 0)
    eye = (row == col).astype(F32)
    mcs = jnp.where(incl, 1.0, 0.0).astype(BF16)

    def p1_load(c, s):
        r0 = pl.multiple_of(c * C, C)
        head = hp * GDN_HEADS_PER_STEP + s
        lo, hi = s * LANES, (s + 1) * LANES
        k = k_ref[pl.ds(r0, C), lo:hi]
        q = q_ref[pl.ds(r0, C), lo:hi]
        v = v_ref[pl.ds(r0, C), lo:hi]
        kk = jnp.concatenate([k, k], axis=0)
        qq = jnp.concatenate([q, q], axis=0)
        vf = jnp.concatenate([v, v], axis=0).astype(F32)
        x = gt_ref[pl.ds(r0, C), :]
        x2 = jnp.concatenate([x, x], axis=0)
        bsel = jnp.where(fwd_row, head, DN_HEADS + head)
        gsel = bsel + 2 * DN_HEADS
        beta = jnp.sum(jnp.where(col == bsel, x2, 0.0), axis=-1, keepdims=True)
        glog = jnp.sum(jnp.where(col == gsel, x2, 0.0), axis=-1, keepdims=True)
        return dict(c=c, s=s, kk=kk, qq=qq, vf=vf, beta=beta, glog=glog)

    def p1_prep(d):
        g_rem = jnp.broadcast_to(d["glog"], (C2, C2))
        gcum = jnp.zeros((C2, C2), F32)
        for _ in range(3):
            piece = g_rem.astype(BF16)
            gcum = gcum + jnp.dot(mcs, piece, preferred_element_type=F32)
            g_rem = g_rem - piece.astype(F32)
        decay = jnp.exp(jnp.where(incl, gcum - gcum.T, -jnp.inf))
        kkt = lax.dot_general(d["kk"], d["kk"], NT_DIMS, preferred_element_type=F32)
        qkt = lax.dot_general(d["qq"], d["kk"], NT_DIMS, preferred_element_type=F32)
        d.update(gcum=gcum, decay=decay, qkt=qkt, lmat=jnp.where(strict, d["beta"] * kkt * decay, 0.0), tinv=eye)

    def p1_finish(d):
        c, s, gcum, beta = d["c"], d["s"], d["gcum"], d["beta"]
        kf = d["kk"].astype(F32)
        eg = jnp.exp(gcum)
        rhs = jnp.concatenate([(d["vf"] * beta).astype(BF16), (kf * beta * eg).astype(BF16)], axis=1)
        uw = jnp.dot(d["tinv"].astype(BF16), rhs, preferred_element_type=F32)
        glast = jnp.where(fwd_row, gcum[C - 1:C, :], gcum[C:C + 1, :])
        ktail = kf * jnp.exp(glast - gcum)
        u_sc[s, c] = uw[:, 0:LANES]
        w_sc[s, c] = uw[:, LANES:2 * LANES].astype(BF16)
        qg_sc[s, c] = (d["qq"].astype(F32) * eg).astype(BF16)
        a_sc[s, c] = (d["qkt"] * d["decay"]).astype(BF16)
        kt_sc[s, c] = ktail.T.astype(BF16)
        egl_sc[s, c] = jnp.exp(jnp.concatenate([jnp.broadcast_to(gcum[C - 1:C, :], (4, LANES)),
                                                jnp.broadcast_to(gcum[C:C + 1, :], (4, LANES))], axis=0))

    def p1_group(c0, cpi):
        probs = [p1_load(c0 + j, s) for j in range(cpi) for s in range(GDN_HEADS_PER_STEP)]
        for d in probs:
            p1_prep(d)
        for lvl in range(int(math.log2(C))):
            for d in probs:
                d["x"] = _split_dot(d["tinv"], jnp.where((rc_xor >> lvl) == 1, d["lmat"], 0.0), GDN_INV_PASSES)
            for d in probs:
                d["tinv"] = d["tinv"] - _split_dot(d["x"], d["tinv"], GDN_INV_PASSES)
        for d in probs:
            p1_finish(d)

    for c_lo, n in ((0, ncc), (ncc, nc - ncc)):
        cpi = _pick(n, tuple(c for c in (8, 4, 2, 1) if c <= GDN_CHUNKS_PER_ITER))

        def p1_body(i, carry, c_lo=c_lo, cpi=cpi):
            p1_group(c_lo + i * cpi, cpi)
            return carry

        lax.fori_loop(0, n // cpi, p1_body, 0)

    oacc_sc[...] = jnp.zeros(oacc_sc.shape, F32)
    s_sc[...] = jnp.zeros(s_sc.shape, F32)
    lane_b = lax.broadcasted_iota(jnp.int32, (LANES, LANES), 1)
    zpad = jnp.zeros((C, LANES), BF16)

    def p2_body(i, carry):
        cf = i
        cb = jnp.where(i < ncc, ncc - 1 - i, nc - 1 + ncc - i)
        rf = pl.multiple_of(cf * C, C)
        rb = pl.multiple_of(cb * C, C)
        heads = range(GDN_HEADS_PER_STEP)
        st = [s_sc[s] for s in heads]
        lhs1 = [jnp.concatenate([
            jnp.concatenate([w_sc[s, cf, 0:C, :], zpad], axis=1),
            jnp.concatenate([zpad, w_sc[s, cb, C:C2, :]], axis=1),
            jnp.concatenate([qg_sc[s, cf, 0:C, :], zpad], axis=1),
            jnp.concatenate([zpad, qg_sc[s, cb, C:C2, :]], axis=1)], axis=0) for s in heads]
        zk = jnp.zeros((LANES, C2), BF16)
        lhs2 = [jnp.concatenate([a_sc[s, cf, 0:C, :], a_sc[s, cb, C:C2, :],
                                 jnp.where(lane_b < C, kt_sc[s, cf], zk),
                                 jnp.where(lane_b >= C, kt_sc[s, cb], zk)], axis=0) for s in heads]
        u = [jnp.concatenate([u_sc[s, cf, 0:C, :], u_sc[s, cb, C:C2, :]], axis=0) for s in heads]
        scale = [jnp.concatenate([jnp.broadcast_to(egl_sc[s, cf][0:1], (DN_DK, DN_DV)),
                                  jnp.broadcast_to(egl_sc[s, cb][4:5], (DN_DK, DN_DV))], axis=0) for s in heads]
        r1 = [jnp.dot(lhs1[s], st[s].astype(BF16), preferred_element_type=F32) for s in heads]
        vnew = [(u[s] - r1[s][0:C2]).astype(BF16) for s in heads]
        r2 = [jnp.dot(lhs2[s], vnew[s], preferred_element_type=F32) for s in heads]
        for s in heads:
            s_sc[s] = st[s] * scale[s] + r2[s][C2:C2 + 2 * DN_DK]
        for s in heads:
            oacc_sc[s, pl.ds(rf, C), :] += r1[s][C2:C2 + C] + r2[s][0:C]
            oacc_sc[s, pl.ds(rb, C), :] += r1[s][C2 + C:2 * C2] + r2[s][C:C2]
        return carry

    lax.fori_loop(0, nc, p2_body, 0)

    ng = ng_ref[...]
    rows = _pick(T, (256, 128))

    def fin_body(c, carry):
        r0 = pl.multiple_of(c * rows, rows)
        for s in range(GDN_HEADS_PER_STEP):
            o = oacc_sc[s, pl.ds(r0, rows), :]
            zf = z_ref[pl.ds(r0, rows), s * LANES:(s + 1) * LANES].astype(F32)
            ms = jnp.mean(o * o, axis=-1, keepdims=True)
            o_ref[pl.ds(r0, rows), s * LANES:(s + 1) * LANES] = (o * lax.rsqrt(ms + EPS) * ng * _silu(zf)).astype(o_ref.dtype)
        return carry

    lax.fori_loop(0, T // rows, fin_body, 0)


def _gdn(dnq, gates, rest, z_col0, norm_g, n_ctx):
    B, T, _ = dnq.shape
    H = DN_HEADS
    hps = GDN_HEADS_PER_STEP
    wblk = hps * LANES
    nc = T // DN_CHUNK
    C2 = 2 * DN_CHUNK
    kern = functools.partial(_gdn_kernel, n_ctx=n_ctx)
    nb = H // hps
    once = dict(pipeline_mode=pl.Buffered(1))
    return pl.pallas_call(
        kern, grid=(B, nb),
        in_specs=[pl.BlockSpec((None, T, wblk), lambda b, h: (b, 0, h), **once),
                  pl.BlockSpec((None, T, wblk), lambda b, h: (b, 0, nb + h), **once),
                  pl.BlockSpec((None, T, wblk), lambda b, h: (b, 0, 2 * nb + h), **once),
                  pl.BlockSpec((None, T, LANES), lambda b, h: (b, 0, 0), **once),
                  pl.BlockSpec((None, T, wblk), lambda b, h: (b, 0, z_col0 // wblk + h), **once),
                  pl.BlockSpec((1, DN_DV), lambda b, h: (0, 0))],
        out_specs=pl.BlockSpec((None, T, wblk), lambda b, h: (b, 0, h)),
        out_shape=jax.ShapeDtypeStruct((B, T, H * DN_DV), BF16),
        scratch_shapes=[pltpu.VMEM((hps, nc, C2, LANES), BF16),
                        pltpu.VMEM((hps, nc, C2, LANES), BF16),
                        pltpu.VMEM((hps, nc, LANES, C2), BF16),
                        pltpu.VMEM((hps, nc, C2, C2), BF16),
                        pltpu.VMEM((hps, nc, C2, LANES), F32),
                        pltpu.VMEM((hps, nc, SUBLANES, LANES), F32),
                        pltpu.VMEM((hps, T, LANES), F32),
                        pltpu.VMEM((hps, 2 * DN_DK, DN_DV), F32)],
        compiler_params=_cparams("parallel", "arbitrary"), name="gated_deltanet",
    )(dnq, dnq, dnq, gates, rest, norm_g.reshape(1, DN_DV))


def _merge_kernel(da_ref, dn_ref, ga_ref, gb_ref, x_ref, wa_ref, wb_ref, wo_ref, g_ref, m_ref, xo_ref, h_ref):
    ya = jnp.dot(da_ref[...], wa_ref[...], preferred_element_type=F32)
    yb = jnp.dot(dn_ref[...], wb_ref[...], preferred_element_type=F32)
    y = jax.nn.sigmoid(ga_ref[...].astype(F32)) * ya + jax.nn.sigmoid(gb_ref[...].astype(F32)) * yb
    y2 = jnp.dot(y.astype(BF16), wo_ref[...], preferred_element_type=F32)
    m = m_ref[...]
    x = x_ref[...] + m[2:3] * y2
    xo_ref[...] = x
    h_ref[...] = _rms_mod(x, g_ref[...], m, 3, 4).astype(BF16)


def _merge(da, dn, rest, ga_col0, xs, w_ba, w_bb, w_o, g2, modsel, n_ctx):
    B, T, D = xs.shape
    tm = _pick(math.gcd(n_ctx, T - n_ctx), (256, 128))
    tok, gain, mod = _seg_specs(tm, n_ctx // tm, D)
    wspec = pl.BlockSpec((D, D), lambda b, t: (0, 0))
    ga_blk = ga_col0 // D
    return pl.pallas_call(
        _merge_kernel, grid=(B, T // tm),
        in_specs=[tok, tok,
                  pl.BlockSpec((None, tm, D), lambda b, t: (b, t, ga_blk)),
                  pl.BlockSpec((None, tm, D), lambda b, t: (b, t, ga_blk + 1)),
                  tok, wspec, wspec, wspec, gain, mod],
        out_specs=[tok, tok],
        out_shape=[jax.ShapeDtypeStruct((B, T, D), F32), jax.ShapeDtypeStruct((B, T, D), BF16)],
        compiler_params=_cparams("parallel", "parallel"), name="merge",
    )(da, dn, rest, rest, xs, w_ba, w_bb, w_o, g2.reshape(1, D), modsel)


def _merge_sort_pairs(n):
    pairs = []
    p = 1
    while p < n:
        k = p
        while k >= 1:
            for j in range(k % p, n - k, 2 * k):
                for i in range(min(k, n - j - k)):
                    if (i + j) // (2 * p) == (i + j + k) // (2 * p):
                        pairs.append((i + j, i + j + k))
            k //= 2
        p *= 2
    return pairs


def _desc_tops(x, n):
    ng = x.shape[0] // SUBLANES
    g = [x[v * SUBLANES:(v + 1) * SUBLANES] for v in range(ng)]
    for i, j in _merge_sort_pairs(pl.next_power_of_2(ng)):
        if j < ng:
            g[i], g[j] = jnp.maximum(g[i], g[j]), jnp.minimum(g[i], g[j])
    neg = jnp.full_like(g[0], -jnp.inf)
    tops = []
    for r in range(n):
        m = jnp.max(g[0], axis=0, keepdims=True)
        tops.append(m)
        hit = g[0] == m
        for v in range(min(ng, n - r)):
            g[v] = jnp.where(hit, g[v + 1] if v + 1 < ng else neg, g[v])
    return tops


def _peer_prep_kernel(h_ref, wq_ref, keys_ref, a_ref, b_ref, kap_ref):
    tm = h_ref.shape[0]
    q = jnp.dot(h_ref[...], wq_ref[...], preferred_element_type=F32).astype(BF16)
    for h in range(PK_HEADS):
        st = [lax.dot_general(keys_ref[2 * h + p], q[:, (2 * h + p) * PK_HALF:(2 * h + p + 1) * PK_HALF],
                              NT_DIMS, preferred_element_type=F32) for p in range(2)]
        ta = _desc_tops(st[0], PK_TOPK + 1)
        tb = _desc_tops(st[1], PK_TOPK + 1)
        tbs = jnp.concatenate(tb[:PK_TOPK], axis=0)
        tb8 = tbs[0:SUBLANES]
        row8 = lax.broadcasted_iota(jnp.int32, tb8.shape, 0)
        pieces = [ta[0] + tbs, ta[1] + tb8]
        for r in range(2, SUBLANES):
            pieces.append(jnp.where(row8 < PK_TOPK // (r + 1), ta[r] + tb8, -jnp.inf))
        pieces.append(jnp.concatenate(ta[SUBLANES:PK_TOPK], axis=0) + tb[0])
        cand = jnp.concatenate(pieces, axis=0)
        best = _desc_tops(cand, PK_TOPK + 1)
        mx = best[0]
        zsum = best[0] * 0.0
        for r in range(PK_TOPK):
            zsum = zsum + jnp.exp(best[r] - mx)
        nxt = jnp.maximum(best[PK_TOPK], jnp.maximum(ta[PK_TOPK] + tb[0], ta[0] + tb[PK_TOPK]))
        thr = 0.5 * (best[PK_TOPK - 1] + nxt)
        rz = GELU_FOLD / zsum
        a_ref[h] = jnp.exp(st[0] - ta[0])
        b_ref[h] = (jnp.exp(st[1] - tb[0]) * rz).astype(b_ref.dtype)
        kap_ref[h:h + 1, :] = jnp.exp(thr - mx) * rz


def _peer_prep(h2, wq, keys):
    M, D = h2.shape
    tm = _pick(M, (256, 128))
    nk = 2 * PK_HEADS
    return pl.pallas_call(
        _peer_prep_kernel, grid=(M // tm,),
        in_specs=[pl.BlockSpec((tm, D), lambda i: (i, 0)),
                  pl.BlockSpec((D, nk * PK_HALF), lambda i: (0, 0)),
                  pl.BlockSpec((nk, N_KEYS, PK_HALF), lambda i: (0, 0, 0))],
        out_specs=[pl.BlockSpec((PK_HEADS, N_KEYS, tm), lambda i: (0, 0, i)),
                   pl.BlockSpec((PK_HEADS, N_KEYS, tm), lambda i: (0, 0, i)),
                   pl.BlockSpec((PK_HEADS, tm), lambda i: (0, i))],
        out_shape=[jax.ShapeDtypeStruct((PK_HEADS, N_KEYS, M), F32),
                   jax.ShapeDtypeStruct((PK_HEADS, N_KEYS, M), BF16),
                   jax.ShapeDtypeStruct((PK_HEADS, M), F32)],
        compiler_params=_cparams("parallel"), name="peer_prep",
    )(h2, wq, keys)


PEER_KEY_ROWS_PER_STEP = 8


def _peer_dense_kernel(h_ref, u_ref, vt_ref, a_ref, b_ref, kap_ref, o_ref, acc_sc, *, sub):
    e = pl.program_id(1)
    tm = h_ref.shape[0]
    eb = u_ref.shape[0]
    ni = eb // N_KEYS

    @pl.when(e == 0)
    def _():
        acc_sc[...] = jnp.zeros(acc_sc.shape, F32)

    def rows16(x):
        x16 = jnp.broadcast_to(x, (2 * SUBLANES, sub)).astype(BF16)
        return jnp.concatenate([x16] * (N_KEYS // (2 * SUBLANES)), axis=0)

    def key_scores(t):
        return lax.dot_general(u_ref[...], h_ref[t * sub:(t + 1) * sub, :], NT_DIMS,
                               preferred_element_type=F32)

    def gates(t):
        tok = slice(t * sub, (t + 1) * sub)
        kaps = [rows16(kap_ref[h:h + 1, tok]) for h in range(PK_HEADS)]
        zero = jnp.zeros((N_KEYS, sub), BF16)
        ws = []
        for il in range(ni):
            w = zero
            for h in range(PK_HEADS):
                prod = b_ref[h, :, tok] * rows16(a_ref[h, il:il + 1, tok])
                w = w + jnp.where(prod >= kaps[h], prod, zero)
            ws.append(w)
        return ws

    n_slab = tm // sub
    sc_next = key_scores(0)
    w_next = gates(0)
    for t in range(n_slab):
        sc, ws = sc_next, w_next
        if t + 1 < n_slab:
            sc_next = key_scores(t + 1)
        was = []
        for il in range(ni):
            t_ = sc[il * N_KEYS:(il + 1) * N_KEYS, :]
            was.append(ws[il] * (t_ + t_ * lax.erf(t_)).astype(BF16))
        acc_sc[:, t * sub:(t + 1) * sub] += jnp.dot(vt_ref[...], jnp.concatenate(was, axis=0),
                                                    preferred_element_type=F32)
        if t + 1 < n_slab:
            w_next = gates(t + 1)

    @pl.when(e == pl.num_programs(1) - 1)
    def _():
        o_ref[...] = acc_sc[...].T


def _peer_dense(h2, u, vt, a, b, kap):
    M, D = h2.shape
    E = u.shape[0]
    tm = _pick(M, (1024, 512, 256, 128))
    sub = _pick(tm, (512, 256, 128))
    ni = PEER_KEY_ROWS_PER_STEP
    eb = ni * N_KEYS
    kern = functools.partial(_peer_dense_kernel, sub=sub)
    return pl.pallas_call(
        kern, grid=(M // tm, E // eb),
        in_specs=[pl.BlockSpec((tm, D), lambda i, e: (i, 0)),
                  pl.BlockSpec((eb, D), lambda i, e: (e, 0)),
                  pl.BlockSpec((D, eb), lambda i, e: (0, e)),
                  pl.BlockSpec((PK_HEADS, ni, tm), lambda i, e: (0, e, i)),
                  pl.BlockSpec((PK_HEADS, N_KEYS, tm), lambda i, e: (0, 0, i)),
                  pl.BlockSpec((PK_HEADS, tm), lambda i, e: (0, i))],
        out_specs=pl.BlockSpec((tm, D), lambda i, e: (i, 0)),
        out_shape=jax.ShapeDtypeStruct((M, D), F32),
        scratch_shapes=[pltpu.VMEM((D, tm), F32)],
        compiler_params=_cparams("parallel", "arbitrary"), name="peer_dense",
    )(h2, u, vt, a, b, kap)


def _rope_tables(n_ctx, S):
    n_freq = DA_DIM // 4
    n_rows = S // GRID_W
    row = jnp.repeat(jnp.arange(n_rows, dtype=F32), GRID_W)
    col = jnp.tile(jnp.arange(GRID_W, dtype=F32), n_rows)
    inv = ROPE_BASE ** (-jnp.arange(n_freq, dtype=F32) / n_freq)
    ang = jnp.concatenate([row[:, None] * inv, col[:, None] * inv], axis=-1)
    cos = jnp.concatenate([jnp.ones((n_ctx, DA_DIM // 2), F32), jnp.cos(ang)], axis=0)
    sin = jnp.concatenate([jnp.zeros((n_ctx, DA_DIM // 2), F32), jnp.sin(ang)], axis=0)
    return jnp.concatenate([cos] * 4, axis=1), jnp.concatenate([-sin, sin, -sin, sin], axis=1)


def _deinterleave_perm():
    idx = np.arange(DA_HEADS * 2 * DA_DIM).reshape(DA_HEADS * 2, DA_DIM // 2, 2)
    return np.concatenate([idx[:, :, 0], idx[:, :, 1]], axis=1).reshape(-1)


def kernel(x, c, ctx, c_ctx, w_ada, b_ada, norm1_g, norm2_g, w_in, da_lambda, da_subln, dn_conv, dn_a_log,
           dn_dt_bias, dn_norm, w_branch_a, w_branch_b, w_out, peer_wq, peer_keys, peer_u, peer_v, final_g):
    B, S, D = x.shape
    n_ctx = ctx.shape[1]
    T = n_ctx + S
    L = w_ada.shape[0]
    M = B * T
    assert B < SUBLANES and S % GRID_W == 0 and n_ctx % DN_CHUNK == 0 and S % DN_CHUNK == 0
    da_qk = DA_HEADS * 2 * DA_DIM
    da_w = DA_HEADS * DA_VDIM
    dn_qkv = 3 * DN_HEADS * DN_DK
    dn_w = DN_HEADS * DN_DV
    nh2 = 2 * DN_HEADS

    c_all = jnp.zeros((SUBLANES, D), F32).at[:B].set(c).at[B].set(c_ctx)
    mod = _ada(c_all, w_ada, b_ada).reshape(L, SUBLANES, 6, D)
    modsel = jnp.stack([jnp.broadcast_to(mod[:, B:B + 1], (L, B, 6, D)), mod[:, :B]], axis=2)

    cos_t, sin_t = _rope_tables(n_ctx, S)
    cos_m = jnp.tile(cos_t, (B, 1))
    sin_m = jnp.tile(sin_t, (B, 1))
    perm = _deinterleave_perm()

    xs = jnp.concatenate([ctx, x], axis=1)
    f = None
    for l in range(L):
        lam_init = 0.8 - 0.6 * math.exp(-0.3 * l)
        w = w_in[l]
        o = 0
        w_q = w[:, o:o + da_qk][:, perm]; o += da_qk
        w_k = w[:, o:o + da_qk][:, perm]; o += da_qk
        w_v = w[:, o:o + da_w]; o += da_w
        w_dn = w[:, o:o + dn_qkv]; o += dn_qkv
        w_z = w[:, o:o + dn_w]; o += dn_w
        w_b = w[:, o:o + nh2]; o += nh2
        w_a = w[:, o:o + nh2]; o += nh2
        w_ga = w[:, o:o + D]; o += D
        w_gb = w[:, o:o + D]; o += D
        w_qk = jnp.concatenate([w_q, w_k], axis=1).astype(BF16)
        w_rest = jnp.concatenate([w_dn, w_z, w_ga, w_gb], axis=1).astype(BF16)
        dn_col0, z_col0, ga_col0 = 0, dn_qkv, dn_qkv + dn_w
        w_ba_pad = jnp.zeros((D, LANES), F32).at[:, :nh2].set(w_b).at[:, nh2:2 * nh2].set(w_a).astype(BF16)
        gate_par = (jnp.zeros((SUBLANES, LANES), F32)
                    .at[0, nh2:2 * nh2].set(dn_a_log[l].reshape(-1))
                    .at[1, nh2:2 * nh2].set(dn_dt_bias[l].reshape(-1)))

        if l == 0:
            h = _modulate(xs, norm1_g[l], modsel[l], n_ctx)
        else:
            xs, h = _resid_modulate(xs, f, modsel[l - 1], norm1_g[l], modsel[l], n_ctx)
        hm = h.reshape(M, D)

        qk = _proj(hm, w_qk, out_dtype=BF16, kernel=_proj_rope_kernel, extra=(cos_m, sin_m),
                   extra_specs=lambda tm: [pl.BlockSpec((tm, LANES), lambda j, i: (i, 0))] * 2,
                   name="proj_qk_rope").reshape(B, T, 2 * da_qk)
        rest = _proj(hm, w_rest, out_dtype=BF16, extra_specs=lambda tm: [], name="proj_rest").reshape(B, T, -1)
        gates = _proj(hm, w_ba_pad, out_dtype=F32, kernel=_proj_gates_kernel, extra=(gate_par,),
                      extra_specs=lambda tm: [pl.BlockSpec((SUBLANES, LANES), lambda j, i: (0, 0))],
                      name="proj_gates").reshape(B, T, LANES)

        vt = _proj_t(hm, w_v.T.astype(BF16), out_dtype=BF16, name="proj_v_t")

        da = _attention(qk, vt, da_lambda[l], da_subln[l], n_ctx, lam_init)
        dnq = _dn_prep(rest, dn_conv[l], n_ctx, dn_col0)
        dn = _gdn(dnq, gates, rest, z_col0, dn_norm[l], n_ctx)

        xs, h2 = _merge(da, dn, rest, ga_col0, xs, w_branch_a[l].astype(BF16), w_branch_b[l].astype(BF16),
                        w_out[l].astype(BF16), norm2_g[l], modsel[l], n_ctx)
        h2m = h2.reshape(M, D)
        pa, pb, kap = _peer_prep(h2m, peer_wq[l].astype(BF16),
                                 peer_keys[l].reshape(2 * PK_HEADS, N_KEYS, PK_HALF).astype(BF16))
        f = _peer_dense(h2m, (peer_u[l] * GELU_FOLD).astype(BF16), peer_v[l].T.astype(BF16),
                        pa, pb, kap).reshape(B, T, D)

    return _final(xs, f, modsel[L - 1], final_g, n_ctx)
```

```python
import functools
import math

import jax
import jax.numpy as jnp
import numpy as np
from jax import lax
from jax.experimental import pallas as pl
from jax.experimental.pallas import tpu as pltpu

F32 = jnp.float32
BF16 = jnp.bfloat16
HIGHEST = lax.Precision.HIGHEST

GRID_W = 64
EPS = 1e-6
ROPE_BASE = 10000.0
DA_HEADS = 8
DA_DIM = 64
DA_VDIM = 2 * DA_DIM
DN_HEADS = 8
DN_DK = 128
DN_DV = 128
DN_CONV = 5
DN_CHUNK = 64
PK_HEADS = 8
N_KEYS = 128
PK_TOPK = 16
PK_HALF = 128
GELU_FOLD = 2.0 ** -0.5

LANES = 128
SUBLANES = 8
VMEM_LIMIT_BYTES = 56 * 1024 * 1024

NT_DIMS = (((1,), (1,)), ((), ()))


def _cparams(*sem, flags=None):
    return pltpu.CompilerParams(dimension_semantics=sem, vmem_limit_bytes=VMEM_LIMIT_BYTES, flags=flags)


def _pick(n, cands):
    for c in cands:
        if n % c == 0:
            return c
    raise ValueError(f"no tile in {cands} divides {n}")


def _silu(x):
    return x * jax.nn.sigmoid(x)


def _ada_kernel(c_ref, w_ref, b_ref, o_ref):
    sc = _silu(c_ref[...])
    o_ref[0] = jnp.dot(sc, w_ref[0], preferred_element_type=F32, precision=HIGHEST) + b_ref[0]


def _ada(c_all, w_ada, b_ada):
    L, D, N = w_ada.shape
    tn = _pick(N, (1536, 1024, 512, 256, 128))
    return pl.pallas_call(
        _ada_kernel,
        grid=(L, N // tn),
        in_specs=[pl.BlockSpec((SUBLANES, D), lambda l, j: (0, 0)),
                  pl.BlockSpec((1, D, tn), lambda l, j: (l, 0, j)),
                  pl.BlockSpec((1, 1, tn), lambda l, j: (l, 0, j))],
        out_specs=pl.BlockSpec((1, SUBLANES, tn), lambda l, j: (l, 0, j)),
        out_shape=jax.ShapeDtypeStruct((L, SUBLANES, N), F32),
        compiler_params=_cparams("parallel", "parallel"),
        name="ada",
    )(c_all, w_ada, b_ada.reshape(L, 1, N))


def _rms_mod(x, g, m, shift_row, scale_row):
    ms = jnp.mean(x * x, axis=-1, keepdims=True)
    y = x * lax.rsqrt(ms + EPS) * g
    return y * (1.0 + m[scale_row:scale_row + 1]) + m[shift_row:shift_row + 1]


def _modulate_kernel(x_ref, g_ref, m_ref, h_ref):
    h_ref[...] = _rms_mod(x_ref[...], g_ref[...], m_ref[...], 0, 1).astype(BF16)


def _resid_modulate_kernel(x_ref, f_ref, mp_ref, g_ref, m_ref, xo_ref, h_ref):
    x = x_ref[...] + mp_ref[...][5:6] * f_ref[...]
    xo_ref[...] = x
    h_ref[...] = _rms_mod(x, g_ref[...], m_ref[...], 0, 1).astype(BF16)


def _seg_specs(tm, n_ctx_tiles, D):
    tok = pl.BlockSpec((None, tm, D), lambda b, t: (b, t, 0))
    gain = pl.BlockSpec((1, D), lambda b, t: (0, 0))
    mod = pl.BlockSpec((None, None, 6, D), lambda b, t: (b, jnp.where(t >= n_ctx_tiles, 1, 0), 0, 0))
    return tok, gain, mod


def _modulate(xs, g, modsel, n_ctx):
    B, T, D = xs.shape
    tm = _pick(math.gcd(n_ctx, T - n_ctx), (256, 128))
    tok, gain, mod = _seg_specs(tm, n_ctx // tm, D)
    return pl.pallas_call(
        _modulate_kernel, grid=(B, T // tm),
        in_specs=[tok, gain, mod], out_specs=tok,
        out_shape=jax.ShapeDtypeStruct((B, T, D), BF16),
        compiler_params=_cparams("parallel", "parallel"), name="modulate",
    )(xs, g.reshape(1, D), modsel)


def _resid_modulate(xs, f, modsel_prev, g, modsel, n_ctx):
    B, T, D = xs.shape
    tm = _pick(math.gcd(n_ctx, T - n_ctx), (256, 128))
    tok, gain, mod = _seg_specs(tm, n_ctx // tm, D)
    return pl.pallas_call(
        _resid_modulate_kernel, grid=(B, T // tm),
        in_specs=[tok, tok, mod, gain, mod], out_specs=[tok, tok],
        out_shape=[jax.ShapeDtypeStruct((B, T, D), F32), jax.ShapeDtypeStruct((B, T, D), BF16)],
        compiler_params=_cparams("parallel", "parallel"), name="resid_modulate",
    )(xs, f, modsel_prev, g.reshape(1, D), modsel)


def _final_kernel(x_ref, f_ref, mp_ref, g_ref, o_ref):
    x = x_ref[...] + mp_ref[...][5:6] * f_ref[...]
    ms = jnp.mean(x * x, axis=-1, keepdims=True)
    o_ref[...] = x * lax.rsqrt(ms + EPS) * g_ref[...]


def _final(xs, f_lat, modsel_prev, g, n_ctx):
    B, T, D = xs.shape
    S = T - n_ctx
    tm = _pick(math.gcd(n_ctx, S), (256, 128))
    off = n_ctx // tm
    tok_in = pl.BlockSpec((None, tm, D), lambda b, t: (b, t + off, 0))
    tok_lat = pl.BlockSpec((None, tm, D), lambda b, t: (b, t, 0))
    return pl.pallas_call(
        _final_kernel, grid=(B, S // tm),
        in_specs=[tok_in, tok_lat,
                  pl.BlockSpec((None, None, 6, D), lambda b, t: (b, 1, 0, 0)),
                  pl.BlockSpec((1, D), lambda b, t: (0, 0))],
        out_specs=pl.BlockSpec((None, tm, D), lambda b, t: (b, t, 0)),
        out_shape=jax.ShapeDtypeStruct((B, S, D), F32),
        compiler_params=_cparams("parallel", "parallel"), name="final_norm",
    )(xs, f_lat, modsel_prev, g.reshape(1, D))


def _proj_kernel(h_ref, w_ref, o_ref):
    o_ref[...] = jnp.dot(h_ref[...], w_ref[...], preferred_element_type=F32).astype(o_ref.dtype)


def _proj_rope_kernel(h_ref, w_ref, cos_ref, sin_ref, o_ref):
    acc = jnp.dot(h_ref[...], w_ref[...], preferred_element_type=F32)
    tn = acc.shape[1]
    half = DA_DIM // 2
    lane = lax.broadcasted_iota(jnp.int32, acc.shape, 1)
    partner = jnp.where((lane & (DA_DIM - 1)) < half,
                        pltpu.roll(acc, tn - half, axis=1),
                        pltpu.roll(acc, half, axis=1))
    reps = tn // LANES
    cos = jnp.concatenate([cos_ref[...]] * reps, axis=1)
    sin = jnp.concatenate([sin_ref[...]] * reps, axis=1)
    y = acc * cos + partner * sin
    scale = jnp.where(pl.program_id(0) == 0, DA_DIM ** -0.5 * math.log2(math.e), 1.0).astype(F32)
    o_ref[...] = (y * scale).astype(o_ref.dtype)


def _proj_gates_kernel(h_ref, w_ref, p_ref, o_ref):
    acc = jnp.dot(h_ref[...], w_ref[...], preferred_element_type=F32)
    p = p_ref[...]
    lane = lax.broadcasted_iota(jnp.int32, acc.shape, 1)
    beta = jax.nn.sigmoid(acc)
    g = -jnp.exp(p[0:1]) * jax.nn.softplus(acc + p[1:2])
    o_ref[...] = jnp.where(lane < 2 * DN_HEADS, beta, jnp.where(lane < 4 * DN_HEADS, g, 0.0))


def _proj(h, w, *, out_dtype, kernel=_proj_kernel, extra=(), extra_specs=(), name="proj"):
    M, D = h.shape
    N = w.shape[1]
    tm = _pick(M, (1024, 512, 256, 128))
    tn = _pick(N, (1024, 512, 256, 128))
    return pl.pallas_call(
        kernel, grid=(N // tn, M // tm),
        in_specs=[pl.BlockSpec((tm, D), lambda j, i: (i, 0)),
                  pl.BlockSpec((D, tn), lambda j, i: (0, j))] + list(extra_specs(tm)),
        out_specs=pl.BlockSpec((tm, tn), lambda j, i: (i, j)),
        out_shape=jax.ShapeDtypeStruct((M, N), out_dtype),
        compiler_params=_cparams("parallel", "parallel"), name=name,
    )(h, w, *extra)


def _proj_t_kernel(h_ref, wt_ref, o_ref):
    o_ref[...] = lax.dot_general(wt_ref[...], h_ref[...], NT_DIMS, preferred_element_type=F32).astype(o_ref.dtype)


def _proj_t(h, wt, *, out_dtype, name):
    M, D = h.shape
    N = wt.shape[0]
    tm = _pick(M, (1024, 512, 256, 128))
    tn = _pick(N, (1024, 512, 256, 128))
    return pl.pallas_call(
        _proj_t_kernel, grid=(N // tn, M // tm),
        in_specs=[pl.BlockSpec((tm, D), lambda j, i: (i, 0)),
                  pl.BlockSpec((tn, D), lambda j, i: (j, 0))],
        out_specs=pl.BlockSpec((tn, tm), lambda j, i: (j, i)),
        out_shape=jax.ShapeDtypeStruct((N, M), out_dtype),
        compiler_params=_cparams("parallel", "parallel"), name=name,
    )(h, wt)


ATTN_HEADS_PER_STEP = 4


def _attn_kernel(lam_ref, q_ref, k_ref, vt_ref, g_ref, o_ref, *, n_ctx, tk, lam_init):
    tq = q_ref.shape[0]
    T = k_ref.shape[0]
    qi = pl.program_id(2)
    heads = range(q_ref.shape[1] // DA_VDIM)
    cols = [slice(hh * DA_VDIM, (hh + 1) * DA_VDIM) for hh in heads]
    qqs = []
    for hh in heads:
        q = q_ref[:, cols[hh]]
        lane = lax.broadcasted_iota(jnp.int32, q.shape, 1)
        zero = jnp.zeros_like(q)
        qqs.append(jnp.concatenate([jnp.where(lane < DA_DIM, q, zero), jnp.where(lane >= DA_DIM, q, zero)], axis=0))

    def scores(hh, r0, rows):
        return lax.dot_general(k_ref[r0:r0 + rows, cols[hh]], qqs[hh], NT_DIMS,
                               preferred_element_type=F32)

    def step(hh, state, s, r0, rows):
        m_prev, l_prev, acc = state
        m_new = jnp.maximum(m_prev, jnp.max(s, axis=0, keepdims=True))
        alpha = jnp.exp2(m_prev - m_new)
        p = jnp.exp2(s - m_new)
        pv = jnp.dot(vt_ref[cols[hh], r0:r0 + rows], p.astype(BF16), preferred_element_type=F32)
        return m_new, alpha * l_prev + jnp.sum(p, axis=0, keepdims=True), alpha * acc + pv

    lam = lam_ref[...]
    lam_val = (jnp.exp(jnp.sum(lam[0:1] * lam[1:2], axis=-1, keepdims=True))
               - jnp.exp(jnp.sum(lam[2:3] * lam[3:4], axis=-1, keepdims=True)) + lam_init)

    def finish(hh, state):
        _, l_fin, acc = state
        o_all = acc / l_fin
        o = o_all[:, 0:tq] - lam_val * o_all[:, tq:2 * tq]
        ms = jnp.mean(o * o, axis=0, keepdims=True)
        y = o * lax.rsqrt(ms + EPS) * (1.0 - lam_init)
        o_ref[:, cols[hh]] = (y.T * g_ref[...]).astype(o_ref.dtype)

    init = (jnp.full((1, 2 * tq), -jnp.inf, F32), jnp.zeros((1, 2 * tq), F32), jnp.zeros((DA_VDIM, 2 * tq), F32))

    def attend(blocks):
        state = [init for _ in heads]
        s_next = [scores(hh, *blocks[0]) for hh in heads]
        for j, (r0, rows) in enumerate(blocks):
            for hh in heads:
                s_cur = s_next[hh]
                if j + 1 < len(blocks):
                    s_next[hh] = scores(hh, *blocks[j + 1])
                state[hh] = step(hh, state[hh], s_cur, r0, rows)
        for hh in heads:
            finish(hh, state[hh])

    @pl.when(qi * tq < n_ctx)
    def _():
        attend([(0, n_ctx)])

    @pl.when(qi * tq >= n_ctx)
    def _():
        attend([(0, n_ctx)] + [(n_ctx + j * tk, tk) for j in range((T - n_ctx) // tk)])


def _attention(qk, vt, lam, subln, n_ctx, lam_init):
    B, T, _ = qk.shape
    H = DA_HEADS
    S = T - n_ctx
    tq = _pick(math.gcd(n_ctx, S), (256, 128))
    tk = _pick(S, (1024, 512, 256, 128))
    hps = ATTN_HEADS_PER_STEP
    nb = H // hps
    wblk = hps * DA_VDIM
    kern = functools.partial(_attn_kernel, n_ctx=n_ctx, tk=tk, lam_init=lam_init)
    return pl.pallas_call(
        kern, grid=(B, nb, T // tq),
        in_specs=[pl.BlockSpec((4, DA_DIM), lambda b, h, i: (0, 0)),
                  pl.BlockSpec((None, tq, wblk), lambda b, h, i: (b, i, h)),
                  pl.BlockSpec((None, T, wblk), lambda b, h, i: (b, 0, nb + h)),
                  pl.BlockSpec((wblk, T), lambda b, h, i: (h, b)),
                  pl.BlockSpec((1, DA_VDIM), lambda b, h, i: (0, 0))],
        out_specs=pl.BlockSpec((None, tq, wblk), lambda b, h, i: (b, i, h)),
        out_shape=jax.ShapeDtypeStruct((B, T, H * DA_VDIM), BF16),
        compiler_params=_cparams("parallel", "parallel", "arbitrary"), name="diff_attention",
    )(lam, qk, qk, vt, subln.reshape(1, DA_VDIM))


def _dn_prep_kernel(x_ref, w_ref, o_ref, pad_sc, *, n_ctx, rows):
    T = x_ref.shape[0]
    halo = SUBLANES
    cb = pl.program_id(1)
    zpad = jnp.zeros((halo, LANES), F32)
    pad_sc[0:halo, :] = zpad
    pad_sc[halo + n_ctx:2 * halo + n_ctx, :] = zpad
    pad_sc[2 * halo + T:3 * halo + T, :] = zpad
    pad_sc[halo:halo + n_ctx, :] = x_ref[0:n_ctx, :].astype(F32)
    pad_sc[2 * halo + n_ctx:2 * halo + T, :] = x_ref[n_ctx:T, :].astype(F32)
    w = w_ref[...]
    is_qk = cb < 2 * DN_HEADS
    post = jnp.where(cb < DN_HEADS, DN_DK ** -0.5, 1.0).astype(F32)
    pad = DN_CONV // 2

    for c in range(T // rows):
        r0 = c * rows
        base = r0 + (2 * halo if r0 >= n_ctx else halo)
        y = jnp.zeros((rows, LANES), F32)
        for j in range(DN_CONV):
            y = y + pad_sc[base + (j - pad):base + (j - pad) + rows, :] * w[j:j + 1]
        y = _silu(y)
        yn = y * lax.rsqrt(jnp.sum(y * y, axis=-1, keepdims=True) + EPS) * post
        o_ref[r0:r0 + rows, :] = jnp.where(is_qk, yn, y).astype(o_ref.dtype)


def _dn_prep(rest, conv_w, n_ctx, col0):
    B, T, _ = rest.shape
    nblk = 3 * DN_HEADS
    rows = _pick(math.gcd(n_ctx, T - n_ctx), (256, 128))
    cw = jnp.zeros((SUBLANES, nblk * LANES), F32).at[:DN_CONV].set(conv_w)
    kern = functools.partial(_dn_prep_kernel, n_ctx=n_ctx, rows=rows)
    return pl.pallas_call(
        kern, grid=(B, nblk),
        in_specs=[pl.BlockSpec((None, T, LANES), lambda b, c: (b, 0, col0 // LANES + c)),
                  pl.BlockSpec((SUBLANES, LANES), lambda b, c: (0, c))],
        out_specs=pl.BlockSpec((None, T, LANES), lambda b, c: (b, 0, c)),
        out_shape=jax.ShapeDtypeStruct((B, T, nblk * LANES), BF16),
        scratch_shapes=[pltpu.VMEM((T + 3 * SUBLANES, LANES), F32)],
        compiler_params=_cparams("parallel", "parallel"), name="dn_prep",
    )(rest, cw)


GDN_HEADS_PER_STEP = 2
GDN_CHUNKS_PER_ITER = 8
GDN_INV_PASSES = 1


def _split_dot(a, b, passes):
    a_hi = a.astype(BF16)
    b_hi = b.astype(BF16)
    out = jnp.dot(a_hi, b_hi, preferred_element_type=F32)
    if passes >= 3:
        a_lo = (a - a_hi.astype(F32)).astype(BF16)
        b_lo = (b - b_hi.astype(F32)).astype(BF16)
        out = out + jnp.dot(a_hi, b_lo, preferred_element_type=F32) + jnp.dot(a_lo, b_hi, preferred_element_type=F32)
    return out


def _gdn_kernel(q_ref, k_ref, v_ref, gt_ref, z_ref, ng_ref, o_ref,
                w_sc, qg_sc, kt_sc, a_sc, u_sc, egl_sc, oacc_sc, s_sc, *, n_ctx):
    C = DN_CHUNK
    C2 = 2 * C
    T = q_ref.shape[0]
    nc = T // C
    ncc = n_ctx // C
    hp = pl.program_id(1)

    row = lax.broadcasted_iota(jnp.int32, (C2, C2), 0)
    col = lax.broadcasted_iota(jnp.int32, (C2, C2), 1)
    fwd_row = row < C
    rc_xor = row ^ col
    ahead = (col - row) * jnp.where(fwd_row, 1, -1)
    same_dir = rc_xor < C
    incl = same_dir & (ahead <= 0)
    strict = same_dir & (ahead < 0)
    eye = (row == col).astype(F32)
    mcs = jnp.where(incl, 1.0, 0.0).astype(BF16)

    def p1_load(c, s):
        r0 = pl.multiple_of(c * C, C)
        head = hp * GDN_HEADS_PER_STEP + s
        lo, hi = s * LANES, (s + 1) * LANES
        k = k_ref[pl.ds(r0, C), lo:hi]
        q = q_ref[pl.ds(r0, C), lo:hi]
        v = v_ref[pl.ds(r0, C), lo:hi]
        kk = jnp.concatenate([k, k], axis=0)
        qq = jnp.concatenate([q, q], axis=0)
        vf = jnp.concatenate([v, v], axis=0).astype(F32)
        x = gt_ref[pl.ds(r0, C), :]
        x2 = jnp.concatenate([x, x], axis=0)
        bsel = jnp.where(fwd_row, head, DN_HEADS + head)
        gsel = bsel + 2 * DN_HEADS
        beta = jnp.sum(jnp.where(col == bsel, x2, 0.0), axis=-1, keepdims=True)
        glog = jnp.sum(jnp.where(col == gsel, x2, 0.0), axis=-1, keepdims=True)
        return dict(c=c, s=s, kk=kk, qq=qq, vf=vf, beta=beta, glog=glog)

    def p1_prep(d):
        g_rem = jnp.broadcast_to(d["glog"], (C2, C2))
        gcum = jnp.zeros((C2, C2), F32)
        for _ in range(3):
            piece = g_rem.astype(BF16)
            gcum = gcum + jnp.dot(mcs, piece, preferred_element_type=F32)
            g_rem = g_rem - piece.astype(F32)
        decay = jnp.exp(jnp.where(incl, gcum - gcum.T, -jnp.inf))
        kkt = lax.dot_general(d["kk"], d["kk"], NT_DIMS, preferred_element_type=F32)
        qkt = lax.dot_general(d["qq"], d["kk"], NT_DIMS, preferred_element_type=F32)
        d.update(gcum=gcum, decay=decay, qkt=qkt, lmat=jnp.where(strict, d["beta"] * kkt * decay, 0.0), tinv=eye)

    def p1_finish(d):
        c, s, gcum, beta = d["c"], d["s"], d["gcum"], d["beta"]
        kf = d["kk"].astype(F32)
        eg = jnp.exp(gcum)
        rhs = jnp.concatenate([(d["vf"] * beta).astype(BF16), (kf * beta * eg).astype(BF16)], axis=1)
        uw = jnp.dot(d["tinv"].astype(BF16), rhs, preferred_element_type=F32)
        glast = jnp.where(fwd_row, gcum[C - 1:C, :], gcum[C:C + 1, :])
        ktail = kf * jnp.exp(glast - gcum)
        u_sc[s, c] = uw[:, 0:LANES]
        w_sc[s, c] = uw[:, LANES:2 * LANES].astype(BF16)
        qg_sc[s, c] = (d["qq"].astype(F32) * eg).astype(BF16)
        a_sc[s, c] = (d["qkt"] * d["decay"]).astype(BF16)
        kt_sc[s, c] = ktail.T.astype(BF16)
        egl_sc[s, c] = jnp.exp(jnp.concatenate([jnp.broadcast_to(gcum[C - 1:C, :], (4, LANES)),
                                                jnp.broadcast_to(gcum[C:C + 1, :], (4, LANES))], axis=0))

    def p1_group(c0, cpi):
        probs = [p1_load(c0 + j, s) for j in range(cpi) for s in range(GDN_HEADS_PER_STEP)]
        for d in probs:
            p1_prep(d)
        for lvl in range(int(math.log2(C))):
            for d in probs:
                d["x"] = _split_dot(d["tinv"], jnp.where((rc_xor >> lvl) == 1, d["lmat"], 0.0), GDN_INV_PASSES)
            for d in probs:
                d["tinv"] = d["tinv"] - _split_dot(d["x"], d["tinv"], GDN_INV_PASSES)
        for d in probs:
            p1_finish(d)

    for c_lo, n in ((0, ncc), (ncc, nc - ncc)):
        cpi = _pick(n, tuple(c for c in (8, 4, 2, 1) if c <= GDN_CHUNKS_PER_ITER))

        def p1_body(i, carry, c_lo=c_lo, cpi=cpi):
            p1_group(c_lo + i * cpi, cpi)
            return carry

        lax.fori_loop(0, n // cpi, p1_body, 0)

    oacc_sc[...] = jnp.zeros(oacc_sc.shape, F32)
    s_sc[...] = jnp.zeros(s_sc.shape, F32)
    lane_b = lax.broadcasted_iota(jnp.int32, (LANES, LANES), 1)
    zpad = jnp.zeros((C, LANES), BF16)

    def p2_body(i, carry):
        cf = i
        cb = jnp.where(i < ncc, ncc - 1 - i, nc - 1 + ncc - i)
        rf = pl.multiple_of(cf * C, C)
        rb = pl.multiple_of(cb * C, C)
        heads = range(GDN_HEADS_PER_STEP)
        st = [s_sc[s] for s in heads]
        lhs1 = [jnp.concatenate([
            jnp.concatenate([w_sc[s, cf, 0:C, :], zpad], axis=1),
            jnp.concatenate([zpad, w_sc[s, cb, C:C2, :]], axis=1),
            jnp.concatenate([qg_sc[s, cf, 0:C, :], zpad], axis=1),
            jnp.concatenate([zpad, qg_sc[s, cb, C:C2, :]], axis=1)], axis=0) for s in heads]
        zk = jnp.zeros((LANES, C2), BF16)
        lhs2 = [jnp.concatenate([a_sc[s, cf, 0:C, :], a_sc[s, cb, C:C2, :],
                                 jnp.where(lane_b < C, kt_sc[s, cf], zk),
                                 jnp.where(lane_b >= C, kt_sc[s, cb], zk)], axis=0) for s in heads]
        u = [jnp.concatenate([u_sc[s, cf, 0:C, :], u_sc[s, cb, C:C2, :]], axis=0) for s in heads]
        scale = [jnp.concatenate([jnp.broadcast_to(egl_sc[s, cf][0:1], (DN_DK, DN_DV)),
                                  jnp.broadcast_to(egl_sc[s, cb][4:5], (DN_DK, DN_DV))], axis=0) for s in heads]
        r1 = [jnp.dot(lhs1[s], st[s].astype(BF16), preferred_element_type=F32) for s in heads]
        vnew = [(u[s] - r1[s][0:C2]).astype(BF16) for s in heads]
        r2 = [jnp.dot(lhs2[s], vnew[s], preferred_element_type=F32) for s in heads]
        for s in heads:
            s_sc[s] = st[s] * scale[s] + r2[s][C2:C2 + 2 * DN_DK]
        for s in heads:
            oacc_sc[s, pl.ds(rf, C), :] += r1[s][C2:C2 + C] + r2[s][0:C]
            oacc_sc[s, pl.ds(rb, C), :] += r1[s][C2 + C:2 * C2] + r2[s][C:C2]
        return carry

    lax.fori_loop(0, nc, p2_body, 0)

    ng = ng_ref[...]
    rows = _pick(T, (256, 128))

    def fin_body(c, carry):
        r0 = pl.multiple_of(c * rows, rows)
        for s in range(GDN_HEADS_PER_STEP):
            o = oacc_sc[s, pl.ds(r0, rows), :]
            zf = z_ref[pl.ds(r0, rows), s * LANES:(s + 1) * LANES].astype(F32)
            ms = jnp.mean(o * o, axis=-1, keepdims=True)
            o_ref[pl.ds(r0, rows), s * LANES:(s + 1) * LANES] = (o * lax.rsqrt(ms + EPS) * ng * _silu(zf)).astype(o_ref.dtype)
        return carry

    lax.fori_loop(0, T // rows, fin_body, 0)


def _gdn(dnq, gates, rest, z_col0, norm_g, n_ctx):
    B, T, _ = dnq.shape
    H = DN_HEADS
    hps = GDN_HEADS_PER_STEP
    wblk = hps * LANES
    nc = T // DN_CHUNK
    C2 = 2 * DN_CHUNK
    kern = functools.partial(_gdn_kernel, n_ctx=n_ctx)
    nb = H // hps
    once = dict(pipeline_mode=pl.Buffered(1))
    return pl.pallas_call(
        kern, grid=(B, nb),
        in_specs=[pl.BlockSpec((None, T, wblk), lambda b, h: (b, 0, h), **once),
                  pl.BlockSpec((None, T, wblk), lambda b, h: (b, 0, nb + h), **once),
                  pl.BlockSpec((None, T, wblk), lambda b, h: (b, 0, 2 * nb + h), **once),
                  pl.BlockSpec((None, T, LANES), lambda b, h: (b, 0, 0), **once),
                  pl.BlockSpec((None, T, wblk), lambda b, h: (b, 0, z_col0 // wblk + h), **once),
                  pl.BlockSpec((1, DN_DV), lambda b, h: (0, 0))],
        out_specs=pl.BlockSpec((None, T, wblk), lambda b, h: (b, 0, h)),
        out_shape=jax.ShapeDtypeStruct((B, T, H * DN_DV), BF16),
        scratch_shapes=[pltpu.VMEM((hps, nc, C2, LANES), BF16),
                        pltpu.VMEM((hps, nc, C2, LANES), BF16),
                        pltpu.VMEM((hps, nc, LANES, C2), BF16),
                        pltpu.VMEM((hps, nc, C2, C2), BF16),
                        pltpu.VMEM((hps, nc, C2, LANES), F32),
                        pltpu.VMEM((hps, nc, SUBLANES, LANES), F32),
                        pltpu.VMEM((hps, T, LANES), F32),
                        pltpu.VMEM((hps, 2 * DN_DK, DN_DV), F32)],
        compiler_params=_cparams("parallel", "arbitrary"), name="gated_deltanet",
    )(dnq, dnq, dnq, gates, rest, norm_g.reshape(1, DN_DV))


def _merge_kernel(da_ref, dn_ref, ga_ref, gb_ref, x_ref, wa_ref, wb_ref, wo_ref, g_ref, m_ref, xo_ref, h_ref):
    ya = jnp.dot(da_ref[...], wa_ref[...], preferred_element_type=F32)
    yb = jnp.dot(dn_ref[...], wb_ref[...], preferred_element_type=F32)
    y = jax.nn.sigmoid(ga_ref[...].astype(F32)) * ya + jax.nn.sigmoid(gb_ref[...].astype(F32)) * yb
    y2 = jnp.dot(y.astype(BF16), wo_ref[...], preferred_element_type=F32)
    m = m_ref[...]
    x = x_ref[...] + m[2:3] * y2
    xo_ref[...] = x
    h_ref[...] = _rms_mod(x, g_ref[...], m, 3, 4).astype(BF16)


def _merge(da, dn, rest, ga_col0, xs, w_ba, w_bb, w_o, g2, modsel, n_ctx):
    B, T, D = xs.shape
    tm = _pick(math.gcd(n_ctx, T - n_ctx), (256, 128))
    tok, gain, mod = _seg_specs(tm, n_ctx // tm, D)
    wspec = pl.BlockSpec((D, D), lambda b, t: (0, 0))
    ga_blk = ga_col0 // D
    return pl.pallas_call(
        _merge_kernel, grid=(B, T // tm),
        in_specs=[tok, tok,
                  pl.BlockSpec((None, tm, D), lambda b, t: (b, t, ga_blk)),
                  pl.BlockSpec((None, tm, D), lambda b, t: (b, t, ga_blk + 1)),
                  tok, wspec, wspec, wspec, gain, mod],
        out_specs=[tok, tok],
        out_shape=[jax.ShapeDtypeStruct((B, T, D), F32), jax.ShapeDtypeStruct((B, T, D), BF16)],
        compiler_params=_cparams("parallel", "parallel"), name="merge",
    )(da, dn, rest, rest, xs, w_ba, w_bb, w_o, g2.reshape(1, D), modsel)


def _merge_sort_pairs(n):
    pairs = []
    p = 1
    while p < n:
        k = p
        while k >= 1:
            for j in range(k % p, n - k, 2 * k):
                for i in range(min(k, n - j - k)):
                    if (i + j) // (2 * p) == (i + j + k) // (2 * p):
                        pairs.append((i + j, i + j + k))
            k //= 2
        p *= 2
    return pairs


def _desc_tops(x, n):
    ng = x.shape[0] // SUBLANES
    g = [x[v * SUBLANES:(v + 1) * SUBLANES] for v in range(ng)]
    for i, j in _merge_sort_pairs(pl.next_power_of_2(ng)):
        if j < ng:
            g[i], g[j] = jnp.maximum(g[i], g[j]), jnp.minimum(g[i], g[j])
    neg = jnp.full_like(g[0], -jnp.inf)
    tops = []
    for r in range(n):
        m = jnp.max(g[0], axis=0, keepdims=True)
        tops.append(m)
        hit = g[0] == m
        for v in range(min(ng, n - r)):
            g[v] = jnp.where(hit, g[v + 1] if v + 1 < ng else neg, g[v])
    return tops


def _peer_prep_kernel(h_ref, wq_ref, keys_ref, a_ref, b_ref, kap_ref):
    tm = h_ref.shape[0]
    q = jnp.dot(h_ref[...], wq_ref[...], preferred_element_type=F32).astype(BF16)
    for h in range(PK_HEADS):
        st = [lax.dot_general(keys_ref[2 * h + p], q[:, (2 * h + p) * PK_HALF:(2 * h + p + 1) * PK_HALF],
                              NT_DIMS, preferred_element_type=F32) for p in range(2)]
        ta = _desc_tops(st[0], PK_TOPK + 1)
        tb = _desc_tops(st[1], PK_TOPK + 1)
        tbs = jnp.concatenate(tb[:PK_TOPK], axis=0)
        tb8 = tbs[0:SUBLANES]
        row8 = lax.broadcasted_iota(jnp.int32, tb8.shape, 0)
        pieces = [ta[0] + tbs, ta[1] + tb8]
        for r in range(2, SUBLANES):
            pieces.append(jnp.where(row8 < PK_TOPK // (r + 1), ta[r] + tb8, -jnp.inf))
        pieces.append(jnp.concatenate(ta[SUBLANES:PK_TOPK], axis=0) + tb[0])
        cand = jnp.concatenate(pieces, axis=0)
        best = _desc_tops(cand, PK_TOPK + 1)
        mx = best[0]
        zsum = best[0] * 0.0
        for r in range(PK_TOPK):
            zsum = zsum + jnp.exp(best[r] - mx)
        nxt = jnp.maximum(best[PK_TOPK], jnp.maximum(ta[PK_TOPK] + tb[0], ta[0] + tb[PK_TOPK]))
        thr = 0.5 * (best[PK_TOPK - 1] + nxt)
        rz = GELU_FOLD / zsum
        a_ref[h] = jnp.exp(st[0] - ta[0])
        b_ref[h] = (jnp.exp(st[1] - tb[0]) * rz).astype(b_ref.dtype)
        kap_ref[h:h + 1, :] = jnp.exp(thr - mx) * rz


def _peer_prep(h2, wq, keys):
    M, D = h2.shape
    tm = _pick(M, (256, 128))
    nk = 2 * PK_HEADS
    return pl.pallas_call(
        _peer_prep_kernel, grid=(M // tm,),
        in_specs=[pl.BlockSpec((tm, D), lambda i: (i, 0)),
                  pl.BlockSpec((D, nk * PK_HALF), lambda i: (0, 0)),
                  pl.BlockSpec((nk, N_KEYS, PK_HALF), lambda i: (0, 0, 0))],
        out_specs=[pl.BlockSpec((PK_HEADS, N_KEYS, tm), lambda i: (0, 0, i)),
                   pl.BlockSpec((PK_HEADS, N_KEYS, tm), lambda i: (0, 0, i)),
                   pl.BlockSpec((PK_HEADS, tm), lambda i: (0, i))],
        out_shape=[jax.ShapeDtypeStruct((PK_HEADS, N_KEYS, M), F32),
                   jax.ShapeDtypeStruct((PK_HEADS, N_KEYS, M), BF16),
                   jax.ShapeDtypeStruct((PK_HEADS, M), F32)],
        compiler_params=_cparams("parallel"), name="peer_prep",
    )(h2, wq, keys)


PEER_KEY_ROWS_PER_STEP = 8


def _peer_dense_kernel(h_ref, u_ref, vt_ref, a_ref, b_ref, kap_ref, o_ref, acc_sc, *, sub):
    e = pl.program_id(1)
    tm = h_ref.shape[0]
    eb = u_ref.shape[0]
    ni = eb // N_KEYS

    @pl.when(e == 0)
    def _():
        acc_sc[...] = jnp.zeros(acc_sc.shape, F32)

    def rows16(x):
        x16 = jnp.broadcast_to(x, (2 * SUBLANES, sub)).astype(BF16)
        return jnp.concatenate([x16] * (N_KEYS // (2 * SUBLANES)), axis=0)

    def key_scores(t):
        return lax.dot_general(u_ref[...], h_ref[t * sub:(t + 1) * sub, :], NT_DIMS,
                               preferred_element_type=F32)

    def gates(t):
        tok = slice(t * sub, (t + 1) * sub)
        kaps = [rows16(kap_ref[h:h + 1, tok]) for h in range(PK_HEADS)]
        zero = jnp.zeros((N_KEYS, sub), BF16)
        ws = []
        for il in range(ni):
            w = zero
            for h in range(PK_HEADS):
                prod = b_ref[h, :, tok] * rows16(a_ref[h, il:il + 1, tok])
                w = w + jnp.where(prod >= kaps[h], prod, zero)
            ws.append(w)
        return ws

    n_slab = tm // sub
    sc_next = key_scores(0)
    w_next = gates(0)
    for t in range(n_slab):
        sc, ws = sc_next, w_next
        if t + 1 < n_slab:
            sc_next = key_scores(t + 1)
        was = []
        for il in range(ni):
            t_ = sc[il * N_KEYS:(il + 1) * N_KEYS, :]
            was.append(ws[il] * (t_ + t_ * lax.erf(t_)).astype(BF16))
        acc_sc[:, t * sub:(t + 1) * sub] += jnp.dot(vt_ref[...], jnp.concatenate(was, axis=0),
                                                    preferred_element_type=F32)
        if t + 1 < n_slab:
            w_next = gates(t + 1)

    @pl.when(e == pl.num_programs(1) - 1)
    def _():
        o_ref[...] = acc_sc[...].T


def _peer_dense(h2, u, vt, a, b, kap):
    M, D = h2.shape
    E = u.shape[0]
    tm = _pick(M, (1024, 512, 256, 128))
    sub = _pick(tm, (512, 256, 128))
    ni = PEER_KEY_ROWS_PER_STEP
    eb = ni * N_KEYS
    kern = functools.partial(_peer_dense_kernel, sub=sub)
    return pl.pallas_call(
        kern, grid=(M // tm, E // eb),
        in_specs=[pl.BlockSpec((tm, D), lambda i, e: (i, 0)),
                  pl.BlockSpec((eb, D), lambda i, e: (e, 0)),
                  pl.BlockSpec((D, eb), lambda i, e: (0, e)),
                  pl.BlockSpec((PK_HEADS, ni, tm), lambda i, e: (0, e, i)),
                  pl.BlockSpec((PK_HEADS, N_KEYS, tm), lambda i, e: (0, 0, i)),
                  pl.BlockSpec((PK_HEADS, tm), lambda i, e: (0, i))],
        out_specs=pl.BlockSpec((tm, D), lambda i, e: (i, 0)),
        out_shape=jax.ShapeDtypeStruct((M, D), F32),
        scratch_shapes=[pltpu.VMEM((D, tm), F32)],
        compiler_params=_cparams("parallel", "arbitrary"), name="peer_dense",
    )(h2, u, vt, a, b, kap)


def _rope_tables(n_ctx, S):
    n_freq = DA_DIM // 4
    n_rows = S // GRID_W
    row = jnp.repeat(jnp.arange(n_rows, dtype=F32), GRID_W)
    col = jnp.tile(jnp.arange(GRID_W, dtype=F32), n_rows)
    inv = ROPE_BASE ** (-jnp.arange(n_freq, dtype=F32) / n_freq)
    ang = jnp.concatenate([row[:, None] * inv, col[:, None] * inv], axis=-1)
    cos = jnp.concatenate([jnp.ones((n_ctx, DA_DIM // 2), F32), jnp.cos(ang)], axis=0)
    sin = jnp.concatenate([jnp.zeros((n_ctx, DA_DIM // 2), F32), jnp.sin(ang)], axis=0)
    return jnp.concatenate([cos] * 4, axis=1), jnp.concatenate([-sin, sin, -sin, sin], axis=1)


def _deinterleave_perm():
    idx = np.arange(DA_HEADS * 2 * DA_DIM).reshape(DA_HEADS * 2, DA_DIM // 2, 2)
    return np.concatenate([idx[:, :, 0], idx[:, :, 1]], axis=1).reshape(-1)


def kernel(x, c, ctx, c_ctx, w_ada, b_ada, norm1_g, norm2_g, w_in, da_lambda, da_subln, dn_conv, dn_a_log,
           dn_dt_bias, dn_norm, w_branch_a, w_branch_b, w_out, peer_wq, peer_keys, peer_u, peer_v, final_g):
    B, S, D = x.shape
    n_ctx = ctx.shape[1]
    T = n_ctx + S
    L = w_ada.shape[0]
    M = B * T
    assert B < SUBLANES and S % GRID_W == 0 and n_ctx % DN_CHUNK == 0 and S % DN_CHUNK == 0
    da_qk = DA_HEADS * 2 * DA_DIM
    da_w = DA_HEADS * DA_VDIM
    dn_qkv = 3 * DN_HEADS * DN_DK
    dn_w = DN_HEADS * DN_DV
    nh2 = 2 * DN_HEADS

    c_all = jnp.zeros((SUBLANES, D), F32).at[:B].set(c).at[B].set(c_ctx)
    mod = _ada(c_all, w_ada, b_ada).reshape(L, SUBLANES, 6, D)
    modsel = jnp.stack([jnp.broadcast_to(mod[:, B:B + 1], (L, B, 6, D)), mod[:, :B]], axis=2)

    cos_t, sin_t = _rope_tables(n_ctx, S)
    cos_m = jnp.tile(cos_t, (B, 1))
    sin_m = jnp.tile(sin_t, (B, 1))
    perm = _deinterleave_perm()

    xs = jnp.concatenate([ctx, x], axis=1)
    f = None
    for l in range(L):
        lam_init = 0.8 - 0.6 * math.exp(-0.3 * l)
        w = w_in[l]
        o = 0
        w_q = w[:, o:o + da_qk][:, perm]; o += da_qk
        w_k = w[:, o:o + da_qk][:, perm]; o += da_qk
        w_v = w[:, o:o + da_w]; o += da_w
        w_dn = w[:, o:o + dn_qkv]; o += dn_qkv
        w_z = w[:, o:o + dn_w]; o += dn_w
        w_b = w[:, o:o + nh2]; o += nh2
        w_a = w[:, o:o + nh2]; o += nh2
        w_ga = w[:, o:o + D]; o += D
        w_gb = w[:, o:o + D]; o += D
        w_qk = jnp.concatenate([w_q, w_k], axis=1).astype(BF16)
        w_rest = jnp.concatenate([w_dn, w_z, w_ga, w_gb], axis=1).astype(BF16)
        dn_col0, z_col0, ga_col0 = 0, dn_qkv, dn_qkv + dn_w
        w_ba_pad = jnp.zeros((D, LANES), F32).at[:, :nh2].set(w_b).at[:, nh2:2 * nh2].set(w_a).astype(BF16)
        gate_par = (jnp.zeros((SUBLANES, LANES), F32)
                    .at[0, nh2:2 * nh2].set(dn_a_log[l].reshape(-1))
                    .at[1, nh2:2 * nh2].set(dn_dt_bias[l].reshape(-1)))

        if l == 0:
            h = _modulate(xs, norm1_g[l], modsel[l], n_ctx)
        else:
            xs, h = _resid_modulate(xs, f, modsel[l - 1], norm1_g[l], modsel[l], n_ctx)
        hm = h.reshape(M, D)

        qk = _proj(hm, w_qk, out_dtype=BF16, kernel=_proj_rope_kernel, extra=(cos_m, sin_m),
                   extra_specs=lambda tm: [pl.BlockSpec((tm, LANES), lambda j, i: (i, 0))] * 2,
                   name="proj_qk_rope").reshape(B, T, 2 * da_qk)
        rest = _proj(hm, w_rest, out_dtype=BF16, extra_specs=lambda tm: [], name="proj_rest").reshape(B, T, -1)
        gates = _proj(hm, w_ba_pad, out_dtype=F32, kernel=_proj_gates_kernel, extra=(gate_par,),
                      extra_specs=lambda tm: [pl.BlockSpec((SUBLANES, LANES), lambda j, i: (0, 0))],
                      name="proj_gates").reshape(B, T, LANES)

        vt = _proj_t(hm, w_v.T.astype(BF16), out_dtype=BF16, name="proj_v_t")

        da = _attention(qk, vt, da_lambda[l], da_subln[l], n_ctx, lam_init)
        dnq = _dn_prep(rest, dn_conv[l], n_ctx, dn_col0)
        dn = _gdn(dnq, gates, rest, z_col0, dn_norm[l], n_ctx)

        xs, h2 = _merge(da, dn, rest, ga_col0, xs, w_branch_a[l].astype(BF16), w_branch_b[l].astype(BF16),
                        w_out[l].astype(BF16), norm2_g[l], modsel[l], n_ctx)
        last = l == L - 1
        h2m = (h2[:, n_ctx:] if last else h2).reshape(-1, D)
        pa, pb, kap = _peer_prep(h2m, peer_wq[l].astype(BF16),
                                 peer_keys[l].reshape(2 * PK_HEADS, N_KEYS, PK_HALF).astype(BF16))
        f = _peer_dense(h2m, (peer_u[l] * GELU_FOLD).astype(BF16), peer_v[l].T.astype(BF16),
                        pa, pb, kap).reshape(B, -1, D)

    return _final(xs, f, modsel[L - 1], final_g, n_ctx)
```
